```python
import math
import jax
import jax.numpy as jnp
from jax import lax
import numpy as np

D_MODEL = 1024
BATCH = 8
SEQ = 2048
DEPTH = 4

N_MIXERS = 3
D_PLE = 256
NORM_EPS = 1e-6
Q_BLOCK = 128
NEG_INF = -1e30

MLA_HEADS = 16
MLA_Q_RANK = 384
MLA_KV_RANK = 256
MLA_NOPE = 64
MLA_ROPE = 32
MLA_V = 64
ROPE_THETA = 10000.0

DIL_PATTERNS = ((128, 1), (512, 4), (2048, 16))
DIL_GROUPS = len(DIL_PATTERNS)
DIL_HEADS = 16
DIL_HEAD_DIM = 64

REL_BUCKETS = 32
REL_MAX_DIST = 2048

FOX_HEADS = 16
FOX_HEAD_DIM = 64

D_FF = -(-8 * D_MODEL // (3 * 256)) * 256

kernel_name = 'hybrid_mla_dilated_fox_trunk'


def rms_norm(x, g):
    xf = x.astype(jnp.float32)
    y = xf * lax.rsqrt(jnp.mean(xf * xf, axis=-1, keepdims=True) + NORM_EPS)
    return (y * g.astype(jnp.float32)).astype(x.dtype)


def apply_rope(x, positions):
    half = x.shape[-1] // 2
    inv = ROPE_THETA ** (-jnp.arange(half, dtype=jnp.float32) / half)
    ang = positions.astype(jnp.float32)[:, :, None, None] * inv
    cos, sin = jnp.cos(ang), jnp.sin(ang)
    xf = x.astype(jnp.float32)
    x1, x2 = xf[..., :half], xf[..., half:]
    return jnp.concatenate([x1 * cos - x2 * sin, x2 * cos + x1 * sin], axis=-1).astype(x.dtype)


def causal_block_attention(q, k, v, scale, log_forget_cumsum=None):
    s_len = q.shape[1]
    outs = []
    for b0 in range(0, s_len, Q_BLOCK):
        b1 = b0 + Q_BLOCK
        qb, kb, vb = q[:, b0:b1], k[:, :b1], v[:, :b1]
        logits = jnp.einsum('bqhd,bkhd->bhqk', qb, kb).astype(jnp.float32) * scale
        if log_forget_cumsum is not None:
            c_q = jnp.transpose(log_forget_cumsum[:, b0:b1], (0, 2, 1))
            c_k = jnp.transpose(log_forget_cumsum[:, :b1], (0, 2, 1))
            logits = logits + (c_q[..., :, None] - c_k[..., None, :])
        q_idx = b0 + jnp.arange(Q_BLOCK)
        k_idx = jnp.arange(b1)
        causal = k_idx[None, :] <= q_idx[:, None]
        logits = jnp.where(causal, logits, NEG_INF)
        probs = jax.nn.softmax(logits, axis=-1).astype(v.dtype)
        outs.append(jnp.einsum('bhqk,bkhd->bqhd', probs, vb))
    return jnp.concatenate(outs, axis=1)


def t5_bucket(dist):
    max_exact = REL_BUCKETS // 2
    n = jnp.maximum(dist.astype(jnp.float32), 1.0)
    large = max_exact + (jnp.log(n / max_exact) / math.log(REL_MAX_DIST / max_exact)
                         * (REL_BUCKETS - max_exact)).astype(jnp.int32)
    large = jnp.minimum(large, REL_BUCKETS - 1)
    return jnp.where(dist < max_exact, dist, large)


def dilated_branch(q, k, v, dilation, span, bias_table):
    b, s_len, h, dh = q.shape
    sub_len = s_len // dilation
    n_blk = -(-sub_len // Q_BLOCK)
    pad_len = n_blk * Q_BLOCK

    def to_blocks(t):
        t = t.reshape(b, sub_len, dilation, h, dh).transpose(0, 2, 1, 3, 4)
        t = t.reshape(b * dilation, sub_len, h, dh)
        t = jnp.pad(t, ((0, 0), (0, pad_len - sub_len), (0, 0), (0, 0)))
        return t.reshape(b * dilation, n_blk, Q_BLOCK, h, dh)

    def with_previous(t):
        prev = jnp.concatenate([jnp.zeros_like(t[:, :1]), t[:, :-1]], axis=1)
        return jnp.concatenate([prev, t], axis=2)

    qs = to_blocks(q)
    kb = with_previous(to_blocks(k))
    vb = with_previous(to_blocks(v))

    logits = jnp.einsum('bnqhd,bnkhd->bnhqk', qs, kb).astype(jnp.float32) * (dh ** -0.5)
    i = jnp.arange(Q_BLOCK)
    j = jnp.arange(2 * Q_BLOCK)
    rel = Q_BLOCK + i[:, None] - j[None, :]
    bucket = t5_bucket(jnp.clip(rel, 0) * dilation)
    bias = jnp.transpose(bias_table[bucket].astype(jnp.float32), (2, 0, 1))
    key_pos = jnp.arange(n_blk)[:, None] * Q_BLOCK - Q_BLOCK + j[None, :]
    valid = ((rel >= 0) & (rel <= span))[None] & (key_pos >= 0)[:, None, :]
    logits = jnp.where(valid[None, :, None], logits + bias[None, None], NEG_INF)
    lse = jax.nn.logsumexp(logits, axis=-1, keepdims=True)
    probs = jnp.exp(logits - lse).astype(v.dtype)
    o = jnp.einsum('bnhqk,bnkhd->bnqhd', probs, vb)

    o = o.reshape(b * dilation, pad_len, h, dh)[:, :sub_len]
    o = o.reshape(b, dilation, sub_len, h, dh).transpose(0, 2, 1, 3, 4).reshape(b, s_len, h, dh)
    lse = jnp.transpose(lse[..., 0], (0, 1, 3, 2)).reshape(b * dilation, pad_len, h)[:, :sub_len]
    lse = lse.reshape(b, dilation, sub_len, h).transpose(0, 2, 1, 3).reshape(b, s_len, h)
    return o, lse


def mla_mixer(h, positions, w_a, q_norm, kv_norm, w_uq, w_ukv, w_o):
    b, s_len, _ = h.shape
    a = h @ w_a
    c_q = rms_norm(a[..., :MLA_Q_RANK], q_norm)
    c_kv = rms_norm(a[..., MLA_Q_RANK:MLA_Q_RANK + MLA_KV_RANK], kv_norm)
    k_rot = apply_rope(a[..., MLA_Q_RANK + MLA_KV_RANK:][:, :, None, :], positions)
    q = (c_q @ w_uq).reshape(b, s_len, MLA_HEADS, MLA_NOPE + MLA_ROPE)
    q = jnp.concatenate([q[..., :MLA_NOPE], apply_rope(q[..., MLA_NOPE:], positions)], axis=-1)
    kv = (c_kv @ w_ukv).reshape(b, s_len, MLA_HEADS, MLA_NOPE + MLA_V)
    k = jnp.concatenate([kv[..., :MLA_NOPE],
                         jnp.broadcast_to(k_rot, (b, s_len, MLA_HEADS, MLA_ROPE))], axis=-1)
    v = kv[..., MLA_NOPE:]
    o = causal_block_attention(q, k, v, (MLA_NOPE + MLA_ROPE) ** -0.5)
    return o.reshape(b, s_len, MLA_HEADS * MLA_V) @ w_o


def dilated_mixer(h, w_qkv, w_o, rel_bias):
    b, s_len, _ = h.shape
    qkv = (h @ w_qkv).reshape(b, s_len, DIL_GROUPS, 3, DIL_HEADS, DIL_HEAD_DIM)
    table = rel_bias.reshape(REL_BUCKETS, DIL_GROUPS, DIL_HEADS)
    outs, lses = [], []
    for g, (window, dilation) in enumerate(DIL_PATTERNS):
        o, lse = dilated_branch(qkv[:, :, g, 0], qkv[:, :, g, 1], qkv[:, :, g, 2],
                                dilation, window // dilation, table[:, g])
        outs.append(o)
        lses.append(lse)
    alpha = jax.nn.softmax(jnp.stack(lses, axis=0), axis=0)
    o = jnp.sum(alpha[..., None] * jnp.stack(outs, axis=0).astype(jnp.float32), axis=0).astype(h.dtype)
    return o.reshape(b, s_len, DIL_HEADS * DIL_HEAD_DIM) @ w_o


def fox_mixer(h, w_qkvf, b_f, w_o):
    b, s_len, _ = h.shape
    inner = FOX_HEADS * FOX_HEAD_DIM
    a = h @ w_qkvf
    qkv = a[..., :3 * inner].reshape(b, s_len, 3, FOX_HEADS, FOX_HEAD_DIM)
    log_f = jax.nn.log_sigmoid((a[..., 3 * inner:] + b_f).astype(jnp.float32))
    cum = jnp.cumsum(log_f, axis=1)
    o = causal_block_attention(qkv[:, :, 0], qkv[:, :, 1], qkv[:, :, 2], FOX_HEAD_DIM ** -0.5, cum)
    return o.reshape(b, s_len, inner) @ w_o


def swiglu(h, w_in, w_out):
    gu = h @ w_in
    return (jax.nn.silu(gu[..., :D_FF]) * gu[..., D_FF:]) @ w_out


def setup_inputs(seed: int = 0) -> dict:
    key = jax.random.key(seed)
    ks = jax.random.split(key, 20)
    f32 = jnp.float32
    n_a, n_b, n_c = (len(range(m, DEPTH, N_MIXERS)) for m in range(N_MIXERS))

    def dense(k, shape):
        return jax.random.normal(k, shape, f32) * shape[-2] ** -0.5

    def gain(k, shape):
        return 1.0 + 0.1 * jax.random.normal(k, shape, f32)

    x = jax.random.normal(ks[0], (BATCH, SEQ, D_MODEL), f32)
    p = jax.random.normal(ks[1], (DEPTH, BATCH, SEQ, D_PLE), f32)
    offsets = jax.random.randint(ks[2], (BATCH, 1), 0, 4096, jnp.int32)
    positions = (offsets + jnp.arange(SEQ, dtype=jnp.int32)[None, :]).astype(jnp.int32)
    norm_g = gain(ks[3], (DEPTH, 4, D_MODEL))
    ffn_w_in = dense(ks[4], (DEPTH, D_MODEL, 2 * D_FF))
    ffn_w_out = dense(ks[5], (DEPTH, D_FF, D_MODEL))
    ple_w_proj = dense(ks[6], (DEPTH, D_PLE, D_MODEL))
    ple_w_gate = dense(ks[7], (DEPTH, D_MODEL, D_MODEL))
    rel_bias = 0.5 * jax.random.normal(ks[8], (REL_BUCKETS, DIL_GROUPS * DIL_HEADS), f32)
    mla_w_a = dense(ks[9], (n_a, D_MODEL, MLA_Q_RANK + MLA_KV_RANK + MLA_ROPE))
    mla_q_norm = gain(ks[10], (n_a, MLA_Q_RANK))
    mla_kv_norm = gain(ks[11], (n_a, MLA_KV_RANK))
    mla_w_uq = dense(ks[12], (n_a, MLA_Q_RANK, MLA_HEADS * (MLA_NOPE + MLA_ROPE)))
    mla_w_ukv = dense(ks[13], (n_a, MLA_KV_RANK, MLA_HEADS * (MLA_NOPE + MLA_V)))
    mla_w_o = dense(ks[14], (n_a, MLA_HEADS * MLA_V, D_MODEL))
    dil_w_qkv = dense(ks[15], (n_b, D_MODEL, DIL_GROUPS * 3 * DIL_HEADS * DIL_HEAD_DIM))
    dil_w_o = dense(ks[16], (n_b, DIL_HEADS * DIL_HEAD_DIM, D_MODEL))
    fox_w_qkvf = dense(ks[17], (n_c, D_MODEL, 3 * FOX_HEADS * FOX_HEAD_DIM + FOX_HEADS))
    fox_b_f = jax.random.uniform(ks[18], (n_c, FOX_HEADS), f32, 1.0, 5.0)
    fox_w_o = dense(ks[19], (n_c, FOX_HEADS * FOX_HEAD_DIM, D_MODEL))
    return {'x': x, 'p': p, 'positions': positions, 'norm_g': norm_g,
            'ffn_w_in': ffn_w_in, 'ffn_w_out': ffn_w_out,
            'ple_w_proj': ple_w_proj, 'ple_w_gate': ple_w_gate, 'rel_bias': rel_bias,
            'mla_w_a': mla_w_a, 'mla_q_norm': mla_q_norm, 'mla_kv_norm': mla_kv_norm,
            'mla_w_uq': mla_w_uq, 'mla_w_ukv': mla_w_ukv, 'mla_w_o': mla_w_o,
            'dil_w_qkv': dil_w_qkv, 'dil_w_o': dil_w_o,
            'fox_w_qkvf': fox_w_qkvf, 'fox_b_f': fox_b_f, 'fox_w_o': fox_w_o}


def reference(x, p, positions, norm_g, ffn_w_in, ffn_w_out, ple_w_proj, ple_w_gate, rel_bias,
              mla_w_a, mla_q_norm, mla_kv_norm, mla_w_uq, mla_w_ukv, mla_w_o,
              dil_w_qkv, dil_w_o, fox_w_qkvf, fox_b_f, fox_w_o):
    h = x
    for i in range(DEPTH):
        mixer, j = i % N_MIXERS, i // N_MIXERS
        g = norm_g[i]
        hn = rms_norm(h, g[0])
        if mixer == 0:
            y = mla_mixer(hn, positions, mla_w_a[j], mla_q_norm[j], mla_kv_norm[j],
                          mla_w_uq[j], mla_w_ukv[j], mla_w_o[j])
        elif mixer == 1:
            y = dilated_mixer(hn, dil_w_qkv[j], dil_w_o[j], rel_bias)
        else:
            y = fox_mixer(hn, fox_w_qkvf[j], fox_b_f[j], fox_w_o[j])
        h = h + rms_norm(y, g[1])
        h = h + rms_norm(swiglu(rms_norm(h, g[2]), ffn_w_in[i], ffn_w_out[i]), g[3])
        h = h + (p[i] @ ple_w_proj[i]) * jax.nn.sigmoid(h @ ple_w_gate[i])
    return h
```

```python
import functools

import numpy as np
import jax
import jax.numpy as jnp
from jax import lax
from jax.experimental import pallas as pl
from jax.experimental.pallas import tpu as pltpu

F32 = jnp.float32
BF16 = jnp.bfloat16

D_MODEL = 1024
N_LAYERS = 4
N_MIXERS = 3
D_PLE = 256
EPS = 1e-6
NEG = -1e30
D_FF = 2816

HEADS = 16
HEAD_DIM = 64
LANES = 128
PAIRS = HEADS // 2

MLA_Q_RANK = 384
MLA_KV_RANK = 256
MLA_NOPE = 64
MLA_ROPE = 32
ROPE_HALF = MLA_ROPE // 2
ROPE_THETA = 10000.0

DIL_PATTERNS = ((128, 1), (512, 4), (2048, 16))
DIL_BLOCK = 128
REL_BUCKETS = 32
REL_MAX_DIST = 2048

VMEM_LIMIT = 56 * 1024 * 1024


def _params(sem):
    return pltpu.CompilerParams(dimension_semantics=sem, vmem_limit_bytes=VMEM_LIMIT)


def _rms(x, g):
    y = x * lax.rsqrt(jnp.mean(x * x, axis=-1, keepdims=True) + EPS)
    return y * g


def _dot(a, b):
    return jnp.dot(a, b, preferred_element_type=F32)


def _dot_nt(a, b):
    return lax.dot_general(a, b, (((1,), (1,)), ((), ())), preferred_element_type=F32)


def _const_spec(shape):
    nd = len(shape)
    return pl.BlockSpec(shape, lambda *_: (0,) * nd)


def _norm_proj_body(h_ref, g_ref, w_ref, scale_ref, out_ref, xn_ref):
    @pl.when(pl.program_id(1) == 0)
    def _():
        xn_ref[...] = _rms(h_ref[...], g_ref[...]).astype(BF16)

    out_ref[...] = (_dot(xn_ref[...], w_ref[...]) * scale_ref[...]).astype(out_ref.dtype)


def _norm_proj(h, g, w, scale, tm, tn):
    n, d = h.shape
    nout = w.shape[1]
    return pl.pallas_call(
        _norm_proj_body,
        grid=(n // tm, nout // tn),
        in_specs=[
            pl.BlockSpec((tm, d), lambda i, j: (i, 0)),
            pl.BlockSpec((1, d), lambda i, j: (0, 0)),
            pl.BlockSpec((d, tn), lambda i, j: (0, j)),
            pl.BlockSpec((1, tn), lambda i, j: (0, j)),
        ],
        out_specs=pl.BlockSpec((tm, tn), lambda i, j: (i, j)),
        out_shape=jax.ShapeDtypeStruct((n, nout), BF16),
        scratch_shapes=[pltpu.VMEM((tm, d), BF16)],
        compiler_params=_params(("parallel", "arbitrary")),
        name="norm_proj",
    )(h, g, w, scale)


def _mla_proj_body(h_ref, pos_ref, g_ref, waq_ref, wakv_ref, war_ref, qn_ref, kvn_ref,
                   wuq_ref, wuk_ref, wuv_ref, inv_ref, q_ref, k_ref, v_ref):
    xn = _rms(h_ref[...], g_ref[...]).astype(BF16)
    cq = _rms(_dot(xn, waq_ref[...]), qn_ref[...]).astype(BF16)
    ckv = _rms(_dot(xn, wakv_ref[...]), kvn_ref[...]).astype(BF16)
    kr = _dot(xn, war_ref[...])
    q = _dot(cq, wuq_ref[...])
    kn = _dot(ckv, wuk_ref[...])
    v_ref[...] = _dot(ckv, wuv_ref[...]).astype(BF16)

    lane = lax.broadcasted_iota(jnp.int32, (1, LANES), 1)
    first = (lane >= MLA_NOPE) & (lane < MLA_NOPE + ROPE_HALF)
    second = (lane >= MLA_NOPE + ROPE_HALF) & (lane < MLA_NOPE + MLA_ROPE)
    ang = pos_ref[...].astype(F32) * inv_ref[...]
    cos, sin = jnp.cos(ang), jnp.sin(ang)
    c_tab = jnp.where(lane < MLA_NOPE, 1.0, jnp.where(first | second, cos, 0.0))
    s_tab = jnp.where(first, -sin, jnp.where(second, sin, 0.0))

    def rope(t):
        other = jnp.where(first, pltpu.roll(t, LANES - ROPE_HALF, 1), pltpu.roll(t, ROPE_HALF, 1))
        return t * c_tab + other * s_tab

    kr = rope(kr)
    scale = (MLA_NOPE + MLA_ROPE) ** -0.5
    for hh in range(HEADS):
        sl = slice(hh * LANES, (hh + 1) * LANES)
        q_ref[:, sl] = (rope(q[:, sl]) * scale).astype(BF16)
        k_ref[:, sl] = (kn[:, sl] + kr).astype(BF16)


def _mla_proj(h, pos, g, w, tm):
    n, d = h.shape
    row = lambda i: (i, 0)
    return pl.pallas_call(
        _mla_proj_body,
        grid=(n // tm,),
        in_specs=[
            pl.BlockSpec((tm, d), row),
            pl.BlockSpec((tm, 1), row),
            _const_spec((1, d)),
            _const_spec(w["waq"].shape), _const_spec(w["wakv"].shape), _const_spec(w["war"].shape),
            _const_spec((1, MLA_Q_RANK)), _const_spec((1, MLA_KV_RANK)),
            _const_spec(w["wuq"].shape), _const_spec(w["wuk"].shape), _const_spec(w["wuv"].shape),
            _const_spec((1, LANES)),
        ],
        out_specs=[
            pl.BlockSpec((tm, HEADS * LANES), row),
            pl.BlockSpec((tm, HEADS * LANES), row),
            pl.BlockSpec((tm, HEADS * HEAD_DIM), row),
        ],
        out_shape=[
            jax.ShapeDtypeStruct((n, HEADS * LANES), BF16),
            jax.ShapeDtypeStruct((n, HEADS * LANES), BF16),
            jax.ShapeDtypeStruct((n, HEADS * HEAD_DIM), BF16),
        ],
        compiler_params=_params(("parallel",)),
        name="mla_proj",
    )(h, pos, g, w["waq"], w["wakv"], w["war"], w["qn"], w["kvn"],
      w["wuq"], w["wuk"], w["wuv"], w["inv"])


def _fox_proj_body(h_ref, g_ref, w_ref, scale_ref, wf_ref, bf_ref, a_ref, logf_ref):
    xn = _rms(h_ref[...], g_ref[...]).astype(BF16)
    a_ref[...] = (_dot(xn, w_ref[...]) * scale_ref[...]).astype(BF16)
    f = _dot(xn, wf_ref[...]) + bf_ref[...]
    logf_ref[...] = jnp.minimum(f, 0.0) - jnp.log1p(jnp.exp(-jnp.abs(f)))


def _fox_proj(h, g, w, tm):
    n, d = h.shape
    nout = w["wqkv"].shape[1]
    row = lambda i: (i, 0)
    return pl.pallas_call(
        _fox_proj_body,
        grid=(n // tm,),
        in_specs=[
            pl.BlockSpec((tm, d), row),
            _const_spec((1, d)),
            _const_spec((d, nout)),
            _const_spec((1, nout)),
            _const_spec((d, LANES)),
            _const_spec((1, LANES)),
        ],
        out_specs=[pl.BlockSpec((tm, nout), row), pl.BlockSpec((tm, LANES), row)],
        out_shape=[jax.ShapeDtypeStruct((n, nout), BF16), jax.ShapeDtypeStruct((n, LANES), F32)],
        compiler_params=_params(("parallel",)),
        name="fox_proj",
    )(h, g, w["wqkv"], w["scale"], w["wf"], w["bf"])


def _cumsum_body(x_ref, c_ref, ct_ref, *, seq, blk):
    r = lax.broadcasted_iota(jnp.int32, (blk, blk), 0)
    c = lax.broadcasted_iota(jnp.int32, (blk, blk), 1)
    tri = (c <= r).astype(F32)
    carry = jnp.zeros((1, LANES), F32)
    for b in range(seq // blk):
        xs = x_ref[0, b * blk:(b + 1) * blk, :]
        cs = lax.dot_general(tri, xs, (((1,), (0,)), ((), ())), precision=lax.Precision.HIGHEST,
                             preferred_element_type=F32) + carry
        c_ref[0, b * blk:(b + 1) * blk, :] = cs
        carry = cs[blk - 1:blk, :]
    ct_ref[0] = c_ref[0].T


def _cumsum(logf, batch, seq):
    x = logf.reshape(batch, seq, LANES)
    return pl.pallas_call(
        functools.partial(_cumsum_body, seq=seq, blk=256),
        grid=(batch,),
        in_specs=[pl.BlockSpec((1, seq, LANES), lambda b: (b, 0, 0))],
        out_specs=[pl.BlockSpec((1, seq, LANES), lambda b: (b, 0, 0)),
                   pl.BlockSpec((1, LANES, seq), lambda b: (b, 0, 0))],
        out_shape=[jax.ShapeDtypeStruct((batch, seq, LANES), F32),
                   jax.ShapeDtypeStruct((batch, LANES, seq), F32)],
        compiler_params=_params(("parallel",)),
        name="fox_cumsum",
    )(x)


def _flash_body(*refs, seq, tq, forget, shared_lanes):
    if forget:
        q_ref, k_ref, v_ref, c_ref, ct_ref, o_ref = refs
    else:
        q_ref, k_ref, v_ref, o_ref = refs
    pair = pl.program_id(1)
    tk = tq
    lane = lax.broadcasted_iota(jnp.int32, (1, LANES), 1)
    low = lane < HEAD_DIM
    tri = (lax.broadcasted_iota(jnp.int32, (tq, tk), 1)
           <= lax.broadcasted_iota(jnp.int32, (tq, tk), 0))

    def q_block(qi, _):
        qs = pl.multiple_of(qi * tq, tq)
        if shared_lanes:
            qp = q_ref[0, pl.ds(qs, tq), :]
            qh = (jnp.where(low, qp, 0), jnp.where(low, 0, qp))
        else:
            qh = (q_ref[0, pl.ds(qs, tq), 0:LANES], q_ref[0, pl.ds(qs, tq), LANES:2 * LANES])
        if forget:
            ctile = c_ref[0, pl.ds(qs, tq), :]
            cq = tuple(jnp.sum(jnp.where(lane == 2 * pair + e, ctile, 0.0), axis=1, keepdims=True)
                       for e in range(2))

        def step(j, carry, masked):
            ks = pl.multiple_of(j * tk, tk)
            vb = v_ref[0, pl.ds(ks, tk), :]
            out = []
            for e in range(2):
                m, l, acc = carry[e]
                if shared_lanes:
                    kb = k_ref[0, pl.ds(ks, tk), :]
                else:
                    kb = k_ref[0, pl.ds(ks, tk), e * LANES:(e + 1) * LANES]
                s = _dot_nt(qh[e], kb)
                if forget:
                    ck = ct_ref[0, pl.ds(2 * pair + e, 1), pl.ds(ks, tk)]
                    s = s + (cq[e] - ck)
                if masked:
                    s = jnp.where(tri, s, NEG)
                m_new = jnp.maximum(m, jnp.max(s, axis=1, keepdims=True))
                p = jnp.exp(s - m_new)
                alpha = jnp.exp(m - m_new)
                l = alpha * l + jnp.sum(p, axis=1, keepdims=True)
                acc = alpha * acc + _dot(p.astype(BF16), vb)
                out.append((m_new, l, acc))
            return tuple(out)

        init = tuple((jnp.full((tq, 1), NEG, F32), jnp.zeros((tq, 1), F32),
                      jnp.zeros((tq, LANES), F32)) for _ in range(2))
        carry = lax.fori_loop(0, qi, lambda j, c: step(j, c, False), init)
        carry = step(qi, carry, True)
        (_, l0, a0), (_, l1, a1) = carry
        o_ref[0, pl.ds(qs, tq), :] = jnp.where(low, a0 / l0, a1 / l1).astype(o_ref.dtype)
        return 0

    lax.fori_loop(0, seq // tq, q_block, 0)


def _flash(q, k, v, batch, seq, *, shared_lanes, q_off, k_off, v_off, forget=None, tq=256):
    qk_w = LANES if shared_lanes else 2 * LANES
    in_specs = [
        pl.BlockSpec((1, seq, qk_w), lambda b, p: (b, 0, q_off + p)),
        pl.BlockSpec((1, seq, qk_w), lambda b, p: (b, 0, k_off + p)),
        pl.BlockSpec((1, seq, LANES), lambda b, p: (b, 0, v_off + p)),
    ]
    args = [q, k, v]
    if forget is not None:
        c, ct = forget
        in_specs += [pl.BlockSpec((1, seq, LANES), lambda b, p: (b, 0, 0)),
                     pl.BlockSpec((1, HEADS, seq), lambda b, p: (b, 0, 0))]
        args += [c, ct]
    return pl.pallas_call(
        functools.partial(_flash_body, seq=seq, tq=tq, forget=forget is not None,
                          shared_lanes=shared_lanes),
        grid=(batch, PAIRS),
        in_specs=in_specs,
        out_specs=pl.BlockSpec((1, seq, LANES), lambda b, p: (b, 0, p)),
        out_shape=jax.ShapeDtypeStruct((batch, seq, HEADS * HEAD_DIM), BF16),
        compiler_params=_params(("parallel", "parallel")),
        name="flash_fox" if forget is not None else "flash_mla",
    )(*args)


def _t5_bucket_np(dist):
    max_exact = REL_BUCKETS // 2
    n = np.maximum(dist.astype(np.float32), np.float32(1.0))
    large = max_exact + (np.log(n / np.float32(max_exact)) / np.float32(np.log(REL_MAX_DIST / max_exact))
                         * np.float32(REL_BUCKETS - max_exact)).astype(np.int32)
    large = np.minimum(large, REL_BUCKETS - 1)
    return np.where(dist < max_exact, dist, large).astype(np.int32)


def _dil_bucket_map(dilation):
    i = np.arange(DIL_BLOCK)
    j = np.arange(2 * DIL_BLOCK)
    rel = DIL_BLOCK + i[:, None] - j[None, :]
    return _t5_bucket_np(np.clip(rel, 0, None) * dilation)


def _dil_body(tab_ref, bucket_ref, q_ref, kp_ref, kc_ref, vp_ref, vc_ref, o_ref, lse_ref, bias_ref,
              *, span):
    blk = DIL_BLOCK
    n = pl.program_id(2)
    first = (pl.program_id(0) == 0) & (pl.program_id(1) == 0) & (n == 0)

    @pl.when(first)
    def _():
        bk = bucket_ref[...]

        def fill(hh, _):
            acc = jnp.zeros((blk, 2 * blk), F32)
            for bb in range(REL_BUCKETS):
                acc = jnp.where(bk == bb, tab_ref[bb, hh], acc)
            bias_ref[hh] = acc
            return 0

        lax.fori_loop(0, HEADS, fill, 0)

    i = lax.broadcasted_iota(jnp.int32, (blk, 2 * blk), 0)
    j = lax.broadcasted_iota(jnp.int32, (blk, 2 * blk), 1)
    rel = blk + i - j
    valid = (rel >= 0) & (rel <= span) & ((n > 0) | (j >= blk))
    lane = lax.broadcasted_iota(jnp.int32, (1, LANES), 1)
    low = lane < HEAD_DIM
    lse_all = jnp.zeros((blk, LANES), F32)
    for p in range(PAIRS):
        sl = slice(p * LANES, (p + 1) * LANES)
        qp = q_ref[0, :, sl]
        qq = jnp.concatenate([jnp.where(low, qp, 0), jnp.where(low, 0, qp)], axis=0)
        kk = jnp.concatenate([kp_ref[0, :, sl], kc_ref[0, :, sl]], axis=0)
        vv = jnp.concatenate([vp_ref[0, :, sl], vc_ref[0, :, sl]], axis=0)
        s = _dot_nt(qq, kk)
        es, ls = [], []
        for e in range(2):
            se = jnp.where(valid, s[e * blk:(e + 1) * blk] + bias_ref[2 * p + e], NEG)
            m = jnp.max(se, axis=1, keepdims=True)
            ex = jnp.exp(se - m)
            l = jnp.sum(ex, axis=1, keepdims=True)
            es.append(ex)
            ls.append(l)
            lse_all = jnp.where(lane == 2 * p + e, m + jnp.log(l), lse_all)
        o = _dot(jnp.concatenate(es, axis=0).astype(BF16), vv)
        o_ref[0, :, sl] = jnp.where(low, o[:blk] / ls[0], o[blk:] / ls[1]).astype(o_ref.dtype)
    lse_ref[0] = lse_all


def _dil_attention(qkv, table, group, batch, seq):
    window, dilation = DIL_PATTERNS[group]
    span = window // dilation
    sub = seq // dilation
    n_blk = -(-sub // DIL_BLOCK)
    assert sub % DIL_BLOCK == 0
    width = HEADS * HEAD_DIM
    per_tok = qkv.shape[-1] // width
    x = qkv.reshape(batch, sub, dilation * qkv.shape[-1])
    bucket = jnp.asarray(_dil_bucket_map(dilation))

    def col(t):
        return lambda b, r, n: (b, n, r * per_tok + group * 3 + t)

    def col_prev(t):
        return lambda b, r, n: (b, jnp.maximum(n - 1, 0), r * per_tok + group * 3 + t)

    blk_spec = lambda im: pl.BlockSpec((1, DIL_BLOCK, width), im)
    o, lse = pl.pallas_call(
        functools.partial(_dil_body, span=span),
        grid=(batch, dilation, n_blk),
        in_specs=[
            pl.BlockSpec(memory_space=pltpu.SMEM),
            pl.BlockSpec((DIL_BLOCK, 2 * DIL_BLOCK), lambda b, r, n: (0, 0)),
            blk_spec(col(0)), blk_spec(col_prev(1)), blk_spec(col(1)),
            blk_spec(col_prev(2)), blk_spec(col(2)),
        ],
        out_specs=[pl.BlockSpec((1, DIL_BLOCK, width), lambda b, r, n: (b, n, r)),
                   pl.BlockSpec((1, DIL_BLOCK, LANES), lambda b, r, n: (b, n, r))],
        out_shape=[jax.ShapeDtypeStruct((batch, sub, dilation * width), BF16),
                   jax.ShapeDtypeStruct((batch, sub, dilation * LANES), F32)],
        scratch_shapes=[pltpu.VMEM((HEADS, DIL_BLOCK, 2 * DIL_BLOCK), F32)],
        compiler_params=_params(("arbitrary", "arbitrary", "arbitrary")),
        name=f"dil_attn_g{group}",
    )(table, bucket, x, x, x, x, x)
    return o.reshape(batch * seq, width), lse.reshape(batch * seq, LANES)


def _oproj_body(o_ref, w_ref, h_ref, g_ref, out_ref):
    out_ref[...] = h_ref[...] + _rms(_dot(o_ref[...], w_ref[...]), g_ref[...])


def _oproj(o, w, h, g, tm):
    n, d = h.shape
    row = lambda i: (i, 0)
    return pl.pallas_call(
        _oproj_body,
        grid=(n // tm,),
        in_specs=[pl.BlockSpec((tm, o.shape[1]), row), _const_spec(w.shape),
                  pl.BlockSpec((tm, d), row), _const_spec((1, d))],
        out_specs=pl.BlockSpec((tm, d), row),
        out_shape=jax.ShapeDtypeStruct((n, d), F32),
        compiler_params=_params(("parallel",)),
        name="oproj",
    )(o, w, h, g)


def _oproj_merge_body(o0_ref, o1_ref, o2_ref, l0_ref, l1_ref, l2_ref, w_ref, h_ref, g_ref, out_ref,
                      merged_ref):
    lses = (l0_ref[...], l1_ref[...], l2_ref[...])
    mx = jnp.maximum(jnp.maximum(lses[0], lses[1]), lses[2])
    ex = [jnp.exp(t - mx) for t in lses]
    den = ex[0] + ex[1] + ex[2]
    alpha = [t / den for t in ex]
    o_refs = (o0_ref, o1_ref, o2_ref)
    tm = h_ref.shape[0]
    low = lax.broadcasted_iota(jnp.int32, (1, LANES), 1) < HEAD_DIM
    for p in range(PAIRS):
        sl = slice(p * LANES, (p + 1) * LANES)
        acc = jnp.zeros((tm, LANES), F32)
        for gidx in range(3):
            a = jnp.where(low, alpha[gidx][:, 2 * p:2 * p + 1], alpha[gidx][:, 2 * p + 1:2 * p + 2])
            acc = acc + a * o_refs[gidx][:, sl].astype(F32)
        merged_ref[:, sl] = acc.astype(BF16)
    out_ref[...] = h_ref[...] + _rms(_dot(merged_ref[...], w_ref[...]), g_ref[...])


def _oproj_merge(os_, lses, w, h, g, tm):
    n, d = h.shape
    row = lambda i: (i, 0)
    width = os_[0].shape[1]
    return pl.pallas_call(
        _oproj_merge_body,
        grid=(n // tm,),
        in_specs=[pl.BlockSpec((tm, width), row)] * 3 + [pl.BlockSpec((tm, LANES), row)] * 3
        + [_const_spec(w.shape), pl.BlockSpec((tm, d), row), _const_spec((1, d))],
        out_specs=pl.BlockSpec((tm, d), row),
        out_shape=jax.ShapeDtypeStruct((n, d), F32),
        scratch_shapes=[pltpu.VMEM((tm, width), BF16)],
        compiler_params=_params(("parallel",)),
        name="oproj_merge",
    )(*os_, *lses, w, h, g)


def _ffn_body(h_ref, p_ref, g2_ref, g3_ref, wg_ref, wu_ref, wo_ref, wproj_ref, wgate_ref, out_ref,
              xn_ref, acc_ref, *, nf):
    j = pl.program_id(1)

    @pl.when(j == 0)
    def _():
        xn_ref[...] = _rms(h_ref[...], g2_ref[...]).astype(BF16)

    xn = xn_ref[...]
    gate = _dot(xn, wg_ref[...])
    up = _dot(xn, wu_ref[...])
    act = (gate * jax.nn.sigmoid(gate) * up).astype(BF16)
    part = _dot(act, wo_ref[...])

    @pl.when(j == 0)
    def _():
        acc_ref[...] = part

    @pl.when(j > 0)
    def _():
        acc_ref[...] += part

    @pl.when(j == nf - 1)
    def _():
        h2 = h_ref[...] + _rms(acc_ref[...], g3_ref[...])
        emb = _dot(p_ref[...].astype(BF16), wproj_ref[...])
        out_ref[...] = h2 + emb * jax.nn.sigmoid(_dot(h2.astype(BF16), wgate_ref[...]))


def _ffn(h, p, g2, g3, w_in, w_out, w_proj, w_gate, tm, tf):
    n, d = h.shape
    nf = D_FF // tf
    return pl.pallas_call(
        functools.partial(_ffn_body, nf=nf),
        grid=(n // tm, nf),
        in_specs=[
            pl.BlockSpec((tm, d), lambda i, j: (i, 0)),
            pl.BlockSpec((tm, D_PLE), lambda i, j: (i, 0)),
            pl.BlockSpec((1, d), lambda i, j: (0, 0)),
            pl.BlockSpec((1, d), lambda i, j: (0, 0)),
            pl.BlockSpec((d, tf), lambda i, j: (0, j)),
            pl.BlockSpec((d, tf), lambda i, j: (0, nf + j)),
            pl.BlockSpec((tf, d), lambda i, j: (j, 0)),
            pl.BlockSpec((D_PLE, d), lambda i, j: (0, 0)),
            pl.BlockSpec((d, d), lambda i, j: (0, 0)),
        ],
        out_specs=pl.BlockSpec((tm, d), lambda i, j: (i, 0)),
        out_shape=jax.ShapeDtypeStruct((n, d), F32),
        scratch_shapes=[pltpu.VMEM((tm, d), BF16), pltpu.VMEM((tm, d), F32)],
        compiler_params=_params(("parallel", "arbitrary")),
        name="ffn",
    )(h, p, g2, g3, w_in, w_in, w_out, w_proj, w_gate)


def _mla_weights(w_a, q_norm, kv_norm, w_uq, w_ukv):
    d = w_a.shape[0]
    waq = w_a[:, :MLA_Q_RANK].astype(BF16)
    wakv = w_a[:, MLA_Q_RANK:MLA_Q_RANK + MLA_KV_RANK].astype(BF16)
    war = jnp.zeros((d, LANES), F32).at[:, MLA_NOPE:MLA_NOPE + MLA_ROPE].set(
        w_a[:, MLA_Q_RANK + MLA_KV_RANK:]).astype(BF16)
    uq = w_uq.reshape(MLA_Q_RANK, HEADS, MLA_NOPE + MLA_ROPE)
    wuq = jnp.pad(uq, ((0, 0), (0, 0), (0, LANES - MLA_NOPE - MLA_ROPE)))
    wuq = wuq.reshape(MLA_Q_RANK, HEADS * LANES).astype(BF16)
    ukv = w_ukv.reshape(MLA_KV_RANK, HEADS, MLA_NOPE + HEAD_DIM)
    wuk = jnp.pad(ukv[:, :, :MLA_NOPE], ((0, 0), (0, 0), (0, LANES - MLA_NOPE)))
    wuk = wuk.reshape(MLA_KV_RANK, HEADS * LANES).astype(BF16)
    wuv = ukv[:, :, MLA_NOPE:].reshape(MLA_KV_RANK, HEADS * HEAD_DIM).astype(BF16)
    inv = ROPE_THETA ** (-jnp.arange(ROPE_HALF, dtype=F32) / ROPE_HALF)
    inv_row = jnp.zeros((1, LANES), F32)
    inv_row = inv_row.at[0, MLA_NOPE:MLA_NOPE + ROPE_HALF].set(inv)
    inv_row = inv_row.at[0, MLA_NOPE + ROPE_HALF:MLA_NOPE + MLA_ROPE].set(inv)
    return dict(waq=waq, wakv=wakv, war=war, qn=q_norm.reshape(1, -1), kvn=kv_norm.reshape(1, -1),
                wuq=wuq, wuk=wuk, wuv=wuv, inv=inv_row)


def _q_scale_row(n_cols, q_starts, width, scale):
    row = np.ones((1, n_cols), np.float32)
    for s in q_starts:
        row[0, s:s + width] = scale
    return jnp.asarray(row)


def kernel(x, p, positions, norm_g, ffn_w_in, ffn_w_out, ple_w_proj, ple_w_gate, rel_bias, mla_w_a, mla_q_norm, mla_kv_norm, mla_w_uq, mla_w_ukv, mla_w_o, dil_w_qkv, dil_w_o, fox_w_qkvf, fox_b_f, fox_w_o):
    batch, seq, d = x.shape
    n = batch * seq
    inner = HEADS * HEAD_DIM
    h = x.reshape(n, d)
    pos = positions.reshape(n, 1)
    for i in range(N_LAYERS):
        mixer, j = i % N_MIXERS, i // N_MIXERS
        g = norm_g[i].reshape(4, 1, d)
        if mixer == 0:
            w = _mla_weights(mla_w_a[j], mla_q_norm[j], mla_kv_norm[j], mla_w_uq[j], mla_w_ukv[j])
            q, k, v = _mla_proj(h, pos, g[0], w, tm=512)
            o = _flash(q.reshape(batch, seq, -1), k.reshape(batch, seq, -1), v.reshape(batch, seq, -1),
                       batch, seq, shared_lanes=False, q_off=0, k_off=0, v_off=0)
            h = _oproj(o.reshape(n, inner), mla_w_o[j].astype(BF16), h, g[1], tm=512)
        elif mixer == 1:
            n_cols = dil_w_qkv.shape[-1]
            scale = _q_scale_row(n_cols, [gi * 3 * inner for gi in range(len(DIL_PATTERNS))], inner,
                                 HEAD_DIM ** -0.5)
            qkv = _norm_proj(h, g[0], dil_w_qkv[j].astype(BF16), scale, tm=1024, tn=1536)
            table = rel_bias.reshape(REL_BUCKETS, len(DIL_PATTERNS), HEADS)
            outs, lses = [], []
            for gi in range(len(DIL_PATTERNS)):
                o, lse = _dil_attention(qkv.reshape(batch, seq, n_cols), table[:, gi], gi, batch, seq)
                outs.append(o)
                lses.append(lse)
            h = _oproj_merge(outs, lses, dil_w_o[j].astype(BF16), h, g[1], tm=512)
        else:
            wq = fox_w_qkvf[j]
            w = dict(
                wqkv=wq[:, :3 * inner].astype(BF16),
                scale=_q_scale_row(3 * inner, [0], inner, HEAD_DIM ** -0.5),
                wf=jnp.pad(wq[:, 3 * inner:], ((0, 0), (0, LANES - HEADS))).astype(BF16),
                bf=jnp.pad(fox_b_f[j], (0, LANES - HEADS)).reshape(1, LANES),
            )
            a, logf = _fox_proj(h, g[0], w, tm=512)
            c, ct = _cumsum(logf, batch, seq)
            a3 = a.reshape(batch, seq, 3 * inner)
            o = _flash(a3, a3, a3, batch, seq, shared_lanes=True, q_off=0, k_off=PAIRS, v_off=2 * PAIRS,
                       forget=(c, ct))
            h = _oproj(o.reshape(n, inner), fox_w_o[j].astype(BF16), h, g[1], tm=512)
        h = _ffn(h, p[i].reshape(n, D_PLE), g[2], g[3], ffn_w_in[i].astype(BF16),
                 ffn_w_out[i].astype(BF16), ple_w_proj[i].astype(BF16), ple_w_gate[i].astype(BF16),
                 tm=512, tf=1408)
    return h.reshape(batch, seq, d)
```

```python
import functools

import numpy as np
import jax
import jax.numpy as jnp
from jax import lax
from jax.experimental import pallas as pl
from jax.experimental.pallas import tpu as pltpu

F32 = jnp.float32
BF16 = jnp.bfloat16

D_MODEL = 1024
N_LAYERS = 4
N_MIXERS = 3
D_PLE = 256
EPS = 1e-6
NEG = -1e30
D_FF = 2816

HEADS = 16
HEAD_DIM = 64
LANES = 128
PAIRS = HEADS // 2

MLA_Q_RANK = 384
MLA_KV_RANK = 256
MLA_NOPE = 64
MLA_ROPE = 32
ROPE_HALF = MLA_ROPE // 2
ROPE_THETA = 10000.0

DIL_PATTERNS = ((128, 1), (512, 4), (2048, 16))
DIL_BLOCK = 128
REL_BUCKETS = 32
REL_MAX_DIST = 2048

VMEM_LIMIT = 56 * 1024 * 1024


def _params(sem):
    return pltpu.CompilerParams(dimension_semantics=sem, vmem_limit_bytes=VMEM_LIMIT)


def _rms(x, g):
    y = x * lax.rsqrt(jnp.mean(x * x, axis=-1, keepdims=True) + EPS)
    return y * g


def _dot(a, b):
    return jnp.dot(a, b, preferred_element_type=F32)


def _dot_nt(a, b):
    return lax.dot_general(a, b, (((1,), (1,)), ((), ())), preferred_element_type=F32)


def _const_spec(shape):
    nd = len(shape)
    return pl.BlockSpec(shape, lambda *_: (0,) * nd)


def _norm_proj_body(h_ref, g_ref, w_ref, scale_ref, out_ref, xn_ref):
    @pl.when(pl.program_id(1) == 0)
    def _():
        xn_ref[...] = _rms(h_ref[...], g_ref[...]).astype(BF16)

    out_ref[...] = (_dot(xn_ref[...], w_ref[...]) * scale_ref[...]).astype(out_ref.dtype)


def _norm_proj(h, g, w, scale, tm, tn):
    n, d = h.shape
    nout = w.shape[1]
    return pl.pallas_call(
        _norm_proj_body,
        grid=(n // tm, nout // tn),
        in_specs=[
            pl.BlockSpec((tm, d), lambda i, j: (i, 0)),
            pl.BlockSpec((1, d), lambda i, j: (0, 0)),
            pl.BlockSpec((d, tn), lambda i, j: (0, j)),
            pl.BlockSpec((1, tn), lambda i, j: (0, j)),
        ],
        out_specs=pl.BlockSpec((tm, tn), lambda i, j: (i, j)),
        out_shape=jax.ShapeDtypeStruct((n, nout), BF16),
        scratch_shapes=[pltpu.VMEM((tm, d), BF16)],
        compiler_params=_params(("parallel", "arbitrary")),
        name="norm_proj",
    )(h, g, w, scale)


def _mla_proj_body(h_ref, pos_ref, g_ref, waq_ref, wakv_ref, war_ref, qn_ref, kvn_ref,
                   wuq_ref, wuk_ref, wuvt_ref, inv_ref, q_ref, k_ref, vt_ref):
    xn = _rms(h_ref[...], g_ref[...]).astype(BF16)
    cq = _rms(_dot(xn, waq_ref[...]), qn_ref[...]).astype(BF16)
    ckv = _rms(_dot(xn, wakv_ref[...]), kvn_ref[...]).astype(BF16)
    kr = _dot(xn, war_ref[...])
    q = _dot(cq, wuq_ref[...])
    kn = _dot(ckv, wuk_ref[...])
    vt_ref[0] = _dot_nt(wuvt_ref[...], ckv).astype(BF16)

    lane = lax.broadcasted_iota(jnp.int32, (1, LANES), 1)
    first = (lane >= MLA_NOPE) & (lane < MLA_NOPE + ROPE_HALF)
    second = (lane >= MLA_NOPE + ROPE_HALF) & (lane < MLA_NOPE + MLA_ROPE)
    ang = pos_ref[...].astype(F32) * inv_ref[...]
    cos, sin = jnp.cos(ang), jnp.sin(ang)
    c_tab = jnp.where(lane < MLA_NOPE, 1.0, jnp.where(first | second, cos, 0.0))
    s_tab = jnp.where(first, -sin, jnp.where(second, sin, 0.0))

    def rope(t):
        other = jnp.where(first, pltpu.roll(t, LANES - ROPE_HALF, 1), pltpu.roll(t, ROPE_HALF, 1))
        return t * c_tab + other * s_tab

    kr = rope(kr)
    scale = (MLA_NOPE + MLA_ROPE) ** -0.5
    for hh in range(HEADS):
        sl = slice(hh * LANES, (hh + 1) * LANES)
        q_ref[:, sl] = (rope(q[:, sl]) * scale).astype(BF16)
        k_ref[:, sl] = (kn[:, sl] + kr).astype(BF16)


def _vt_spec(tm, seq, width):
    per = seq // tm
    return pl.BlockSpec((1, width, tm), lambda i: (i // per, 0, i % per))


def _mla_proj(h, pos, g, w, tm, batch, seq):
    n, d = h.shape
    row = lambda i: (i, 0)
    width = HEADS * HEAD_DIM
    return pl.pallas_call(
        _mla_proj_body,
        grid=(n // tm,),
        in_specs=[
            pl.BlockSpec((tm, d), row),
            pl.BlockSpec((tm, 1), row),
            _const_spec((1, d)),
            _const_spec(w["waq"].shape), _const_spec(w["wakv"].shape), _const_spec(w["war"].shape),
            _const_spec((1, MLA_Q_RANK)), _const_spec((1, MLA_KV_RANK)),
            _const_spec(w["wuq"].shape), _const_spec(w["wuk"].shape), _const_spec(w["wuvt"].shape),
            _const_spec((1, LANES)),
        ],
        out_specs=[
            pl.BlockSpec((tm, HEADS * LANES), row),
            pl.BlockSpec((tm, HEADS * LANES), row),
            _vt_spec(tm, seq, width),
        ],
        out_shape=[
            jax.ShapeDtypeStruct((n, HEADS * LANES), BF16),
            jax.ShapeDtypeStruct((n, HEADS * LANES), BF16),
            jax.ShapeDtypeStruct((batch, width, seq), BF16),
        ],
        compiler_params=_params(("parallel",)),
        name="mla_proj",
    )(h, pos, g, w["waq"], w["wakv"], w["war"], w["qn"], w["kvn"],
      w["wuq"], w["wuk"], w["wuvt"], w["inv"])


def _fox_proj_body(h_ref, g_ref, w_ref, scale_ref, wvt_ref, wf_ref, bf_ref, a_ref, vt_ref, logf_ref):
    xn = _rms(h_ref[...], g_ref[...]).astype(BF16)
    a_ref[...] = (_dot(xn, w_ref[...]) * scale_ref[...]).astype(BF16)
    vt_ref[0] = _dot_nt(wvt_ref[...], xn).astype(BF16)
    f = _dot(xn, wf_ref[...]) + bf_ref[...]
    logf_ref[...] = jnp.minimum(f, 0.0) - jnp.log1p(jnp.exp(-jnp.abs(f)))


def _fox_proj(h, g, w, tm, batch, seq):
    n, d = h.shape
    nout = w["wqk"].shape[1]
    width = HEADS * HEAD_DIM
    row = lambda i: (i, 0)
    return pl.pallas_call(
        _fox_proj_body,
        grid=(n // tm,),
        in_specs=[
            pl.BlockSpec((tm, d), row),
            _const_spec((1, d)),
            _const_spec((d, nout)),
            _const_spec((1, nout)),
            _const_spec((width, d)),
            _const_spec((d, LANES)),
            _const_spec((1, LANES)),
        ],
        out_specs=[pl.BlockSpec((tm, nout), row), _vt_spec(tm, seq, width),
                   pl.BlockSpec((tm, LANES), row)],
        out_shape=[jax.ShapeDtypeStruct((n, nout), BF16),
                   jax.ShapeDtypeStruct((batch, width, seq), BF16),
                   jax.ShapeDtypeStruct((n, LANES), F32)],
        compiler_params=_params(("parallel",)),
        name="fox_proj",
    )(h, g, w["wqk"], w["scale"], w["wvt"], w["wf"], w["bf"])


def _cumsum_body(x_ref, c_ref, ct_ref, *, seq, blk):
    r = lax.broadcasted_iota(jnp.int32, (blk, blk), 0)
    c = lax.broadcasted_iota(jnp.int32, (blk, blk), 1)
    tri = (c <= r).astype(F32)
    carry = jnp.zeros((1, LANES), F32)
    for b in range(seq // blk):
        xs = x_ref[0, b * blk:(b + 1) * blk, :]
        cs = lax.dot_general(tri, xs, (((1,), (0,)), ((), ())), precision=lax.Precision.HIGHEST,
                             preferred_element_type=F32) + carry
        c_ref[0, b * blk:(b + 1) * blk, :] = cs
        carry = cs[blk - 1:blk, :]
    ct_ref[0] = c_ref[0].T


def _cumsum(logf, batch, seq):
    x = logf.reshape(batch, seq, LANES)
    return pl.pallas_call(
        functools.partial(_cumsum_body, seq=seq, blk=256),
        grid=(batch,),
        in_specs=[pl.BlockSpec((1, seq, LANES), lambda b: (b, 0, 0))],
        out_specs=[pl.BlockSpec((1, seq, LANES), lambda b: (b, 0, 0)),
                   pl.BlockSpec((1, LANES, seq), lambda b: (b, 0, 0))],
        out_shape=[jax.ShapeDtypeStruct((batch, seq, LANES), F32),
                   jax.ShapeDtypeStruct((batch, LANES, seq), F32)],
        compiler_params=_params(("parallel",)),
        name="fox_cumsum",
    )(x)


def _flash_body(*refs, seq, tq, forget, shared_lanes):
    if forget:
        q_ref, k_ref, vt_ref, c_ref, ct_ref, o_ref, ck_ref = refs
    else:
        q_ref, k_ref, vt_ref, o_ref = refs
    pair = pl.program_id(1)
    lane = lax.broadcasted_iota(jnp.int32, (1, LANES), 1)
    low = lane < HEAD_DIM
    causal = (lax.broadcasted_iota(jnp.int32, (tq, tq), 0)
              <= lax.broadcasted_iota(jnp.int32, (tq, tq), 1))
    if forget:
        for e in range(2):
            col = jnp.sum(jnp.where(lane == 2 * pair + e, c_ref[0], 0.0), axis=1, keepdims=True)
            ck_ref[e] = jnp.broadcast_to(col, (seq, tq))

    for qi in range(seq // tq):
        qs = qi * tq
        outs = []
        for e in range(2):
            if shared_lanes:
                qp = q_ref[0, qs:qs + tq, :]
                qe = jnp.where(low, qp, 0) if e == 0 else jnp.where(low, 0, qp)
                ksl = slice(0, LANES)
            else:
                ksl = slice(e * LANES, (e + 1) * LANES)
                qe = q_ref[0, qs:qs + tq, ksl]
            vsl = slice(e * HEAD_DIM, (e + 1) * HEAD_DIM)
            if forget:
                cq = ct_ref[0, pl.ds(2 * pair + e, 1), qs:qs + tq]

            s_d = _dot_nt(k_ref[0, qs:qs + tq, ksl], qe)
            if forget:
                s_d = s_d + (cq - ck_ref[e, qs:qs + tq, :])
            s_d = jnp.where(causal, s_d, NEG)
            m = jnp.max(s_d, axis=0, keepdims=True)
            if qi > 0:
                s_m = _dot_nt(k_ref[0, 0:qs, ksl], qe)
                if forget:
                    s_m = s_m + (cq - ck_ref[e, 0:qs, :])
                m = jnp.maximum(m, jnp.max(s_m, axis=0, keepdims=True))
                p_m = jnp.exp(s_m - m)
                l = jnp.sum(p_m, axis=0, keepdims=True)
                acc = _dot(vt_ref[0, vsl, 0:qs], p_m.astype(BF16))
            p_d = jnp.exp(s_d - m)
            l_d = jnp.sum(p_d, axis=0, keepdims=True)
            acc_d = _dot(vt_ref[0, vsl, qs:qs + tq], p_d.astype(BF16))
            if qi > 0:
                l, acc = l + l_d, acc + acc_d
            else:
                l, acc = l_d, acc_d
            outs.append(acc / l)
        o_t = jnp.concatenate(outs, axis=0)
        o_ref[0, qs:qs + tq, :] = o_t.T.astype(o_ref.dtype)


def _flash(q, k, vt, batch, seq, *, shared_lanes, q_off, k_off, forget=None, tq=256):
    qk_w = LANES if shared_lanes else 2 * LANES
    in_specs = [
        pl.BlockSpec((1, seq, qk_w), lambda b, p: (b, 0, q_off + p)),
        pl.BlockSpec((1, seq, qk_w), lambda b, p: (b, 0, k_off + p)),
        pl.BlockSpec((1, LANES, seq), lambda b, p: (b, p, 0)),
    ]
    args = [q, k, vt]
    scratch = []
    if forget is not None:
        c, ct = forget
        in_specs += [pl.BlockSpec((1, seq, LANES), lambda b, p: (b, 0, 0)),
                     pl.BlockSpec((1, HEADS, seq), lambda b, p: (b, 0, 0))]
        args += [c, ct]
        scratch = [pltpu.VMEM((2, seq, tq), F32)]
    return pl.pallas_call(
        functools.partial(_flash_body, seq=seq, tq=tq, forget=forget is not None,
                          shared_lanes=shared_lanes),
        grid=(batch, PAIRS),
        in_specs=in_specs,
        out_specs=pl.BlockSpec((1, seq, LANES), lambda b, p: (b, 0, p)),
        out_shape=jax.ShapeDtypeStruct((batch, seq, HEADS * HEAD_DIM), BF16),
        scratch_shapes=scratch,
        compiler_params=_params(("parallel", "parallel")),
        name="flash_fox" if forget is not None else "flash_mla",
    )(*args)


def _t5_bucket_np(dist):
    max_exact = REL_BUCKETS // 2
    n = np.maximum(dist.astype(np.float32), np.float32(1.0))
    large = max_exact + (np.log(n / np.float32(max_exact)) / np.float32(np.log(REL_MAX_DIST / max_exact))
                         * np.float32(REL_BUCKETS - max_exact)).astype(np.int32)
    large = np.minimum(large, REL_BUCKETS - 1)
    return np.where(dist < max_exact, dist, large).astype(np.int32)


def _dil_bucket_map(dilation):
    i = np.arange(DIL_BLOCK)
    j = np.arange(2 * DIL_BLOCK)
    rel = DIL_BLOCK + i[:, None] - j[None, :]
    return _t5_bucket_np(np.clip(rel, 0, None) * dilation)


def _dil_body(tab_ref, bucket_ref, q_ref, kp_ref, kc_ref, vp_ref, vc_ref, o_ref, lse_ref, bias_ref,
              *, span):
    blk = DIL_BLOCK
    n = pl.program_id(2)
    first = (pl.program_id(0) == 0) & (pl.program_id(1) == 0) & (n == 0)

    @pl.when(first)
    def _():
        bk = bucket_ref[...]

        def fill(hh, _):
            acc = jnp.zeros((blk, 2 * blk), F32)
            for bb in range(REL_BUCKETS):
                acc = jnp.where(bk == bb, tab_ref[bb, hh], acc)
            bias_ref[hh] = acc
            return 0

        lax.fori_loop(0, HEADS, fill, 0)

    i = lax.broadcasted_iota(jnp.int32, (blk, 2 * blk), 0)
    j = lax.broadcasted_iota(jnp.int32, (blk, 2 * blk), 1)
    rel = blk + i - j
    valid = (rel >= 0) & (rel <= span) & ((n > 0) | (j >= blk))
    lane = lax.broadcasted_iota(jnp.int32, (1, LANES), 1)
    low = lane < HEAD_DIM
    lse_all = jnp.zeros((blk, LANES), F32)
    for p in range(PAIRS):
        sl = slice(p * LANES, (p + 1) * LANES)
        qp = q_ref[0, :, sl]
        qq = jnp.concatenate([jnp.where(low, qp, 0), jnp.where(low, 0, qp)], axis=0)
        kk = jnp.concatenate([kp_ref[0, :, sl], kc_ref[0, :, sl]], axis=0)
        vv = jnp.concatenate([vp_ref[0, :, sl], vc_ref[0, :, sl]], axis=0)
        s = _dot_nt(qq, kk)
        es, ls = [], []
        for e in range(2):
            se = jnp.where(valid, s[e * blk:(e + 1) * blk] + bias_ref[2 * p + e], NEG)
            m = jnp.max(se, axis=1, keepdims=True)
            ex = jnp.exp(se - m)
            l = jnp.sum(ex, axis=1, keepdims=True)
            es.append(ex)
            ls.append(l)
            lse_all = jnp.where(lane == 2 * p + e, m + jnp.log(l), lse_all)
        o = _dot(jnp.concatenate(es, axis=0).astype(BF16), vv)
        o_ref[0, :, sl] = jnp.where(low, o[:blk] / ls[0], o[blk:] / ls[1]).astype(o_ref.dtype)
    lse_ref[0] = lse_all


def _dil_attention(qkv, table, group, batch, seq):
    window, dilation = DIL_PATTERNS[group]
    span = window // dilation
    sub = seq // dilation
    n_blk = -(-sub // DIL_BLOCK)
    assert sub % DIL_BLOCK == 0
    width = HEADS * HEAD_DIM
    per_tok = qkv.shape[-1] // width
    x = qkv.reshape(batch, sub, dilation * qkv.shape[-1])
    bucket = jnp.asarray(_dil_bucket_map(dilation))

    def col(t):
        return lambda b, r, n: (b, n, r * per_tok + group * 3 + t)

    def col_prev(t):
        return lambda b, r, n: (b, jnp.maximum(n - 1, 0), r * per_tok + group * 3 + t)

    blk_spec = lambda im: pl.BlockSpec((1, DIL_BLOCK, width), im)
    o, lse = pl.pallas_call(
        functools.partial(_dil_body, span=span),
        grid=(batch, dilation, n_blk),
        in_specs=[
            pl.BlockSpec(memory_space=pltpu.SMEM),
            pl.BlockSpec((DIL_BLOCK, 2 * DIL_BLOCK), lambda b, r, n: (0, 0)),
            blk_spec(col(0)), blk_spec(col_prev(1)), blk_spec(col(1)),
            blk_spec(col_prev(2)), blk_spec(col(2)),
        ],
        out_specs=[pl.BlockSpec((1, DIL_BLOCK, width), lambda b, r, n: (b, n, r)),
                   pl.BlockSpec((1, DIL_BLOCK, LANES), lambda b, r, n: (b, n, r))],
        out_shape=[jax.ShapeDtypeStruct((batch, sub, dilation * width), BF16),
                   jax.ShapeDtypeStruct((batch, sub, dilation * LANES), F32)],
        scratch_shapes=[pltpu.VMEM((HEADS, DIL_BLOCK, 2 * DIL_BLOCK), F32)],
        compiler_params=_params(("arbitrary", "arbitrary", "arbitrary")),
        name=f"dil_attn_g{group}",
    )(table, bucket, x, x, x, x, x)
    return o.reshape(batch * seq, width), lse.reshape(batch * seq, LANES)


def _oproj_body(o_ref, w_ref, h_ref, g_ref, out_ref):
    out_ref[...] = h_ref[...] + _rms(_dot(o_ref[...], w_ref[...]), g_ref[...])


def _oproj(o, w, h, g, tm):
    n, d = h.shape
    row = lambda i: (i, 0)
    return pl.pallas_call(
        _oproj_body,
        grid=(n // tm,),
        in_specs=[pl.BlockSpec((tm, o.shape[1]), row), _const_spec(w.shape),
                  pl.BlockSpec((tm, d), row), _const_spec((1, d))],
        out_specs=pl.BlockSpec((tm, d), row),
        out_shape=jax.ShapeDtypeStruct((n, d), F32),
        compiler_params=_params(("parallel",)),
        name="oproj",
    )(o, w, h, g)


def _oproj_merge_body(o0_ref, o1_ref, o2_ref, l0_ref, l1_ref, l2_ref, w_ref, h_ref, g_ref, out_ref,
                      merged_ref):
    lses = (l0_ref[...], l1_ref[...], l2_ref[...])
    mx = jnp.maximum(jnp.maximum(lses[0], lses[1]), lses[2])
    ex = [jnp.exp(t - mx) for t in lses]
    den = ex[0] + ex[1] + ex[2]
    alpha = [t / den for t in ex]
    o_refs = (o0_ref, o1_ref, o2_ref)
    tm = h_ref.shape[0]
    low = lax.broadcasted_iota(jnp.int32, (1, LANES), 1) < HEAD_DIM
    for p in range(PAIRS):
        sl = slice(p * LANES, (p + 1) * LANES)
        acc = jnp.zeros((tm, LANES), F32)
        for gidx in range(3):
            a = jnp.where(low, alpha[gidx][:, 2 * p:2 * p + 1], alpha[gidx][:, 2 * p + 1:2 * p + 2])
            acc = acc + a * o_refs[gidx][:, sl].astype(F32)
        merged_ref[:, sl] = acc.astype(BF16)
    out_ref[...] = h_ref[...] + _rms(_dot(merged_ref[...], w_ref[...]), g_ref[...])


def _oproj_merge(os_, lses, w, h, g, tm):
    n, d = h.shape
    row = lambda i: (i, 0)
    width = os_[0].shape[1]
    return pl.pallas_call(
        _oproj_merge_body,
        grid=(n // tm,),
        in_specs=[pl.BlockSpec((tm, width), row)] * 3 + [pl.BlockSpec((tm, LANES), row)] * 3
        + [_const_spec(w.shape), pl.BlockSpec((tm, d), row), _const_spec((1, d))],
        out_specs=pl.BlockSpec((tm, d), row),
        out_shape=jax.ShapeDtypeStruct((n, d), F32),
        scratch_shapes=[pltpu.VMEM((tm, width), BF16)],
        compiler_params=_params(("parallel",)),
        name="oproj_merge",
    )(*os_, *lses, w, h, g)


def _ffn_body(h_ref, p_ref, g2_ref, g3_ref, wg_ref, wu_ref, wo_ref, wproj_ref, wgate_ref, out_ref,
              xn_ref, acc_ref, *, nf):
    j = pl.program_id(1)

    @pl.when(j == 0)
    def _():
        xn_ref[...] = _rms(h_ref[...], g2_ref[...]).astype(BF16)

    xn = xn_ref[...]
    gate = _dot(xn, wg_ref[...])
    up = _dot(xn, wu_ref[...])
    act = (gate * jax.nn.sigmoid(gate) * up).astype(BF16)
    part = _dot(act, wo_ref[...])

    @pl.when(j == 0)
    def _():
        acc_ref[...] = part

    @pl.when(j > 0)
    def _():
        acc_ref[...] += part

    @pl.when(j == nf - 1)
    def _():
        h2 = h_ref[...] + _rms(acc_ref[...], g3_ref[...])
        emb = _dot(p_ref[...].astype(BF16), wproj_ref[...])
        out_ref[...] = h2 + emb * jax.nn.sigmoid(_dot(h2.astype(BF16), wgate_ref[...]))


def _ffn(h, p, g2, g3, w_in, w_out, w_proj, w_gate, tm, tf):
    n, d = h.shape
    nf = D_FF // tf
    return pl.pallas_call(
        functools.partial(_ffn_body, nf=nf),
        grid=(n // tm, nf),
        in_specs=[
            pl.BlockSpec((tm, d), lambda i, j: (i, 0)),
            pl.BlockSpec((tm, D_PLE), lambda i, j: (i, 0)),
            pl.BlockSpec((1, d), lambda i, j: (0, 0)),
            pl.BlockSpec((1, d), lambda i, j: (0, 0)),
            pl.BlockSpec((d, tf), lambda i, j: (0, j)),
            pl.BlockSpec((d, tf), lambda i, j: (0, nf + j)),
            pl.BlockSpec((tf, d), lambda i, j: (j, 0)),
            pl.BlockSpec((D_PLE, d), lambda i, j: (0, 0)),
            pl.BlockSpec((d, d), lambda i, j: (0, 0)),
        ],
        out_specs=pl.BlockSpec((tm, d), lambda i, j: (i, 0)),
        out_shape=jax.ShapeDtypeStruct((n, d), F32),
        scratch_shapes=[pltpu.VMEM((tm, d), BF16), pltpu.VMEM((tm, d), F32)],
        compiler_params=_params(("parallel", "arbitrary")),
        name="ffn",
    )(h, p, g2, g3, w_in, w_in, w_out, w_proj, w_gate)


def _mla_weights(w_a, q_norm, kv_norm, w_uq, w_ukv):
    d = w_a.shape[0]
    waq = w_a[:, :MLA_Q_RANK].astype(BF16)
    wakv = w_a[:, MLA_Q_RANK:MLA_Q_RANK + MLA_KV_RANK].astype(BF16)
    war = jnp.zeros((d, LANES), F32).at[:, MLA_NOPE:MLA_NOPE + MLA_ROPE].set(
        w_a[:, MLA_Q_RANK + MLA_KV_RANK:]).astype(BF16)
    uq = w_uq.reshape(MLA_Q_RANK, HEADS, MLA_NOPE + MLA_ROPE)
    wuq = jnp.pad(uq, ((0, 0), (0, 0), (0, LANES - MLA_NOPE - MLA_ROPE)))
    wuq = wuq.reshape(MLA_Q_RANK, HEADS * LANES).astype(BF16)
    ukv = w_ukv.reshape(MLA_KV_RANK, HEADS, MLA_NOPE + HEAD_DIM)
    wuk = jnp.pad(ukv[:, :, :MLA_NOPE], ((0, 0), (0, 0), (0, LANES - MLA_NOPE)))
    wuk = wuk.reshape(MLA_KV_RANK, HEADS * LANES).astype(BF16)
    wuvt = ukv[:, :, MLA_NOPE:].reshape(MLA_KV_RANK, HEADS * HEAD_DIM).T.astype(BF16)
    inv = ROPE_THETA ** (-jnp.arange(ROPE_HALF, dtype=F32) / ROPE_HALF)
    inv_row = jnp.zeros((1, LANES), F32)
    inv_row = inv_row.at[0, MLA_NOPE:MLA_NOPE + ROPE_HALF].set(inv)
    inv_row = inv_row.at[0, MLA_NOPE + ROPE_HALF:MLA_NOPE + MLA_ROPE].set(inv)
    return dict(waq=waq, wakv=wakv, war=war, qn=q_norm.reshape(1, -1), kvn=kv_norm.reshape(1, -1),
                wuq=wuq, wuk=wuk, wuvt=wuvt, inv=inv_row)


def _q_scale_row(n_cols, q_starts, width, scale):
    row = np.ones((1, n_cols), np.float32)
    for s in q_starts:
        row[0, s:s + width] = scale
    return jnp.asarray(row)


def kernel(x, p, positions, norm_g, ffn_w_in, ffn_w_out, ple_w_proj, ple_w_gate, rel_bias, mla_w_a, mla_q_norm, mla_kv_norm, mla_w_uq, mla_w_ukv, mla_w_o, dil_w_qkv, dil_w_o, fox_w_qkvf, fox_b_f, fox_w_o):
    batch, seq, d = x.shape
    n = batch * seq
    inner = HEADS * HEAD_DIM
    h = x.reshape(n, d)
    pos = positions.reshape(n, 1)
    for i in range(N_LAYERS):
        mixer, j = i % N_MIXERS, i // N_MIXERS
        g = norm_g[i].reshape(4, 1, d)
        if mixer == 0:
            w = _mla_weights(mla_w_a[j], mla_q_norm[j], mla_kv_norm[j], mla_w_uq[j], mla_w_ukv[j])
            q, k, vt = _mla_proj(h, pos, g[0], w, 512, batch, seq)
            o = _flash(q.reshape(batch, seq, -1), k.reshape(batch, seq, -1), vt, batch, seq,
                       shared_lanes=False, q_off=0, k_off=0)
            h = _oproj(o.reshape(n, inner), mla_w_o[j].astype(BF16), h, g[1], tm=512)
        elif mixer == 1:
            n_cols = dil_w_qkv.shape[-1]
            scale = _q_scale_row(n_cols, [gi * 3 * inner for gi in range(len(DIL_PATTERNS))], inner,
                                 HEAD_DIM ** -0.5)
            qkv = _norm_proj(h, g[0], dil_w_qkv[j].astype(BF16), scale, tm=1024, tn=1536)
            table = rel_bias.reshape(REL_BUCKETS, len(DIL_PATTERNS), HEADS)
            outs, lses = [], []
            for gi in range(len(DIL_PATTERNS)):
                o, lse = _dil_attention(qkv.reshape(batch, seq, n_cols), table[:, gi], gi, batch, seq)
                outs.append(o)
                lses.append(lse)
            h = _oproj_merge(outs, lses, dil_w_o[j].astype(BF16), h, g[1], tm=512)
        else:
            wq = fox_w_qkvf[j]
            w = dict(
                wqk=wq[:, :2 * inner].astype(BF16),
                scale=_q_scale_row(2 * inner, [0], inner, HEAD_DIM ** -0.5),
                wvt=wq[:, 2 * inner:3 * inner].T.astype(BF16),
                wf=jnp.pad(wq[:, 3 * inner:], ((0, 0), (0, LANES - HEADS))).astype(BF16),
                bf=jnp.pad(fox_b_f[j], (0, LANES - HEADS)).reshape(1, LANES),
            )
            a, vt, logf = _fox_proj(h, g[0], w, 512, batch, seq)
            c, ct = _cumsum(logf, batch, seq)
            a3 = a.reshape(batch, seq, 2 * inner)
            o = _flash(a3, a3, vt, batch, seq, shared_lanes=True, q_off=0, k_off=PAIRS, forget=(c, ct))
            h = _oproj(o.reshape(n, inner), fox_w_o[j].astype(BF16), h, g[1], tm=512)
        h = _ffn(h, p[i].reshape(n, D_PLE), g[2], g[3], ffn_w_in[i].astype(BF16),
                 ffn_w_out[i].astype(BF16), ple_w_proj[i].astype(BF16), ple_w_gate[i].astype(BF16),
                 tm=512, tf=1408)
    return h.reshape(batch, seq, d)
```

```python
import functools

import numpy as np
import jax
import jax.numpy as jnp
from jax import lax
from jax.experimental import pallas as pl
from jax.experimental.pallas import tpu as pltpu

F32 = jnp.float32
BF16 = jnp.bfloat16

D_MODEL = 1024
N_LAYERS = 4
N_MIXERS = 3
D_PLE = 256
EPS = 1e-6
NEG = -1e30
D_FF = 2816

HEADS = 16
HEAD_DIM = 64
LANES = 128
PAIRS = HEADS // 2

MLA_Q_RANK = 384
MLA_KV_RANK = 256
MLA_NOPE = 64
MLA_ROPE = 32
ROPE_HALF = MLA_ROPE // 2
ROPE_THETA = 10000.0

DIL_PATTERNS = ((128, 1), (512, 4), (2048, 16))
DIL_BLOCK = 128
REL_BUCKETS = 32
REL_MAX_DIST = 2048

VMEM_LIMIT = 56 * 1024 * 1024


def _params(sem):
    return pltpu.CompilerParams(dimension_semantics=sem, vmem_limit_bytes=VMEM_LIMIT)


def _rms(x, g):
    y = x * lax.rsqrt(jnp.mean(x * x, axis=-1, keepdims=True) + EPS)
    return y * g


def _dot(a, b):
    return jnp.dot(a, b, preferred_element_type=F32)


def _dot_nt(a, b):
    return lax.dot_general(a, b, (((1,), (1,)), ((), ())), preferred_element_type=F32)


def _const_spec(shape):
    nd = len(shape)
    return pl.BlockSpec(shape, lambda *_: (0,) * nd)


ROW_CHUNK = 256


def _dil_proj_body(h_ref, g_ref, w_ref, wvt_ref, scale_ref, qk_ref, vt_ref, slab_ref, perm_ref,
                   *, dilation, seq):
    c = pl.program_id(1)
    n_slab = D_MODEL // LANES

    @pl.when(c == 0)
    def _():
        def norm_rows(i, _):
            rows = pl.ds(pl.multiple_of(i * ROW_CHUNK, ROW_CHUNK), ROW_CHUNK)
            xn = _rms(h_ref[0, rows, :], g_ref[...])
            if dilation == 1:
                perm_ref[rows, :] = xn.astype(BF16)
            else:
                for s in range(n_slab):
                    slab_ref[s, rows, :] = xn[:, s * LANES:(s + 1) * LANES]
            return 0

        lax.fori_loop(0, seq // ROW_CHUNK, norm_rows, 0)
        if dilation > 1:
            sub = seq // dilation
            for r in range(dilation):
                for s in range(n_slab):
                    perm_ref[r * sub:(r + 1) * sub, s * LANES:(s + 1) * LANES] = (
                        slab_ref[s, pl.ds(r, sub, stride=dilation), :].astype(BF16))

    @pl.when(c < 2)
    def _():
        qk_ref[0] = (_dot(perm_ref[...], w_ref[...]) * scale_ref[...]).astype(BF16)

    @pl.when(c == 2)
    def _():
        vt_ref[0] = _dot_nt(wvt_ref[0], perm_ref[...]).astype(BF16)


def _dil_proj(h3, g, w, wvt, scale, group):
    batch, seq, d = h3.shape
    dilation = DIL_PATTERNS[group][1]
    width = HEADS * HEAD_DIM
    col = lambda b, c: (0, group * 3 + jnp.minimum(c, 1))
    return pl.pallas_call(
        functools.partial(_dil_proj_body, dilation=dilation, seq=seq),
        grid=(batch, 3),
        in_specs=[
            pl.BlockSpec((1, seq, d), lambda b, c: (b, 0, 0), pipeline_mode=pl.Buffered(1)),
            pl.BlockSpec((1, d), lambda b, c: (0, 0)),
            pl.BlockSpec((d, width), col),
            pl.BlockSpec((1, width, d), lambda b, c: (group, 0, 0)),
            pl.BlockSpec((1, width), col),
        ],
        out_specs=[
            pl.BlockSpec((1, seq, width), lambda b, c: (b, 0, jnp.minimum(c, 1))),
            pl.BlockSpec((1, width, seq), lambda b, c: (b, 0, 0)),
        ],
        out_shape=[jax.ShapeDtypeStruct((batch, seq, 2 * width), BF16),
                   jax.ShapeDtypeStruct((batch, width, seq), BF16)],
        scratch_shapes=[pltpu.VMEM((d // LANES, seq, LANES), F32), pltpu.VMEM((seq, d), BF16)],
        compiler_params=_params(("arbitrary", "arbitrary")),
        name=f"dil_proj_g{group}",
    )(h3, g, w, wvt, scale)


def _mla_proj_body(h_ref, pos_ref, g_ref, waq_ref, wakv_ref, war_ref, qn_ref, kvn_ref,
                   wuq_ref, wuk_ref, wuvt_ref, inv_ref, q_ref, k_ref, vt_ref):
    xn = _rms(h_ref[...], g_ref[...]).astype(BF16)
    cq = _rms(_dot(xn, waq_ref[...]), qn_ref[...]).astype(BF16)
    ckv = _rms(_dot(xn, wakv_ref[...]), kvn_ref[...]).astype(BF16)
    kr = _dot(xn, war_ref[...])
    q = _dot(cq, wuq_ref[...])
    kn = _dot(ckv, wuk_ref[...])
    vt_ref[0] = _dot_nt(wuvt_ref[...], ckv).astype(BF16)

    lane = lax.broadcasted_iota(jnp.int32, (1, LANES), 1)
    first = (lane >= MLA_NOPE) & (lane < MLA_NOPE + ROPE_HALF)
    second = (lane >= MLA_NOPE + ROPE_HALF) & (lane < MLA_NOPE + MLA_ROPE)
    ang = pos_ref[...].astype(F32) * inv_ref[...]
    cos, sin = jnp.cos(ang), jnp.sin(ang)
    c_tab = jnp.where(lane < MLA_NOPE, 1.0, jnp.where(first | second, cos, 0.0))
    s_tab = jnp.where(first, -sin, jnp.where(second, sin, 0.0))

    def rope(t):
        other = jnp.where(first, pltpu.roll(t, LANES - ROPE_HALF, 1), pltpu.roll(t, ROPE_HALF, 1))
        return t * c_tab + other * s_tab

    kr = rope(kr)
    scale = (MLA_NOPE + MLA_ROPE) ** -0.5
    for hh in range(HEADS):
        sl = slice(hh * LANES, (hh + 1) * LANES)
        q_ref[:, sl] = (rope(q[:, sl]) * scale).astype(BF16)
        k_ref[:, sl] = (kn[:, sl] + kr).astype(BF16)


def _vt_spec(tm, seq, width):
    per = seq // tm
    return pl.BlockSpec((1, width, tm), lambda i: (i // per, 0, i % per))


def _mla_proj(h, pos, g, w, tm, batch, seq):
    n, d = h.shape
    row = lambda i: (i, 0)
    width = HEADS * HEAD_DIM
    return pl.pallas_call(
        _mla_proj_body,
        grid=(n // tm,),
        in_specs=[
            pl.BlockSpec((tm, d), row),
            pl.BlockSpec((tm, 1), row),
            _const_spec((1, d)),
            _const_spec(w["waq"].shape), _const_spec(w["wakv"].shape), _const_spec(w["war"].shape),
            _const_spec((1, MLA_Q_RANK)), _const_spec((1, MLA_KV_RANK)),
            _const_spec(w["wuq"].shape), _const_spec(w["wuk"].shape), _const_spec(w["wuvt"].shape),
            _const_spec((1, LANES)),
        ],
        out_specs=[
            pl.BlockSpec((tm, HEADS * LANES), row),
            pl.BlockSpec((tm, HEADS * LANES), row),
            _vt_spec(tm, seq, width),
        ],
        out_shape=[
            jax.ShapeDtypeStruct((n, HEADS * LANES), BF16),
            jax.ShapeDtypeStruct((n, HEADS * LANES), BF16),
            jax.ShapeDtypeStruct((batch, width, seq), BF16),
        ],
        compiler_params=_params(("parallel",)),
        name="mla_proj",
    )(h, pos, g, w["waq"], w["wakv"], w["war"], w["qn"], w["kvn"],
      w["wuq"], w["wuk"], w["wuvt"], w["inv"])


def _fox_proj_body(h_ref, g_ref, w_ref, scale_ref, wvt_ref, wf_ref, bf_ref, a_ref, vt_ref, logf_ref):
    xn = _rms(h_ref[...], g_ref[...]).astype(BF16)
    a_ref[...] = (_dot(xn, w_ref[...]) * scale_ref[...]).astype(BF16)
    vt_ref[0] = _dot_nt(wvt_ref[...], xn).astype(BF16)
    f = _dot(xn, wf_ref[...]) + bf_ref[...]
    logf_ref[...] = jnp.minimum(f, 0.0) - jnp.log1p(jnp.exp(-jnp.abs(f)))


def _fox_proj(h, g, w, tm, batch, seq):
    n, d = h.shape
    nout = w["wqk"].shape[1]
    width = HEADS * HEAD_DIM
    row = lambda i: (i, 0)
    return pl.pallas_call(
        _fox_proj_body,
        grid=(n // tm,),
        in_specs=[
            pl.BlockSpec((tm, d), row),
            _const_spec((1, d)),
            _const_spec((d, nout)),
            _const_spec((1, nout)),
            _const_spec((width, d)),
            _const_spec((d, LANES)),
            _const_spec((1, LANES)),
        ],
        out_specs=[pl.BlockSpec((tm, nout), row), _vt_spec(tm, seq, width),
                   pl.BlockSpec((tm, LANES), row)],
        out_shape=[jax.ShapeDtypeStruct((n, nout), BF16),
                   jax.ShapeDtypeStruct((batch, width, seq), BF16),
                   jax.ShapeDtypeStruct((n, LANES), F32)],
        compiler_params=_params(("parallel",)),
        name="fox_proj",
    )(h, g, w["wqk"], w["scale"], w["wvt"], w["wf"], w["bf"])


def _cumsum_body(x_ref, c_ref, ct_ref, *, seq, blk):
    r = lax.broadcasted_iota(jnp.int32, (blk, blk), 0)
    c = lax.broadcasted_iota(jnp.int32, (blk, blk), 1)
    tri = (c <= r).astype(F32)
    carry = jnp.zeros((1, LANES), F32)
    for b in range(seq // blk):
        xs = x_ref[0, b * blk:(b + 1) * blk, :]
        cs = lax.dot_general(tri, xs, (((1,), (0,)), ((), ())), precision=lax.Precision.HIGHEST,
                             preferred_element_type=F32) + carry
        c_ref[0, b * blk:(b + 1) * blk, :] = cs
        carry = cs[blk - 1:blk, :]
    ct_ref[0] = c_ref[0].T


def _cumsum(logf, batch, seq):
    x = logf.reshape(batch, seq, LANES)
    return pl.pallas_call(
        functools.partial(_cumsum_body, seq=seq, blk=256),
        grid=(batch,),
        in_specs=[pl.BlockSpec((1, seq, LANES), lambda b: (b, 0, 0))],
        out_specs=[pl.BlockSpec((1, seq, LANES), lambda b: (b, 0, 0)),
                   pl.BlockSpec((1, LANES, seq), lambda b: (b, 0, 0))],
        out_shape=[jax.ShapeDtypeStruct((batch, seq, LANES), F32),
                   jax.ShapeDtypeStruct((batch, LANES, seq), F32)],
        compiler_params=_params(("parallel",)),
        name="fox_cumsum",
    )(x)


def _flash_body(*refs, seq, tq, forget, shared_lanes):
    if forget:
        q_ref, k_ref, vt_ref, c_ref, ct_ref, o_ref, ck_ref = refs
    else:
        q_ref, k_ref, vt_ref, o_ref = refs
    pair = pl.program_id(1)
    lane = lax.broadcasted_iota(jnp.int32, (1, LANES), 1)
    low = lane < HEAD_DIM
    causal = (lax.broadcasted_iota(jnp.int32, (tq, tq), 0)
              <= lax.broadcasted_iota(jnp.int32, (tq, tq), 1))
    if forget:
        for e in range(2):
            col = jnp.sum(jnp.where(lane == 2 * pair + e, c_ref[0], 0.0), axis=1, keepdims=True)
            ck_ref[e] = jnp.broadcast_to(col, (seq, tq))

    for qi in range(seq // tq):
        qs = qi * tq
        outs = []
        for e in range(2):
            if shared_lanes:
                qp = q_ref[0, qs:qs + tq, :]
                qe = jnp.where(low, qp, 0) if e == 0 else jnp.where(low, 0, qp)
                ksl = slice(0, LANES)
            else:
                ksl = slice(e * LANES, (e + 1) * LANES)
                qe = q_ref[0, qs:qs + tq, ksl]
            vsl = slice(e * HEAD_DIM, (e + 1) * HEAD_DIM)
            if forget:
                cq = ct_ref[0, pl.ds(2 * pair + e, 1), qs:qs + tq]

            s_d = _dot_nt(k_ref[0, qs:qs + tq, ksl], qe)
            if forget:
                s_d = s_d + (cq - ck_ref[e, qs:qs + tq, :])
            s_d = jnp.where(causal, s_d, NEG)
            m = jnp.max(s_d, axis=0, keepdims=True)
            if qi > 0:
                s_m = _dot_nt(k_ref[0, 0:qs, ksl], qe)
                if forget:
                    s_m = s_m + (cq - ck_ref[e, 0:qs, :])
                m = jnp.maximum(m, jnp.max(s_m, axis=0, keepdims=True))
                p_m = jnp.exp(s_m - m)
                l = jnp.sum(p_m, axis=0, keepdims=True)
                acc = _dot(vt_ref[0, vsl, 0:qs], p_m.astype(BF16))
            p_d = jnp.exp(s_d - m)
            l_d = jnp.sum(p_d, axis=0, keepdims=True)
            acc_d = _dot(vt_ref[0, vsl, qs:qs + tq], p_d.astype(BF16))
            if qi > 0:
                l, acc = l + l_d, acc + acc_d
            else:
                l, acc = l_d, acc_d
            outs.append(acc / l)
        o_t = jnp.concatenate(outs, axis=0)
        o_ref[0, qs:qs + tq, :] = o_t.T.astype(o_ref.dtype)


def _flash(q, k, vt, batch, seq, *, shared_lanes, q_off, k_off, forget=None, tq=256):
    qk_w = LANES if shared_lanes else 2 * LANES
    in_specs = [
        pl.BlockSpec((1, seq, qk_w), lambda b, p: (b, 0, q_off + p)),
        pl.BlockSpec((1, seq, qk_w), lambda b, p: (b, 0, k_off + p)),
        pl.BlockSpec((1, LANES, seq), lambda b, p: (b, p, 0)),
    ]
    args = [q, k, vt]
    scratch = []
    if forget is not None:
        c, ct = forget
        in_specs += [pl.BlockSpec((1, seq, LANES), lambda b, p: (b, 0, 0)),
                     pl.BlockSpec((1, HEADS, seq), lambda b, p: (b, 0, 0))]
        args += [c, ct]
        scratch = [pltpu.VMEM((2, seq, tq), F32)]
    return pl.pallas_call(
        functools.partial(_flash_body, seq=seq, tq=tq, forget=forget is not None,
                          shared_lanes=shared_lanes),
        grid=(batch, PAIRS),
        in_specs=in_specs,
        out_specs=pl.BlockSpec((1, seq, LANES), lambda b, p: (b, 0, p)),
        out_shape=jax.ShapeDtypeStruct((batch, seq, HEADS * HEAD_DIM), BF16),
        scratch_shapes=scratch,
        compiler_params=_params(("parallel", "parallel")),
        name="flash_fox" if forget is not None else "flash_mla",
    )(*args)


def _t5_bucket_np(dist):
    max_exact = REL_BUCKETS // 2
    n = np.maximum(dist.astype(np.float32), np.float32(1.0))
    large = max_exact + (np.log(n / np.float32(max_exact)) / np.float32(np.log(REL_MAX_DIST / max_exact))
                         * np.float32(REL_BUCKETS - max_exact)).astype(np.int32)
    large = np.minimum(large, REL_BUCKETS - 1)
    return np.where(dist < max_exact, dist, large).astype(np.int32)


def _dil_bucket_map(dilation, use_prev):
    qry = np.arange(DIL_BLOCK)
    key = np.arange(2 * DIL_BLOCK)
    rel = DIL_BLOCK + qry[None, :] - key[:, None]
    bk = _t5_bucket_np(np.clip(rel, 0, None) * dilation)
    if not use_prev:
        bk = bk[DIL_BLOCK:]
    return np.concatenate([bk, bk], axis=1)


def _dil_body(tab_ref, bucket_ref, q_ref, k_ref, vt_ref, o_ref, lse_ref, bias_ref, *, span, n_blk, seq):
    blk = DIL_BLOCK
    use_prev = n_blk > 1
    nkeys = 2 * blk if use_prev else blk

    @pl.when(pl.program_id(0) == 0)
    def _():
        bk = bucket_ref[...]
        second = lax.broadcasted_iota(jnp.int32, (1, 2 * blk), 1) >= blk

        def fill(p, _):
            acc = jnp.zeros((nkeys, 2 * blk), F32)
            for bb in range(REL_BUCKETS):
                val = jnp.where(second, tab_ref[bb, 2 * p + 1], tab_ref[bb, 2 * p])
                acc = jnp.where(bk == bb, val, acc)
            bias_ref[p] = acc
            return 0

        lax.fori_loop(0, PAIRS, fill, 0)

    key = lax.broadcasted_iota(jnp.int32, (nkeys, 2 * blk), 0) + (0 if use_prev else blk)
    qry = lax.broadcasted_iota(jnp.int32, (nkeys, 2 * blk), 1) & (blk - 1)
    rel = blk + qry - key
    band = (rel >= 0) & (rel <= span)
    low = lax.broadcasted_iota(jnp.int32, (1, LANES), 1) < HEAD_DIM
    head_row = lax.broadcasted_iota(jnp.int32, (LANES, blk), 0)

    def block(s, _):
        cur = pl.ds(pl.multiple_of(s * blk, blk), blk)
        if use_prev:
            n = s % n_blk
            prv = pl.ds(pl.multiple_of(jnp.where(n > 0, s - 1, s) * blk, blk), blk)
            valid = band & ((n > 0) | (key >= blk))
        else:
            valid = band
        lse_t = jnp.zeros((LANES, blk), F32)
        for p in range(PAIRS):
            sl = slice(p * LANES, (p + 1) * LANES)
            qp = q_ref[0, cur, sl]
            qq = jnp.concatenate([jnp.where(low, qp, 0), jnp.where(low, 0, qp)], axis=0)
            if use_prev:
                kk = jnp.concatenate([k_ref[0, prv, sl], k_ref[0, cur, sl]], axis=0)
                vv = jnp.concatenate([vt_ref[0, sl, prv], vt_ref[0, sl, cur]], axis=1)
            else:
                kk = k_ref[0, cur, sl]
                vv = vt_ref[0, sl, cur]
            st = jnp.where(valid, _dot_nt(kk, qq) + bias_ref[p], NEG)
            m = jnp.max(st, axis=0, keepdims=True)
            ex = jnp.exp(st - m)
            l = jnp.sum(ex, axis=0, keepdims=True)
            ot = _dot(vv, ex.astype(BF16)) / l
            o_t = jnp.concatenate([ot[:HEAD_DIM, :blk], ot[HEAD_DIM:, blk:]], axis=0)
            o_ref[0, cur, sl] = o_t.T.astype(o_ref.dtype)
            lse = m + jnp.log(l)
            lse_t = jnp.where(head_row == 2 * p, lse[:, :blk],
                              jnp.where(head_row == 2 * p + 1, lse[:, blk:], lse_t))
        lse_ref[0, cur, :] = lse_t.T
        return 0

    lax.fori_loop(0, seq // blk, block, 0)


def _dil_attention(qk, vt, table, group, batch, seq):
    window, dilation = DIL_PATTERNS[group]
    span = window // dilation
    sub = seq // dilation
    assert sub % DIL_BLOCK == 0
    n_blk = sub // DIL_BLOCK
    width = HEADS * HEAD_DIM
    bucket = jnp.asarray(_dil_bucket_map(dilation, n_blk > 1))
    return pl.pallas_call(
        functools.partial(_dil_body, span=span, n_blk=n_blk, seq=seq),
        grid=(batch,),
        in_specs=[
            pl.BlockSpec(memory_space=pltpu.SMEM),
            _const_spec(bucket.shape),
            pl.BlockSpec((1, seq, width), lambda b: (b, 0, 0)),
            pl.BlockSpec((1, seq, width), lambda b: (b, 0, 1)),
            pl.BlockSpec((1, width, seq), lambda b: (b, 0, 0)),
        ],
        out_specs=[pl.BlockSpec((1, seq, width), lambda b: (b, 0, 0)),
                   pl.BlockSpec((1, seq, LANES), lambda b: (b, 0, 0))],
        out_shape=[jax.ShapeDtypeStruct((batch, seq, width), BF16),
                   jax.ShapeDtypeStruct((batch, seq, LANES), F32)],
        scratch_shapes=[pltpu.VMEM((PAIRS,) + bucket.shape, F32)],
        compiler_params=_params(("arbitrary",)),
        name=f"dil_attn_g{group}",
    )(table, bucket, qk, qk, vt)


def _oproj_body(o_ref, w_ref, h_ref, g_ref, out_ref):
    out_ref[...] = h_ref[...] + _rms(_dot(o_ref[...], w_ref[...]), g_ref[...])


def _oproj(o, w, h, g, tm):
    n, d = h.shape
    row = lambda i: (i, 0)
    return pl.pallas_call(
        _oproj_body,
        grid=(n // tm,),
        in_specs=[pl.BlockSpec((tm, o.shape[1]), row), _const_spec(w.shape),
                  pl.BlockSpec((tm, d), row), _const_spec((1, d))],
        out_specs=pl.BlockSpec((tm, d), row),
        out_shape=jax.ShapeDtypeStruct((n, d), F32),
        compiler_params=_params(("parallel",)),
        name="oproj",
    )(o, w, h, g)


def _oproj_merge_body(o0_ref, o1_ref, o2_ref, l0_ref, l1_ref, l2_ref, w_ref, h_ref, g_ref, out_ref,
                      lse_ref, slab_ref, merged_ref):
    o_refs = (o0_ref, o1_ref, o2_ref)
    l_refs = (l0_ref, l1_ref, l2_ref)
    n_groups = len(DIL_PATTERNS)
    tm = h_ref.shape[0]
    lses = []
    for gi in range(n_groups):
        dil = DIL_PATTERNS[gi][1]
        if dil == 1:
            lses.append(l_refs[gi][0, 0])
            continue
        for r in range(dil):
            rows = pl.ds(r, tm // dil, stride=dil)
            lse_ref[gi, rows, :] = l_refs[gi][0, r]
            for s in range(PAIRS):
                slab_ref[gi, s, rows, :] = o_refs[gi][0, r, :, s * LANES:(s + 1) * LANES].astype(F32)
        lses.append(lse_ref[gi])
    mx = functools.reduce(jnp.maximum, lses)
    ex = [jnp.exp(t - mx) for t in lses]
    den = functools.reduce(jnp.add, ex)
    alpha = [t / den for t in ex]
    low = lax.broadcasted_iota(jnp.int32, (1, LANES), 1) < HEAD_DIM
    for p in range(PAIRS):
        sl = slice(p * LANES, (p + 1) * LANES)
        acc = jnp.zeros((tm, LANES), F32)
        for gi in range(n_groups):
            a = jnp.where(low, alpha[gi][:, 2 * p:2 * p + 1], alpha[gi][:, 2 * p + 1:2 * p + 2])
            if DIL_PATTERNS[gi][1] == 1:
                acc = acc + a * o_refs[gi][0, 0, :, sl].astype(F32)
            else:
                acc = acc + a * slab_ref[gi, p]
        merged_ref[:, sl] = acc.astype(BF16)
    out_ref[...] = h_ref[...] + _rms(_dot(merged_ref[...], w_ref[...]), g_ref[...])


def _oproj_merge(os_, lses, w, h, g, tm, batch, seq):
    n, d = h.shape
    row = lambda i: (i, 0)
    width = os_[0].shape[-1]
    per = seq // tm
    n_groups = len(DIL_PATTERNS)

    def grouped(x, dil):
        return x.reshape(batch, dil, seq // dil, x.shape[-1])

    def grouped_spec(dil, cols):
        return pl.BlockSpec((1, dil, tm // dil, cols), lambda i: (i // per, 0, i % per, 0))

    dils = [dil for _, dil in DIL_PATTERNS]
    return pl.pallas_call(
        _oproj_merge_body,
        grid=(n // tm,),
        in_specs=[grouped_spec(dil, width) for dil in dils] + [grouped_spec(dil, LANES) for dil in dils]
        + [_const_spec(w.shape), pl.BlockSpec((tm, d), row), _const_spec((1, d))],
        out_specs=pl.BlockSpec((tm, d), row),
        out_shape=jax.ShapeDtypeStruct((n, d), F32),
        scratch_shapes=[pltpu.VMEM((n_groups, tm, LANES), F32),
                        pltpu.VMEM((n_groups, PAIRS, tm, LANES), F32),
                        pltpu.VMEM((tm, width), BF16)],
        compiler_params=_params(("parallel",)),
        name="oproj_merge",
    )(*[grouped(o, dil) for o, dil in zip(os_, dils)],
      *[grouped(t, dil) for t, dil in zip(lses, dils)], w, h, g)


def _ffn_body(h_ref, p_ref, g2_ref, g3_ref, wg_ref, wu_ref, wo_ref, wproj_ref, wgate_ref, out_ref,
              xn_ref, acc_ref, *, nf):
    j = pl.program_id(1)

    @pl.when(j == 0)
    def _():
        xn_ref[...] = _rms(h_ref[...], g2_ref[...]).astype(BF16)

    xn = xn_ref[...]
    gate = _dot(xn, wg_ref[...])
    up = _dot(xn, wu_ref[...])
    act = (gate * jax.nn.sigmoid(gate) * up).astype(BF16)
    part = _dot(act, wo_ref[...])

    @pl.when(j == 0)
    def _():
        acc_ref[...] = part

    @pl.when(j > 0)
    def _():
        acc_ref[...] += part

    @pl.when(j == nf - 1)
    def _():
        h2 = h_ref[...] + _rms(acc_ref[...], g3_ref[...])
        emb = _dot(p_ref[...].astype(BF16), wproj_ref[...])
        out_ref[...] = h2 + emb * jax.nn.sigmoid(_dot(h2.astype(BF16), wgate_ref[...]))


def _ffn(h, p, g2, g3, w_in, w_out, w_proj, w_gate, tm, tf):
    n, d = h.shape
    nf = D_FF // tf
    return pl.pallas_call(
        functools.partial(_ffn_body, nf=nf),
        grid=(n // tm, nf),
        in_specs=[
            pl.BlockSpec((tm, d), lambda i, j: (i, 0)),
            pl.BlockSpec((tm, D_PLE), lambda i, j: (i, 0)),
            pl.BlockSpec((1, d), lambda i, j: (0, 0)),
            pl.BlockSpec((1, d), lambda i, j: (0, 0)),
            pl.BlockSpec((d, tf), lambda i, j: (0, j)),
            pl.BlockSpec((d, tf), lambda i, j: (0, nf + j)),
            pl.BlockSpec((tf, d), lambda i, j: (j, 0)),
            pl.BlockSpec((D_PLE, d), lambda i, j: (0, 0)),
            pl.BlockSpec((d, d), lambda i, j: (0, 0)),
        ],
        out_specs=pl.BlockSpec((tm, d), lambda i, j: (i, 0)),
        out_shape=jax.ShapeDtypeStruct((n, d), F32),
        scratch_shapes=[pltpu.VMEM((tm, d), BF16), pltpu.VMEM((tm, d), F32)],
        compiler_params=_params(("parallel", "arbitrary")),
        name="ffn",
    )(h, p, g2, g3, w_in, w_in, w_out, w_proj, w_gate)


def _mla_weights(w_a, q_norm, kv_norm, w_uq, w_ukv):
    d = w_a.shape[0]
    waq = w_a[:, :MLA_Q_RANK].astype(BF16)
    wakv = w_a[:, MLA_Q_RANK:MLA_Q_RANK + MLA_KV_RANK].astype(BF16)
    war = jnp.zeros((d, LANES), F32).at[:, MLA_NOPE:MLA_NOPE + MLA_ROPE].set(
        w_a[:, MLA_Q_RANK + MLA_KV_RANK:]).astype(BF16)
    uq = w_uq.reshape(MLA_Q_RANK, HEADS, MLA_NOPE + MLA_ROPE)
    wuq = jnp.pad(uq, ((0, 0), (0, 0), (0, LANES - MLA_NOPE - MLA_ROPE)))
    wuq = wuq.reshape(MLA_Q_RANK, HEADS * LANES).astype(BF16)
    ukv = w_ukv.reshape(MLA_KV_RANK, HEADS, MLA_NOPE + HEAD_DIM)
    wuk = jnp.pad(ukv[:, :, :MLA_NOPE], ((0, 0), (0, 0), (0, LANES - MLA_NOPE)))
    wuk = wuk.reshape(MLA_KV_RANK, HEADS * LANES).astype(BF16)
    wuvt = ukv[:, :, MLA_NOPE:].reshape(MLA_KV_RANK, HEADS * HEAD_DIM).T.astype(BF16)
    inv = ROPE_THETA ** (-jnp.arange(ROPE_HALF, dtype=F32) / ROPE_HALF)
    inv_row = jnp.zeros((1, LANES), F32)
    inv_row = inv_row.at[0, MLA_NOPE:MLA_NOPE + ROPE_HALF].set(inv)
    inv_row = inv_row.at[0, MLA_NOPE + ROPE_HALF:MLA_NOPE + MLA_ROPE].set(inv)
    return dict(waq=waq, wakv=wakv, war=war, qn=q_norm.reshape(1, -1), kvn=kv_norm.reshape(1, -1),
                wuq=wuq, wuk=wuk, wuvt=wuvt, inv=inv_row)


def _q_scale_row(n_cols, q_starts, width, scale):
    row = np.ones((1, n_cols), np.float32)
    for s in q_starts:
        row[0, s:s + width] = scale
    return jnp.asarray(row)


def kernel(x, p, positions, norm_g, ffn_w_in, ffn_w_out, ple_w_proj, ple_w_gate, rel_bias, mla_w_a, mla_q_norm, mla_kv_norm, mla_w_uq, mla_w_ukv, mla_w_o, dil_w_qkv, dil_w_o, fox_w_qkvf, fox_b_f, fox_w_o):
    batch, seq, d = x.shape
    n = batch * seq
    inner = HEADS * HEAD_DIM
    h = x.reshape(n, d)
    pos = positions.reshape(n, 1)
    for i in range(N_LAYERS):
        mixer, j = i % N_MIXERS, i // N_MIXERS
        g = norm_g[i].reshape(4, 1, d)
        if mixer == 0:
            w = _mla_weights(mla_w_a[j], mla_q_norm[j], mla_kv_norm[j], mla_w_uq[j], mla_w_ukv[j])
            q, k, vt = _mla_proj(h, pos, g[0], w, 512, batch, seq)
            o = _flash(q.reshape(batch, seq, -1), k.reshape(batch, seq, -1), vt, batch, seq,
                       shared_lanes=False, q_off=0, k_off=0)
            h = _oproj(o.reshape(n, inner), mla_w_o[j].astype(BF16), h, g[1], tm=512)
        elif mixer == 1:
            n_cols = dil_w_qkv.shape[-1]
            scale = _q_scale_row(n_cols, [gi * 3 * inner for gi in range(len(DIL_PATTERNS))], inner,
                                 HEAD_DIM ** -0.5)
            wq = dil_w_qkv[j]
            wvt = jnp.stack([wq[:, (3 * gi + 2) * inner:(3 * gi + 3) * inner].T
                             for gi in range(len(DIL_PATTERNS))]).astype(BF16)
            wq = wq.astype(BF16)
            table = rel_bias.reshape(REL_BUCKETS, len(DIL_PATTERNS), HEADS)
            h3 = h.reshape(batch, seq, d)
            outs, lses = [], []
            for gi in range(len(DIL_PATTERNS)):
                qk, vt = _dil_proj(h3, g[0], wq, wvt, scale, gi)
                o, lse = _dil_attention(qk, vt, table[:, gi], gi, batch, seq)
                outs.append(o)
                lses.append(lse)
            h = _oproj_merge(outs, lses, dil_w_o[j].astype(BF16), h, g[1], 512, batch, seq)
        else:
            wq = fox_w_qkvf[j]
            w = dict(
                wqk=wq[:, :2 * inner].astype(BF16),
                scale=_q_scale_row(2 * inner, [0], inner, HEAD_DIM ** -0.5),
                wvt=wq[:, 2 * inner:3 * inner].T.astype(BF16),
                wf=jnp.pad(wq[:, 3 * inner:], ((0, 0), (0, LANES - HEADS))).astype(BF16),
                bf=jnp.pad(fox_b_f[j], (0, LANES - HEADS)).reshape(1, LANES),
            )
            a, vt, logf = _fox_proj(h, g[0], w, 512, batch, seq)
            c, ct = _cumsum(logf, batch, seq)
            a3 = a.reshape(batch, seq, 2 * inner)
            o = _flash(a3, a3, vt, batch, seq, shared_lanes=True, q_off=0, k_off=PAIRS, forget=(c, ct))
            h = _oproj(o.reshape(n, inner), fox_w_o[j].astype(BF16), h, g[1], tm=512)
        h = _ffn(h, p[i].reshape(n, D_PLE), g[2], g[3], ffn_w_in[i].astype(BF16),
                 ffn_w_out[i].astype(BF16), ple_w_proj[i].astype(BF16), ple_w_gate[i].astype(BF16),
                 tm=512, tf=1408)
    return h.reshape(batch, seq, d)
```

```python
import functools

import numpy as np
import jax
import jax.numpy as jnp
from jax import lax
from jax.experimental import pallas as pl
from jax.experimental.pallas import tpu as pltpu

F32 = jnp.float32
BF16 = jnp.bfloat16

D_MODEL = 1024
N_LAYERS = 4
N_MIXERS = 3
D_PLE = 256
EPS = 1e-6
NEG = -1e30
D_FF = 2816

HEADS = 16
HEAD_DIM = 64
LANES = 128
PAIRS = HEADS // 2
ONES_ROWS = 16
LOG2E = 1.4426950408889634

MLA_Q_RANK = 384
MLA_KV_RANK = 256
MLA_NOPE = 64
MLA_ROPE = 32
ROPE_HALF = MLA_ROPE // 2
ROPE_THETA = 10000.0

DIL_PATTERNS = ((128, 1), (512, 4), (2048, 16))
DIL_BLOCK = 128
REL_BUCKETS = 32
REL_MAX_DIST = 2048

VMEM_LIMIT = 56 * 1024 * 1024


def _params(sem):
    return pltpu.CompilerParams(dimension_semantics=sem, vmem_limit_bytes=VMEM_LIMIT)


def _rms(x, g):
    y = x * lax.rsqrt(jnp.mean(x * x, axis=-1, keepdims=True) + EPS)
    return y * g


def _dot(a, b):
    return jnp.dot(a, b, preferred_element_type=F32)


def _dot_nt(a, b):
    return lax.dot_general(a, b, (((1,), (1,)), ((), ())), preferred_element_type=F32)


def _const_spec(shape):
    nd = len(shape)
    return pl.BlockSpec(shape, lambda *_: (0,) * nd)


ROW_CHUNK = 256


def _dil_proj_body(h_ref, g_ref, w_ref, wvt_ref, scale_ref, qk_ref, vt_ref, slab_ref, perm_ref,
                   *, dilation, seq):
    c = pl.program_id(1)
    n_slab = D_MODEL // LANES

    @pl.when(c == 0)
    def _():
        def norm_rows(i, _):
            rows = pl.ds(pl.multiple_of(i * ROW_CHUNK, ROW_CHUNK), ROW_CHUNK)
            xn = _rms(h_ref[0, rows, :], g_ref[...])
            if dilation == 1:
                perm_ref[rows, :] = xn.astype(BF16)
            else:
                for s in range(n_slab):
                    slab_ref[s, rows, :] = xn[:, s * LANES:(s + 1) * LANES]
            return 0

        lax.fori_loop(0, seq // ROW_CHUNK, norm_rows, 0)
        if dilation > 1:
            sub = seq // dilation
            for r in range(dilation):
                for s in range(n_slab):
                    perm_ref[r * sub:(r + 1) * sub, s * LANES:(s + 1) * LANES] = (
                        slab_ref[s, pl.ds(r, sub, stride=dilation), :].astype(BF16))

    @pl.when(c < 2)
    def _():
        qk_ref[0] = (_dot(perm_ref[...], w_ref[...]) * scale_ref[...]).astype(BF16)

    @pl.when(c == 2)
    def _():
        vt_ref[0] = _dot_nt(wvt_ref[0], perm_ref[...]).astype(BF16)


def _dil_proj(h3, g, w, wvt, scale, group):
    batch, seq, d = h3.shape
    dilation = DIL_PATTERNS[group][1]
    width = HEADS * HEAD_DIM
    col = lambda b, c: (0, group * 3 + jnp.minimum(c, 1))
    return pl.pallas_call(
        functools.partial(_dil_proj_body, dilation=dilation, seq=seq),
        grid=(batch, 3),
        in_specs=[
            pl.BlockSpec((1, seq, d), lambda b, c: (b, 0, 0), pipeline_mode=pl.Buffered(1)),
            pl.BlockSpec((1, d), lambda b, c: (0, 0)),
            pl.BlockSpec((d, width), col),
            pl.BlockSpec((1, width, d), lambda b, c: (group, 0, 0)),
            pl.BlockSpec((1, width), col),
        ],
        out_specs=[
            pl.BlockSpec((1, seq, width), lambda b, c: (b, 0, jnp.minimum(c, 1))),
            pl.BlockSpec((1, width, seq), lambda b, c: (b, 0, 0)),
        ],
        out_shape=[jax.ShapeDtypeStruct((batch, seq, 2 * width), BF16),
                   jax.ShapeDtypeStruct((batch, width, seq), BF16)],
        scratch_shapes=[pltpu.VMEM((d // LANES, seq, LANES), F32), pltpu.VMEM((seq, d), BF16)],
        compiler_params=_params(("arbitrary", "arbitrary")),
        name=f"dil_proj_g{group}",
    )(h3, g, w, wvt, scale)


def _mla_proj_body(h_ref, pos_ref, g_ref, waq_ref, wakv_ref, war_ref, qn_ref, kvn_ref,
                   wuq_ref, wuk_ref, wuvt_ref, inv_ref, q_ref, k_ref, vt_ref):
    xn = _rms(h_ref[...], g_ref[...]).astype(BF16)
    cq = _rms(_dot(xn, waq_ref[...]), qn_ref[...]).astype(BF16)
    ckv = _rms(_dot(xn, wakv_ref[...]), kvn_ref[...]).astype(BF16)
    kr = _dot(xn, war_ref[...])
    q = _dot(cq, wuq_ref[...])
    kn = _dot(ckv, wuk_ref[...])
    vt_ref[0] = _dot_nt(wuvt_ref[...], ckv).astype(BF16)

    lane = lax.broadcasted_iota(jnp.int32, (1, LANES), 1)
    first = (lane >= MLA_NOPE) & (lane < MLA_NOPE + ROPE_HALF)
    second = (lane >= MLA_NOPE + ROPE_HALF) & (lane < MLA_NOPE + MLA_ROPE)
    ang = pos_ref[...].astype(F32) * inv_ref[...]
    cos, sin = jnp.cos(ang), jnp.sin(ang)
    c_tab = jnp.where(lane < MLA_NOPE, 1.0, jnp.where(first | second, cos, 0.0))
    s_tab = jnp.where(first, -sin, jnp.where(second, sin, 0.0))

    def rope(t):
        other = jnp.where(first, pltpu.roll(t, LANES - ROPE_HALF, 1), pltpu.roll(t, ROPE_HALF, 1))
        return t * c_tab + other * s_tab

    kr = rope(kr)
    scale = (MLA_NOPE + MLA_ROPE) ** -0.5 * LOG2E
    for hh in range(HEADS):
        sl = slice(hh * LANES, (hh + 1) * LANES)
        q_ref[:, sl] = (rope(q[:, sl]) * scale).astype(BF16)
        k_ref[:, sl] = (kn[:, sl] + kr).astype(BF16)


def _vt_spec(tm, seq, width):
    per = seq // tm
    return pl.BlockSpec((1, width, tm), lambda i: (i // per, 0, i % per))


def _mla_proj(h, pos, g, w, tm, batch, seq):
    n, d = h.shape
    row = lambda i: (i, 0)
    width = HEADS * HEAD_DIM
    return pl.pallas_call(
        _mla_proj_body,
        grid=(n // tm,),
        in_specs=[
            pl.BlockSpec((tm, d), row),
            pl.BlockSpec((tm, 1), row),
            _const_spec((1, d)),
            _const_spec(w["waq"].shape), _const_spec(w["wakv"].shape), _const_spec(w["war"].shape),
            _const_spec((1, MLA_Q_RANK)), _const_spec((1, MLA_KV_RANK)),
            _const_spec(w["wuq"].shape), _const_spec(w["wuk"].shape), _const_spec(w["wuvt"].shape),
            _const_spec((1, LANES)),
        ],
        out_specs=[
            pl.BlockSpec((tm, HEADS * LANES), row),
            pl.BlockSpec((tm, HEADS * LANES), row),
            _vt_spec(tm, seq, width),
        ],
        out_shape=[
            jax.ShapeDtypeStruct((n, HEADS * LANES), BF16),
            jax.ShapeDtypeStruct((n, HEADS * LANES), BF16),
            jax.ShapeDtypeStruct((batch, width, seq), BF16),
        ],
        compiler_params=_params(("parallel",)),
        name="mla_proj",
    )(h, pos, g, w["waq"], w["wakv"], w["war"], w["qn"], w["kvn"],
      w["wuq"], w["wuk"], w["wuvt"], w["inv"])


def _fox_proj_body(h_ref, g_ref, w_ref, scale_ref, wvt_ref, wf_ref, bf_ref, a_ref, vt_ref, logf_ref):
    xn = _rms(h_ref[...], g_ref[...]).astype(BF16)
    a_ref[...] = (_dot(xn, w_ref[...]) * scale_ref[...]).astype(BF16)
    vt_ref[0] = _dot_nt(wvt_ref[...], xn).astype(BF16)
    f = _dot(xn, wf_ref[...]) + bf_ref[...]
    logf_ref[...] = jnp.minimum(f, 0.0) - jnp.log1p(jnp.exp(-jnp.abs(f)))


def _fox_proj(h, g, w, tm, batch, seq):
    n, d = h.shape
    nout = w["wqk"].shape[1]
    width = HEADS * HEAD_DIM
    row = lambda i: (i, 0)
    return pl.pallas_call(
        _fox_proj_body,
        grid=(n // tm,),
        in_specs=[
            pl.BlockSpec((tm, d), row),
            _const_spec((1, d)),
            _const_spec((d, nout)),
            _const_spec((1, nout)),
            _const_spec((width, d)),
            _const_spec((d, LANES)),
            _const_spec((1, LANES)),
        ],
        out_specs=[pl.BlockSpec((tm, nout), row), _vt_spec(tm, seq, width),
                   pl.BlockSpec((tm, LANES), row)],
        out_shape=[jax.ShapeDtypeStruct((n, nout), BF16),
                   jax.ShapeDtypeStruct((batch, width, seq), BF16),
                   jax.ShapeDtypeStruct((n, LANES), F32)],
        compiler_params=_params(("parallel",)),
        name="fox_proj",
    )(h, g, w["wqk"], w["scale"], w["wvt"], w["wf"], w["bf"])


def _cumsum_body(x_ref, c_ref, ct_ref, *, seq, blk):
    r = lax.broadcasted_iota(jnp.int32, (blk, blk), 0)
    c = lax.broadcasted_iota(jnp.int32, (blk, blk), 1)
    tri = (c <= r).astype(F32)
    carry = jnp.zeros((1, LANES), F32)
    for b in range(seq // blk):
        xs = x_ref[0, b * blk:(b + 1) * blk, :]
        cs = lax.dot_general(tri, xs, (((1,), (0,)), ((), ())), precision=lax.Precision.HIGHEST,
                             preferred_element_type=F32) + carry
        c_ref[0, b * blk:(b + 1) * blk, :] = cs
        carry = cs[blk - 1:blk, :]
    ct_ref[0] = c_ref[0].T


def _cumsum(logf, batch, seq):
    x = logf.reshape(batch, seq, LANES)
    return pl.pallas_call(
        functools.partial(_cumsum_body, seq=seq, blk=256),
        grid=(batch,),
        in_specs=[pl.BlockSpec((1, seq, LANES), lambda b: (b, 0, 0))],
        out_specs=[pl.BlockSpec((1, seq, LANES), lambda b: (b, 0, 0)),
                   pl.BlockSpec((1, LANES, seq), lambda b: (b, 0, 0))],
        out_shape=[jax.ShapeDtypeStruct((batch, seq, LANES), F32),
                   jax.ShapeDtypeStruct((batch, LANES, seq), F32)],
        compiler_params=_params(("parallel",)),
        name="fox_cumsum",
    )(x)


def _flash_body(*refs, seq, tq, forget, shared_lanes):
    if forget:
        q_ref, k_ref, vt_ref, c_ref, ct_ref, o_ref, va_ref, ck_ref = refs
    else:
        q_ref, k_ref, vt_ref, o_ref, va_ref = refs
    pair = pl.program_id(1)
    lane = lax.broadcasted_iota(jnp.int32, (1, LANES), 1)
    low = lane < HEAD_DIM
    causal = (lax.broadcasted_iota(jnp.int32, (tq, tq), 0)
              <= lax.broadcasted_iota(jnp.int32, (tq, tq), 1))
    for e in range(2):
        va_ref[e, :HEAD_DIM, :] = vt_ref[0, e * HEAD_DIM:(e + 1) * HEAD_DIM, :]
        va_ref[e, HEAD_DIM:, :] = jnp.ones((ONES_ROWS, seq), BF16)
    if forget:
        for e in range(2):
            col = jnp.sum(jnp.where(lane == 2 * pair + e, c_ref[0], 0.0), axis=1, keepdims=True)
            ck_ref[e] = jnp.broadcast_to(col * LOG2E, (seq, tq))

    class Chain:
        pass

    def start(qi, e):
        ch = Chain()
        qs = qi * tq
        ch.qi, ch.e = qi, e
        if shared_lanes:
            qp = q_ref[0, qs:qs + tq, :]
            ch.q = jnp.where(low, qp, 0) if e == 0 else jnp.where(low, 0, qp)
            ch.ksl = slice(0, LANES)
        else:
            ch.ksl = slice(e * LANES, (e + 1) * LANES)
            ch.q = q_ref[0, qs:qs + tq, ch.ksl]
        if forget:
            ch.cq = ct_ref[0, pl.ds(2 * pair + e, 1), qs:qs + tq] * LOG2E
        ch.s, ch.m, ch.acc = [], None, None
        return ch

    def pass1(ch, c):
        ks = c * tq
        s = _dot_nt(k_ref[0, ks:ks + tq, ch.ksl], ch.q)
        if forget:
            s = s + (ch.cq - ck_ref[ch.e, ks:ks + tq, :])
        if c == ch.qi:
            s = jnp.where(causal, s, NEG)
        mc = jnp.max(s, axis=0, keepdims=True)
        ch.m = mc if ch.m is None else jnp.maximum(ch.m, mc)
        ch.s.append(s)

    def pass2(ch, c):
        ks = c * tq
        p = jnp.exp2(ch.s[c] - ch.m)
        ac = _dot(va_ref[ch.e, :, ks:ks + tq], p.astype(BF16))
        ch.acc = ac if ch.acc is None else ch.acc + ac

    outs = []

    def finish(ch):
        outs.append(ch.acc[:HEAD_DIM] / ch.acc[HEAD_DIM:HEAD_DIM + 1])
        if ch.e == 1:
            qs = ch.qi * tq
            o_t = jnp.concatenate(outs, axis=0)
            o_ref[0, qs:qs + tq, :] = o_t.T.astype(o_ref.dtype)
            outs.clear()

    prev = ()
    for qi in range(seq // tq + 1):
        cur = tuple(start(qi, e) for e in range(2)) if qi < seq // tq else ()
        for c in range(qi + 1):
            for ch in cur:
                pass1(ch, c)
            if c < qi:
                for ch in prev:
                    pass2(ch, c)
        for ch in prev:
            finish(ch)
        prev = cur


def _flash(q, k, vt, batch, seq, *, shared_lanes, q_off, k_off, forget=None, tq=256):
    qk_w = LANES if shared_lanes else 2 * LANES
    in_specs = [
        pl.BlockSpec((1, seq, qk_w), lambda b, p: (b, 0, q_off + p)),
        pl.BlockSpec((1, seq, qk_w), lambda b, p: (b, 0, k_off + p)),
        pl.BlockSpec((1, LANES, seq), lambda b, p: (b, p, 0)),
    ]
    args = [q, k, vt]
    scratch = [pltpu.VMEM((2, HEAD_DIM + ONES_ROWS, seq), BF16)]
    if forget is not None:
        c, ct = forget
        in_specs += [pl.BlockSpec((1, seq, LANES), lambda b, p: (b, 0, 0)),
                     pl.BlockSpec((1, HEADS, seq), lambda b, p: (b, 0, 0))]
        args += [c, ct]
        scratch += [pltpu.VMEM((2, seq, tq), F32)]
    return pl.pallas_call(
        functools.partial(_flash_body, seq=seq, tq=tq, forget=forget is not None,
                          shared_lanes=shared_lanes),
        grid=(batch, PAIRS),
        in_specs=in_specs,
        out_specs=pl.BlockSpec((1, seq, LANES), lambda b, p: (b, 0, p)),
        out_shape=jax.ShapeDtypeStruct((batch, seq, HEADS * HEAD_DIM), BF16),
        scratch_shapes=scratch,
        compiler_params=_params(("parallel", "parallel")),
        name="flash_fox" if forget is not None else "flash_mla",
    )(*args)


def _t5_bucket_np(dist):
    max_exact = REL_BUCKETS // 2
    n = np.maximum(dist.astype(np.float32), np.float32(1.0))
    large = max_exact + (np.log(n / np.float32(max_exact)) / np.float32(np.log(REL_MAX_DIST / max_exact))
                         * np.float32(REL_BUCKETS - max_exact)).astype(np.int32)
    large = np.minimum(large, REL_BUCKETS - 1)
    return np.where(dist < max_exact, dist, large).astype(np.int32)


def _dil_bucket_map(dilation, use_prev):
    qry = np.arange(DIL_BLOCK)
    key = np.arange(2 * DIL_BLOCK)
    rel = DIL_BLOCK + qry[None, :] - key[:, None]
    bk = _t5_bucket_np(np.clip(rel, 0, None) * dilation)
    if not use_prev:
        bk = bk[DIL_BLOCK:]
    return np.concatenate([bk, bk], axis=1)


def _dil_body(tab_ref, bucket_ref, q_ref, k_ref, vt_ref, o_ref, lse_ref, bias_ref, *, span, n_blk, seq):
    blk = DIL_BLOCK
    use_prev = n_blk > 1
    nkeys = 2 * blk if use_prev else blk

    @pl.when(pl.program_id(0) == 0)
    def _():
        bk = bucket_ref[...]
        second = lax.broadcasted_iota(jnp.int32, (1, 2 * blk), 1) >= blk

        def fill(p, _):
            acc = jnp.zeros((nkeys, 2 * blk), F32)
            for bb in range(REL_BUCKETS):
                val = jnp.where(second, tab_ref[bb, 2 * p + 1], tab_ref[bb, 2 * p])
                acc = jnp.where(bk == bb, val * LOG2E, acc)
            bias_ref[p] = acc
            return 0

        lax.fori_loop(0, PAIRS, fill, 0)

    key = lax.broadcasted_iota(jnp.int32, (nkeys, 2 * blk), 0) + (0 if use_prev else blk)
    qry = lax.broadcasted_iota(jnp.int32, (nkeys, 2 * blk), 1) & (blk - 1)
    rel = blk + qry - key
    band = (rel >= 0) & (rel <= span)
    low = lax.broadcasted_iota(jnp.int32, (1, LANES), 1) < HEAD_DIM
    head_row = lax.broadcasted_iota(jnp.int32, (LANES, blk), 0)

    def block(s, _):
        cur = pl.ds(pl.multiple_of(s * blk, blk), blk)
        if use_prev:
            n = s % n_blk
            prv = pl.ds(pl.multiple_of(jnp.where(n > 0, s - 1, s) * blk, blk), blk)
            valid = band & ((n > 0) | (key >= blk))
        else:
            valid = band
        def logits(p):
            sl = slice(p * LANES, (p + 1) * LANES)
            qp = q_ref[0, cur, sl]
            qq = jnp.concatenate([jnp.where(low, qp, 0), jnp.where(low, 0, qp)], axis=0)
            if use_prev:
                kk = jnp.concatenate([k_ref[0, prv, sl], k_ref[0, cur, sl]], axis=0)
            else:
                kk = k_ref[0, cur, sl]
            return _dot_nt(kk, qq)

        def attend(p, raw, lse_t):
            sl = slice(p * LANES, (p + 1) * LANES)
            if use_prev:
                vv = jnp.concatenate([vt_ref[0, sl, prv], vt_ref[0, sl, cur]], axis=1)
            else:
                vv = vt_ref[0, sl, cur]
            vv = jnp.concatenate([vv, jnp.ones((ONES_ROWS, nkeys), BF16)], axis=0)
            st = jnp.where(valid, raw + bias_ref[p], NEG)
            m = jnp.max(st, axis=0, keepdims=True)
            ot = _dot(vv, jnp.exp2(st - m).astype(BF16))
            l = ot[LANES:LANES + 1]
            ot = ot[:LANES] / l
            o_t = jnp.concatenate([ot[:HEAD_DIM, :blk], ot[HEAD_DIM:, blk:]], axis=0)
            o_ref[0, cur, sl] = o_t.T.astype(o_ref.dtype)
            lse = m * (1.0 / LOG2E) + jnp.log(l)
            return jnp.where(head_row == 2 * p, lse[:, :blk],
                             jnp.where(head_row == 2 * p + 1, lse[:, blk:], lse_t))

        lse_t = jnp.zeros((LANES, blk), F32)
        raw = logits(0)
        for p in range(PAIRS):
            nxt = logits(p + 1) if p + 1 < PAIRS else None
            lse_t = attend(p, raw, lse_t)
            raw = nxt
        lse_ref[0, cur, :] = lse_t.T
        return 0

    lax.fori_loop(0, seq // blk, block, 0)


def _dil_attention(qk, vt, table, group, batch, seq):
    window, dilation = DIL_PATTERNS[group]
    span = window // dilation
    sub = seq // dilation
    assert sub % DIL_BLOCK == 0
    n_blk = sub // DIL_BLOCK
    width = HEADS * HEAD_DIM
    bucket = jnp.asarray(_dil_bucket_map(dilation, n_blk > 1))
    return pl.pallas_call(
        functools.partial(_dil_body, span=span, n_blk=n_blk, seq=seq),
        grid=(batch,),
        in_specs=[
            pl.BlockSpec(memory_space=pltpu.SMEM),
            _const_spec(bucket.shape),
            pl.BlockSpec((1, seq, width), lambda b: (b, 0, 0)),
            pl.BlockSpec((1, seq, width), lambda b: (b, 0, 1)),
            pl.BlockSpec((1, width, seq), lambda b: (b, 0, 0)),
        ],
        out_specs=[pl.BlockSpec((1, seq, width), lambda b: (b, 0, 0)),
                   pl.BlockSpec((1, seq, LANES), lambda b: (b, 0, 0))],
        out_shape=[jax.ShapeDtypeStruct((batch, seq, width), BF16),
                   jax.ShapeDtypeStruct((batch, seq, LANES), F32)],
        scratch_shapes=[pltpu.VMEM((PAIRS,) + bucket.shape, F32)],
        compiler_params=_params(("arbitrary",)),
        name=f"dil_attn_g{group}",
    )(table, bucket, qk, qk, vt)


def _oproj_body(o_ref, w_ref, h_ref, g_ref, out_ref):
    out_ref[...] = h_ref[...] + _rms(_dot(o_ref[...], w_ref[...]), g_ref[...])


def _oproj(o, w, h, g, tm):
    n, d = h.shape
    row = lambda i: (i, 0)
    return pl.pallas_call(
        _oproj_body,
        grid=(n // tm,),
        in_specs=[pl.BlockSpec((tm, o.shape[1]), row), _const_spec(w.shape),
                  pl.BlockSpec((tm, d), row), _const_spec((1, d))],
        out_specs=pl.BlockSpec((tm, d), row),
        out_shape=jax.ShapeDtypeStruct((n, d), F32),
        compiler_params=_params(("parallel",)),
        name="oproj",
    )(o, w, h, g)


def _oproj_merge_body(o0_ref, o1_ref, o2_ref, l0_ref, l1_ref, l2_ref, w_ref, h_ref, g_ref, out_ref,
                      lse_ref, slab_ref, merged_ref):
    o_refs = (o0_ref, o1_ref, o2_ref)
    l_refs = (l0_ref, l1_ref, l2_ref)
    n_groups = len(DIL_PATTERNS)
    tm = h_ref.shape[0]
    lses = []
    for gi in range(n_groups):
        dil = DIL_PATTERNS[gi][1]
        if dil == 1:
            lses.append(l_refs[gi][0, 0])
            continue
        for r in range(dil):
            rows = pl.ds(r, tm // dil, stride=dil)
            lse_ref[gi, rows, :] = l_refs[gi][0, r]
            for s in range(PAIRS):
                slab_ref[gi, s, rows, :] = o_refs[gi][0, r, :, s * LANES:(s + 1) * LANES].astype(F32)
        lses.append(lse_ref[gi])
    mx = functools.reduce(jnp.maximum, lses)
    ex = [jnp.exp(t - mx) for t in lses]
    den = functools.reduce(jnp.add, ex)
    alpha = [t / den for t in ex]
    low = lax.broadcasted_iota(jnp.int32, (1, LANES), 1) < HEAD_DIM
    for p in range(PAIRS):
        sl = slice(p * LANES, (p + 1) * LANES)
        acc = jnp.zeros((tm, LANES), F32)
        for gi in range(n_groups):
            a = jnp.where(low, alpha[gi][:, 2 * p:2 * p + 1], alpha[gi][:, 2 * p + 1:2 * p + 2])
            if DIL_PATTERNS[gi][1] == 1:
                acc = acc + a * o_refs[gi][0, 0, :, sl].astype(F32)
            else:
                acc = acc + a * slab_ref[gi, p]
        merged_ref[:, sl] = acc.astype(BF16)
    out_ref[...] = h_ref[...] + _rms(_dot(merged_ref[...], w_ref[...]), g_ref[...])


def _oproj_merge(os_, lses, w, h, g, tm, batch, seq):
    n, d = h.shape
    row = lambda i: (i, 0)
    width = os_[0].shape[-1]
    per = seq // tm
    n_groups = len(DIL_PATTERNS)

    def grouped(x, dil):
        return x.reshape(batch, dil, seq // dil, x.shape[-1])

    def grouped_spec(dil, cols):
        return pl.BlockSpec((1, dil, tm // dil, cols), lambda i: (i // per, 0, i % per, 0))

    dils = [dil for _, dil in DIL_PATTERNS]
    return pl.pallas_call(
        _oproj_merge_body,
        grid=(n // tm,),
        in_specs=[grouped_spec(dil, width) for dil in dils] + [grouped_spec(dil, LANES) for dil in dils]
        + [_const_spec(w.shape), pl.BlockSpec((tm, d), row), _const_spec((1, d))],
        out_specs=pl.BlockSpec((tm, d), row),
        out_shape=jax.ShapeDtypeStruct((n, d), F32),
        scratch_shapes=[pltpu.VMEM((n_groups, tm, LANES), F32),
                        pltpu.VMEM((n_groups, PAIRS, tm, LANES), F32),
                        pltpu.VMEM((tm, width), BF16)],
        compiler_params=_params(("parallel",)),
        name="oproj_merge",
    )(*[grouped(o, dil) for o, dil in zip(os_, dils)],
      *[grouped(t, dil) for t, dil in zip(lses, dils)], w, h, g)


def _ffn_body(h_ref, p_ref, g2_ref, g3_ref, wg_ref, wu_ref, wo_ref, wproj_ref, wgate_ref, out_ref,
              xn_ref, acc_ref, *, nf):
    j = pl.program_id(1)

    @pl.when(j == 0)
    def _():
        xn_ref[...] = _rms(h_ref[...], g2_ref[...]).astype(BF16)

    xn = xn_ref[...]
    gate = _dot(xn, wg_ref[...])
    up = _dot(xn, wu_ref[...])
    act = (gate * jax.nn.sigmoid(gate) * up).astype(BF16)
    part = _dot(act, wo_ref[...])

    @pl.when(j == 0)
    def _():
        acc_ref[...] = part

    @pl.when(j > 0)
    def _():
        acc_ref[...] += part

    @pl.when(j == nf - 1)
    def _():
        h2 = h_ref[...] + _rms(acc_ref[...], g3_ref[...])
        emb = _dot(p_ref[...].astype(BF16), wproj_ref[...])
        out_ref[...] = h2 + emb * jax.nn.sigmoid(_dot(h2.astype(BF16), wgate_ref[...]))


def _ffn(h, p, g2, g3, w_in, w_out, w_proj, w_gate, tm, tf):
    n, d = h.shape
    nf = D_FF // tf
    return pl.pallas_call(
        functools.partial(_ffn_body, nf=nf),
        grid=(n // tm, nf),
        in_specs=[
            pl.BlockSpec((tm, d), lambda i, j: (i, 0)),
            pl.BlockSpec((tm, D_PLE), lambda i, j: (i, 0)),
            pl.BlockSpec((1, d), lambda i, j: (0, 0)),
            pl.BlockSpec((1, d), lambda i, j: (0, 0)),
            pl.BlockSpec((d, tf), lambda i, j: (0, j)),
            pl.BlockSpec((d, tf), lambda i, j: (0, nf + j)),
            pl.BlockSpec((tf, d), lambda i, j: (j, 0)),
            pl.BlockSpec((D_PLE, d), lambda i, j: (0, 0)),
            pl.BlockSpec((d, d), lambda i, j: (0, 0)),
        ],
        out_specs=pl.BlockSpec((tm, d), lambda i, j: (i, 0)),
        out_shape=jax.ShapeDtypeStruct((n, d), F32),
        scratch_shapes=[pltpu.VMEM((tm, d), BF16), pltpu.VMEM((tm, d), F32)],
        compiler_params=_params(("parallel", "arbitrary")),
        name="ffn",
    )(h, p, g2, g3, w_in, w_in, w_out, w_proj, w_gate)


def _mla_weights(w_a, q_norm, kv_norm, w_uq, w_ukv):
    d = w_a.shape[0]
    waq = w_a[:, :MLA_Q_RANK].astype(BF16)
    wakv = w_a[:, MLA_Q_RANK:MLA_Q_RANK + MLA_KV_RANK].astype(BF16)
    war = jnp.zeros((d, LANES), F32).at[:, MLA_NOPE:MLA_NOPE + MLA_ROPE].set(
        w_a[:, MLA_Q_RANK + MLA_KV_RANK:]).astype(BF16)
    uq = w_uq.reshape(MLA_Q_RANK, HEADS, MLA_NOPE + MLA_ROPE)
    wuq = jnp.pad(uq, ((0, 0), (0, 0), (0, LANES - MLA_NOPE - MLA_ROPE)))
    wuq = wuq.reshape(MLA_Q_RANK, HEADS * LANES).astype(BF16)
    ukv = w_ukv.reshape(MLA_KV_RANK, HEADS, MLA_NOPE + HEAD_DIM)
    wuk = jnp.pad(ukv[:, :, :MLA_NOPE], ((0, 0), (0, 0), (0, LANES - MLA_NOPE)))
    wuk = wuk.reshape(MLA_KV_RANK, HEADS * LANES).astype(BF16)
    wuvt = ukv[:, :, MLA_NOPE:].reshape(MLA_KV_RANK, HEADS * HEAD_DIM).T.astype(BF16)
    inv = ROPE_THETA ** (-jnp.arange(ROPE_HALF, dtype=F32) / ROPE_HALF)
    inv_row = jnp.zeros((1, LANES), F32)
    inv_row = inv_row.at[0, MLA_NOPE:MLA_NOPE + ROPE_HALF].set(inv)
    inv_row = inv_row.at[0, MLA_NOPE + ROPE_HALF:MLA_NOPE + MLA_ROPE].set(inv)
    return dict(waq=waq, wakv=wakv, war=war, qn=q_norm.reshape(1, -1), kvn=kv_norm.reshape(1, -1),
                wuq=wuq, wuk=wuk, wuvt=wuvt, inv=inv_row)


def _q_scale_row(n_cols, q_starts, width, scale):
    row = np.ones((1, n_cols), np.float32)
    for s in q_starts:
        row[0, s:s + width] = scale
    return jnp.asarray(row)


def kernel(x, p, positions, norm_g, ffn_w_in, ffn_w_out, ple_w_proj, ple_w_gate, rel_bias, mla_w_a, mla_q_norm, mla_kv_norm, mla_w_uq, mla_w_ukv, mla_w_o, dil_w_qkv, dil_w_o, fox_w_qkvf, fox_b_f, fox_w_o):
    batch, seq, d = x.shape
    n = batch * seq
    inner = HEADS * HEAD_DIM
    h = x.reshape(n, d)
    pos = positions.reshape(n, 1)
    for i in range(N_LAYERS):
        mixer, j = i % N_MIXERS, i // N_MIXERS
        g = norm_g[i].reshape(4, 1, d)
        if mixer == 0:
            w = _mla_weights(mla_w_a[j], mla_q_norm[j], mla_kv_norm[j], mla_w_uq[j], mla_w_ukv[j])
            q, k, vt = _mla_proj(h, pos, g[0], w, 512, batch, seq)
            o = _flash(q.reshape(batch, seq, -1), k.reshape(batch, seq, -1), vt, batch, seq,
                       shared_lanes=False, q_off=0, k_off=0)
            h = _oproj(o.reshape(n, inner), mla_w_o[j].astype(BF16), h, g[1], tm=512)
        elif mixer == 1:
            n_cols = dil_w_qkv.shape[-1]
            scale = _q_scale_row(n_cols, [gi * 3 * inner for gi in range(len(DIL_PATTERNS))], inner,
                                 HEAD_DIM ** -0.5 * LOG2E)
            wq = dil_w_qkv[j]
            wvt = jnp.stack([wq[:, (3 * gi + 2) * inner:(3 * gi + 3) * inner].T
                             for gi in range(len(DIL_PATTERNS))]).astype(BF16)
            wq = wq.astype(BF16)
            table = rel_bias.reshape(REL_BUCKETS, len(DIL_PATTERNS), HEADS)
            h3 = h.reshape(batch, seq, d)
            outs, lses = [], []
            for gi in range(len(DIL_PATTERNS)):
                qk, vt = _dil_proj(h3, g[0], wq, wvt, scale, gi)
                o, lse = _dil_attention(qk, vt, table[:, gi], gi, batch, seq)
                outs.append(o)
                lses.append(lse)
            h = _oproj_merge(outs, lses, dil_w_o[j].astype(BF16), h, g[1], 512, batch, seq)
        else:
            wq = fox_w_qkvf[j]
            w = dict(
                wqk=wq[:, :2 * inner].astype(BF16),
                scale=_q_scale_row(2 * inner, [0], inner, HEAD_DIM ** -0.5 * LOG2E),
                wvt=wq[:, 2 * inner:3 * inner].T.astype(BF16),
                wf=jnp.pad(wq[:, 3 * inner:], ((0, 0), (0, LANES - HEADS))).astype(BF16),
                bf=jnp.pad(fox_b_f[j], (0, LANES - HEADS)).reshape(1, LANES),
            )
            a, vt, logf = _fox_proj(h, g[0], w, 512, batch, seq)
            c, ct = _cumsum(logf, batch, seq)
            a3 = a.reshape(batch, seq, 2 * inner)
            o = _flash(a3, a3, vt, batch, seq, shared_lanes=True, q_off=0, k_off=PAIRS, forget=(c, ct))
            h = _oproj(o.reshape(n, inner), fox_w_o[j].astype(BF16), h, g[1], tm=512)
        h = _ffn(h, p[i].reshape(n, D_PLE), g[2], g[3], ffn_w_in[i].astype(BF16),
                 ffn_w_out[i].astype(BF16), ple_w_proj[i].astype(BF16), ple_w_gate[i].astype(BF16),
                 tm=512, tf=1408)
    return h.reshape(batch, seq, d)
```

```python
import functools

import numpy as np
import jax
import jax.numpy as jnp
from jax import lax
from jax.experimental import pallas as pl
from jax.experimental.pallas import tpu as pltpu

F32 = jnp.float32
BF16 = jnp.bfloat16

D_MODEL = 1024
N_LAYERS = 4
N_MIXERS = 3
D_PLE = 256
EPS = 1e-6
NEG = -1e30
D_FF = 2816

HEADS = 16
HEAD_DIM = 64
LANES = 128
PAIRS = HEADS // 2
ONES_ROWS = 16
LOG2E = 1.4426950408889634

MLA_Q_RANK = 384
MLA_KV_RANK = 256
MLA_NOPE = 64
MLA_ROPE = 32
ROPE_HALF = MLA_ROPE // 2
ROPE_THETA = 10000.0

DIL_PATTERNS = ((128, 1), (512, 4), (2048, 16))
DIL_BLOCK = 128
REL_BUCKETS = 32
REL_MAX_DIST = 2048

VMEM_LIMIT = 56 * 1024 * 1024


def _params(sem):
    return pltpu.CompilerParams(dimension_semantics=sem, vmem_limit_bytes=VMEM_LIMIT)


def _rms(x, g):
    y = x * lax.rsqrt(jnp.mean(x * x, axis=-1, keepdims=True) + EPS)
    return y * g


def _dot(a, b):
    return jnp.dot(a, b, preferred_element_type=F32)


def _dot_nt(a, b):
    return lax.dot_general(a, b, (((1,), (1,)), ((), ())), preferred_element_type=F32)


def _const_spec(shape):
    nd = len(shape)
    return pl.BlockSpec(shape, lambda *_: (0,) * nd)


ROW_CHUNK = 256


def _dil_proj_body(h_ref, g_ref, w_ref, wvt_ref, scale_ref, qk_ref, vt_ref, slab_ref, perm_ref,
                   *, dilation, seq):
    c = pl.program_id(1)
    n_slab = D_MODEL // LANES

    @pl.when(c == 0)
    def _():
        part = slab_ref.shape[1]
        for h0 in range(0, seq, part):
            def norm_rows(i, _):
                rows = pl.multiple_of(i * ROW_CHUNK, ROW_CHUNK)
                xn = _rms(h_ref[0, pl.ds(h0 + rows, ROW_CHUNK), :], g_ref[...])
                if dilation == 1:
                    perm_ref[pl.ds(h0 + rows, ROW_CHUNK), :] = xn.astype(BF16)
                else:
                    for s in range(n_slab):
                        slab_ref[s, pl.ds(rows, ROW_CHUNK), :] = xn[:, s * LANES:(s + 1) * LANES]
                return 0

            lax.fori_loop(0, part // ROW_CHUNK, norm_rows, 0)
            if dilation > 1:
                sub, cnt = seq // dilation, part // dilation
                for r in range(dilation):
                    dst = r * sub + h0 // dilation
                    for s in range(n_slab):
                        perm_ref[dst:dst + cnt, s * LANES:(s + 1) * LANES] = (
                            slab_ref[s, pl.ds(r, cnt, stride=dilation), :].astype(BF16))

    @pl.when(c < 2)
    def _():
        qk_ref[0] = (_dot(perm_ref[...], w_ref[...]) * scale_ref[...]).astype(BF16)

    @pl.when(c == 2)
    def _():
        vt_ref[0] = _dot_nt(wvt_ref[0], perm_ref[...]).astype(BF16)


def _dil_proj(h3, g, w, wvt, scale, group):
    batch, seq, d = h3.shape
    dilation = DIL_PATTERNS[group][1]
    width = HEADS * HEAD_DIM
    col = lambda b, c: (0, group * 3 + jnp.minimum(c, 1))
    return pl.pallas_call(
        functools.partial(_dil_proj_body, dilation=dilation, seq=seq),
        grid=(batch, 3),
        in_specs=[
            pl.BlockSpec((1, seq, d), lambda b, c: (b, 0, 0)),
            pl.BlockSpec((1, d), lambda b, c: (0, 0)),
            pl.BlockSpec((d, width), col),
            pl.BlockSpec((1, width, d), lambda b, c: (group, 0, 0)),
            pl.BlockSpec((1, width), col),
        ],
        out_specs=[
            pl.BlockSpec((1, seq, width), lambda b, c: (b, 0, jnp.minimum(c, 1))),
            pl.BlockSpec((1, width, seq), lambda b, c: (b, 0, 0)),
        ],
        out_shape=[jax.ShapeDtypeStruct((batch, seq, 2 * width), BF16),
                   jax.ShapeDtypeStruct((batch, width, seq), BF16)],
        scratch_shapes=[pltpu.VMEM((d // LANES, seq // 2, LANES), F32), pltpu.VMEM((seq, d), BF16)],
        compiler_params=_params(("arbitrary", "arbitrary")),
        name=f"dil_proj_g{group}",
    )(h3, g, w, wvt, scale)


def _mla_proj_body(h_ref, pos_ref, g_ref, waq_ref, wakv_ref, war_ref, qn_ref, kvn_ref,
                   wuq_ref, wuk_ref, wuvt_ref, inv_ref, q_ref, k_ref, vt_ref):
    xn = _rms(h_ref[...], g_ref[...]).astype(BF16)
    cq = _rms(_dot(xn, waq_ref[...]), qn_ref[...]).astype(BF16)
    ckv = _rms(_dot(xn, wakv_ref[...]), kvn_ref[...]).astype(BF16)
    kr = _dot(xn, war_ref[...])
    q = _dot(cq, wuq_ref[...])
    kn = _dot(ckv, wuk_ref[...])
    vt_ref[0] = _dot_nt(wuvt_ref[...], ckv).astype(BF16)

    lane = lax.broadcasted_iota(jnp.int32, (1, LANES), 1)
    first = (lane >= MLA_NOPE) & (lane < MLA_NOPE + ROPE_HALF)
    second = (lane >= MLA_NOPE + ROPE_HALF) & (lane < MLA_NOPE + MLA_ROPE)
    ang = pos_ref[...].astype(F32) * inv_ref[...]
    cos, sin = jnp.cos(ang), jnp.sin(ang)
    c_tab = jnp.where(lane < MLA_NOPE, 1.0, jnp.where(first | second, cos, 0.0))
    s_tab = jnp.where(first, -sin, jnp.where(second, sin, 0.0))

    def rope(t):
        other = jnp.where(first, pltpu.roll(t, LANES - ROPE_HALF, 1), pltpu.roll(t, ROPE_HALF, 1))
        return t * c_tab + other * s_tab

    kr = rope(kr)
    scale = (MLA_NOPE + MLA_ROPE) ** -0.5 * LOG2E
    for hh in range(HEADS):
        sl = slice(hh * LANES, (hh + 1) * LANES)
        q_ref[:, sl] = (rope(q[:, sl]) * scale).astype(BF16)
        k_ref[:, sl] = (kn[:, sl] + kr).astype(BF16)


def _vt_spec(tm, seq, width):
    per = seq // tm
    return pl.BlockSpec((1, width, tm), lambda i: (i // per, 0, i % per))


def _mla_proj(h, pos, g, w, tm, batch, seq):
    n, d = h.shape
    row = lambda i: (i, 0)
    width = HEADS * HEAD_DIM
    return pl.pallas_call(
        _mla_proj_body,
        grid=(n // tm,),
        in_specs=[
            pl.BlockSpec((tm, d), row),
            pl.BlockSpec((tm, 1), row),
            _const_spec((1, d)),
            _const_spec(w["waq"].shape), _const_spec(w["wakv"].shape), _const_spec(w["war"].shape),
            _const_spec((1, MLA_Q_RANK)), _const_spec((1, MLA_KV_RANK)),
            _const_spec(w["wuq"].shape), _const_spec(w["wuk"].shape), _const_spec(w["wuvt"].shape),
            _const_spec((1, LANES)),
        ],
        out_specs=[
            pl.BlockSpec((tm, HEADS * LANES), row),
            pl.BlockSpec((tm, HEADS * LANES), row),
            _vt_spec(tm, seq, width),
        ],
        out_shape=[
            jax.ShapeDtypeStruct((n, HEADS * LANES), BF16),
            jax.ShapeDtypeStruct((n, HEADS * LANES), BF16),
            jax.ShapeDtypeStruct((batch, width, seq), BF16),
        ],
        compiler_params=_params(("parallel",)),
        name="mla_proj",
    )(h, pos, g, w["waq"], w["wakv"], w["war"], w["qn"], w["kvn"],
      w["wuq"], w["wuk"], w["wuvt"], w["inv"])


def _fox_proj_body(h_ref, g_ref, w_ref, scale_ref, wvt_ref, wf_ref, bf_ref, a_ref, vt_ref, logf_ref):
    xn = _rms(h_ref[...], g_ref[...]).astype(BF16)
    a_ref[...] = (_dot(xn, w_ref[...]) * scale_ref[...]).astype(BF16)
    vt_ref[0] = _dot_nt(wvt_ref[...], xn).astype(BF16)
    f = _dot(xn, wf_ref[...]) + bf_ref[...]
    logf_ref[...] = jnp.minimum(f, 0.0) - jnp.log1p(jnp.exp(-jnp.abs(f)))


def _fox_proj(h, g, w, tm, batch, seq):
    n, d = h.shape
    nout = w["wqk"].shape[1]
    width = HEADS * HEAD_DIM
    row = lambda i: (i, 0)
    return pl.pallas_call(
        _fox_proj_body,
        grid=(n // tm,),
        in_specs=[
            pl.BlockSpec((tm, d), row),
            _const_spec((1, d)),
            _const_spec((d, nout)),
            _const_spec((1, nout)),
            _const_spec((width, d)),
            _const_spec((d, LANES)),
            _const_spec((1, LANES)),
        ],
        out_specs=[pl.BlockSpec((tm, nout), row), _vt_spec(tm, seq, width),
                   pl.BlockSpec((tm, LANES), row)],
        out_shape=[jax.ShapeDtypeStruct((n, nout), BF16),
                   jax.ShapeDtypeStruct((batch, width, seq), BF16),
                   jax.ShapeDtypeStruct((n, LANES), F32)],
        compiler_params=_params(("parallel",)),
        name="fox_proj",
    )(h, g, w["wqk"], w["scale"], w["wvt"], w["wf"], w["bf"])


def _cumsum_body(x_ref, c_ref, ct_ref, *, seq, blk):
    r = lax.broadcasted_iota(jnp.int32, (blk, blk), 0)
    c = lax.broadcasted_iota(jnp.int32, (blk, blk), 1)
    tri = (c <= r).astype(F32)
    carry = jnp.zeros((1, LANES), F32)
    for b in range(seq // blk):
        xs = x_ref[0, b * blk:(b + 1) * blk, :]
        cs = lax.dot_general(tri, xs, (((1,), (0,)), ((), ())), precision=lax.Precision.HIGHEST,
                             preferred_element_type=F32) + carry
        c_ref[0, b * blk:(b + 1) * blk, :] = cs
        carry = cs[blk - 1:blk, :]
    ct_ref[0] = c_ref[0].T


def _cumsum(logf, batch, seq):
    x = logf.reshape(batch, seq, LANES)
    return pl.pallas_call(
        functools.partial(_cumsum_body, seq=seq, blk=256),
        grid=(batch,),
        in_specs=[pl.BlockSpec((1, seq, LANES), lambda b: (b, 0, 0))],
        out_specs=[pl.BlockSpec((1, seq, LANES), lambda b: (b, 0, 0)),
                   pl.BlockSpec((1, LANES, seq), lambda b: (b, 0, 0))],
        out_shape=[jax.ShapeDtypeStruct((batch, seq, LANES), F32),
                   jax.ShapeDtypeStruct((batch, LANES, seq), F32)],
        compiler_params=_params(("parallel",)),
        name="fox_cumsum",
    )(x)


def _flash_body(*refs, seq, tq, forget, shared_lanes):
    if forget:
        q_ref, k_ref, vt_ref, c_ref, ct_ref, o_ref, va_ref, ck_ref = refs
    else:
        q_ref, k_ref, vt_ref, o_ref, va_ref = refs
    pair = pl.program_id(1)
    lane = lax.broadcasted_iota(jnp.int32, (1, LANES), 1)
    low = lane < HEAD_DIM
    causal = (lax.broadcasted_iota(jnp.int32, (tq, tq), 0)
              <= lax.broadcasted_iota(jnp.int32, (tq, tq), 1))
    for e in range(2):
        va_ref[e, :HEAD_DIM, :] = vt_ref[0, e * HEAD_DIM:(e + 1) * HEAD_DIM, :]
        va_ref[e, HEAD_DIM:, :] = jnp.ones((ONES_ROWS, seq), BF16)
    if forget:
        for e in range(2):
            col = jnp.sum(jnp.where(lane == 2 * pair + e, c_ref[0], 0.0), axis=1, keepdims=True)
            ck_ref[e] = jnp.broadcast_to(col * LOG2E, (seq, tq))

    class Chain:
        pass

    def start(qi, e):
        ch = Chain()
        qs = qi * tq
        ch.qi, ch.e = qi, e
        if shared_lanes:
            qp = q_ref[0, qs:qs + tq, :]
            ch.q = jnp.where(low, qp, 0) if e == 0 else jnp.where(low, 0, qp)
            ch.ksl = slice(0, LANES)
        else:
            ch.ksl = slice(e * LANES, (e + 1) * LANES)
            ch.q = q_ref[0, qs:qs + tq, ch.ksl]
        if forget:
            ch.cq = ct_ref[0, pl.ds(2 * pair + e, 1), qs:qs + tq] * LOG2E
        ch.s, ch.m, ch.acc = [], None, None
        return ch

    def pass1(ch, c):
        ks = c * tq
        s = _dot_nt(k_ref[0, ks:ks + tq, ch.ksl], ch.q)
        if forget:
            s = s + (ch.cq - ck_ref[ch.e, ks:ks + tq, :])
        if c == ch.qi:
            s = jnp.where(causal, s, NEG)
        mc = jnp.max(s, axis=0, keepdims=True)
        ch.m = mc if ch.m is None else jnp.maximum(ch.m, mc)
        ch.s.append(s)

    def pass2(ch, c):
        ks = c * tq
        p = jnp.exp2(ch.s[c] - ch.m)
        ac = _dot(va_ref[ch.e, :, ks:ks + tq], p.astype(BF16))
        ch.acc = ac if ch.acc is None else ch.acc + ac

    outs = []

    def finish(ch):
        outs.append(ch.acc[:HEAD_DIM] / ch.acc[HEAD_DIM:HEAD_DIM + 1])
        if ch.e == 1:
            qs = ch.qi * tq
            o_t = jnp.concatenate(outs, axis=0)
            o_ref[0, qs:qs + tq, :] = o_t.T.astype(o_ref.dtype)
            outs.clear()

    prev = ()
    for qi in range(seq // tq + 1):
        cur = tuple(start(qi, e) for e in range(2)) if qi < seq // tq else ()
        for c in range(qi + 1):
            for ch in cur:
                pass1(ch, c)
            if c < qi:
                for ch in prev:
                    pass2(ch, c)
        for ch in prev:
            finish(ch)
        prev = cur


def _flash(q, k, vt, batch, seq, *, shared_lanes, q_off, k_off, forget=None, tq=256):
    qk_w = LANES if shared_lanes else 2 * LANES
    in_specs = [
        pl.BlockSpec((1, seq, qk_w), lambda b, p: (b, 0, q_off + p)),
        pl.BlockSpec((1, seq, qk_w), lambda b, p: (b, 0, k_off + p)),
        pl.BlockSpec((1, LANES, seq), lambda b, p: (b, p, 0)),
    ]
    args = [q, k, vt]
    scratch = [pltpu.VMEM((2, HEAD_DIM + ONES_ROWS, seq), BF16)]
    if forget is not None:
        c, ct = forget
        in_specs += [pl.BlockSpec((1, seq, LANES), lambda b, p: (b, 0, 0)),
                     pl.BlockSpec((1, HEADS, seq), lambda b, p: (b, 0, 0))]
        args += [c, ct]
        scratch += [pltpu.VMEM((2, seq, tq), F32)]
    return pl.pallas_call(
        functools.partial(_flash_body, seq=seq, tq=tq, forget=forget is not None,
                          shared_lanes=shared_lanes),
        grid=(batch, PAIRS),
        in_specs=in_specs,
        out_specs=pl.BlockSpec((1, seq, LANES), lambda b, p: (b, 0, p)),
        out_shape=jax.ShapeDtypeStruct((batch, seq, HEADS * HEAD_DIM), BF16),
        scratch_shapes=scratch,
        compiler_params=_params(("parallel", "parallel")),
        name="flash_fox" if forget is not None else "flash_mla",
    )(*args)


def _t5_bucket_np(dist):
    max_exact = REL_BUCKETS // 2
    n = np.maximum(dist.astype(np.float32), np.float32(1.0))
    large = max_exact + (np.log(n / np.float32(max_exact)) / np.float32(np.log(REL_MAX_DIST / max_exact))
                         * np.float32(REL_BUCKETS - max_exact)).astype(np.int32)
    large = np.minimum(large, REL_BUCKETS - 1)
    return np.where(dist < max_exact, dist, large).astype(np.int32)


def _dil_bucket_map(dilation, use_prev):
    qry = np.arange(DIL_BLOCK)
    key = np.arange(2 * DIL_BLOCK)
    rel = DIL_BLOCK + qry[None, :] - key[:, None]
    bk = _t5_bucket_np(np.clip(rel, 0, None) * dilation)
    if not use_prev:
        bk = bk[DIL_BLOCK:]
    return np.concatenate([bk, bk], axis=1)


def _dil_body(tab_ref, bucket_ref, q_ref, k_ref, vt_ref, o_ref, lse_ref, bias_ref, *, span, n_blk, seq):
    blk = DIL_BLOCK
    use_prev = n_blk > 1
    nkeys = 2 * blk if use_prev else blk

    @pl.when(pl.program_id(0) == 0)
    def _():
        bk = bucket_ref[...]
        second = lax.broadcasted_iota(jnp.int32, (1, 2 * blk), 1) >= blk

        def fill(p, _):
            acc = jnp.zeros((nkeys, 2 * blk), F32)
            for bb in range(REL_BUCKETS):
                val = jnp.where(second, tab_ref[bb, 2 * p + 1], tab_ref[bb, 2 * p])
                acc = jnp.where(bk == bb, val * LOG2E, acc)
            bias_ref[p] = acc
            return 0

        lax.fori_loop(0, PAIRS, fill, 0)

    key = lax.broadcasted_iota(jnp.int32, (nkeys, 2 * blk), 0) + (0 if use_prev else blk)
    qry = lax.broadcasted_iota(jnp.int32, (nkeys, 2 * blk), 1) & (blk - 1)
    rel = blk + qry - key
    band = (rel >= 0) & (rel <= span)
    low = lax.broadcasted_iota(jnp.int32, (1, LANES), 1) < HEAD_DIM
    head_row = lax.broadcasted_iota(jnp.int32, (LANES, blk), 0)

    def block(s, _):
        cur = pl.ds(pl.multiple_of(s * blk, blk), blk)
        if use_prev:
            n = s % n_blk
            prv = pl.ds(pl.multiple_of(jnp.where(n > 0, s - 1, s) * blk, blk), blk)
            valid = band & ((n > 0) | (key >= blk))
        else:
            valid = band
        def logits(p):
            sl = slice(p * LANES, (p + 1) * LANES)
            qp = q_ref[0, cur, sl]
            qq = jnp.concatenate([jnp.where(low, qp, 0), jnp.where(low, 0, qp)], axis=0)
            if use_prev:
                kk = jnp.concatenate([k_ref[0, prv, sl], k_ref[0, cur, sl]], axis=0)
            else:
                kk = k_ref[0, cur, sl]
            return _dot_nt(kk, qq)

        def masked(p, raw):
            st = jnp.where(valid, raw + bias_ref[p], NEG)
            return st, jnp.max(st, axis=0, keepdims=True)

        def weighted(p, st, m):
            sl = slice(p * LANES, (p + 1) * LANES)
            if use_prev:
                vv = jnp.concatenate([vt_ref[0, sl, prv], vt_ref[0, sl, cur]], axis=1)
            else:
                vv = vt_ref[0, sl, cur]
            vv = jnp.concatenate([vv, jnp.ones((ONES_ROWS, nkeys), BF16)], axis=0)
            return _dot(vv, jnp.exp2(st - m).astype(BF16))

        def finish(p, ot, m, lse_t):
            sl = slice(p * LANES, (p + 1) * LANES)
            l = ot[LANES:LANES + 1]
            ot = ot[:LANES] / l
            o_t = jnp.concatenate([ot[:HEAD_DIM, :blk], ot[HEAD_DIM:, blk:]], axis=0)
            o_ref[0, cur, sl] = o_t.T.astype(o_ref.dtype)
            lse = m * (1.0 / LOG2E) + jnp.log(l)
            return jnp.where(head_row == 2 * p, lse[:, :blk],
                             jnp.where(head_row == 2 * p + 1, lse[:, blk:], lse_t))

        raws = [logits(p) for p in range(PAIRS)]
        sms = [masked(p, raws[p]) for p in range(PAIRS)]
        ots = [weighted(p, *sms[p]) for p in range(PAIRS)]
        lse_t = jnp.zeros((LANES, blk), F32)
        for p in range(PAIRS):
            lse_t = finish(p, ots[p], sms[p][1], lse_t)
        lse_ref[0, cur, :] = lse_t.T
        return 0

    lax.fori_loop(0, seq // blk, block, 0, unroll=2)


def _dil_attention(qk, vt, table, group, batch, seq):
    window, dilation = DIL_PATTERNS[group]
    span = window // dilation
    sub = seq // dilation
    assert sub % DIL_BLOCK == 0
    n_blk = sub // DIL_BLOCK
    width = HEADS * HEAD_DIM
    bucket = jnp.asarray(_dil_bucket_map(dilation, n_blk > 1))
    return pl.pallas_call(
        functools.partial(_dil_body, span=span, n_blk=n_blk, seq=seq),
        grid=(batch,),
        in_specs=[
            pl.BlockSpec(memory_space=pltpu.SMEM),
            _const_spec(bucket.shape),
            pl.BlockSpec((1, seq, width), lambda b: (b, 0, 0)),
            pl.BlockSpec((1, seq, width), lambda b: (b, 0, 1)),
            pl.BlockSpec((1, width, seq), lambda b: (b, 0, 0)),
        ],
        out_specs=[pl.BlockSpec((1, seq, width), lambda b: (b, 0, 0)),
                   pl.BlockSpec((1, seq, LANES), lambda b: (b, 0, 0))],
        out_shape=[jax.ShapeDtypeStruct((batch, seq, width), BF16),
                   jax.ShapeDtypeStruct((batch, seq, LANES), F32)],
        scratch_shapes=[pltpu.VMEM((PAIRS,) + bucket.shape, F32)],
        compiler_params=_params(("arbitrary",)),
        name=f"dil_attn_g{group}",
    )(table, bucket, qk, qk, vt)


def _oproj_body(o_ref, w_ref, h_ref, g_ref, out_ref):
    out_ref[...] = h_ref[...] + _rms(_dot(o_ref[...], w_ref[...]), g_ref[...])


def _oproj(o, w, h, g, tm):
    n, d = h.shape
    row = lambda i: (i, 0)
    return pl.pallas_call(
        _oproj_body,
        grid=(n // tm,),
        in_specs=[pl.BlockSpec((tm, o.shape[1]), row), _const_spec(w.shape),
                  pl.BlockSpec((tm, d), row), _const_spec((1, d))],
        out_specs=pl.BlockSpec((tm, d), row),
        out_shape=jax.ShapeDtypeStruct((n, d), F32),
        compiler_params=_params(("parallel",)),
        name="oproj",
    )(o, w, h, g)


def _oproj_merge_body(o0_ref, o1_ref, o2_ref, l0_ref, l1_ref, l2_ref, w_ref, h_ref, g_ref, out_ref,
                      lse_ref, slab_ref, merged_ref):
    o_refs = (o0_ref, o1_ref, o2_ref)
    l_refs = (l0_ref, l1_ref, l2_ref)
    n_groups = len(DIL_PATTERNS)
    tm = h_ref.shape[0]
    lses = []
    for gi in range(n_groups):
        dil = DIL_PATTERNS[gi][1]
        if dil == 1:
            lses.append(l_refs[gi][0, 0])
            continue
        for r in range(dil):
            rows = pl.ds(r, tm // dil, stride=dil)
            lse_ref[gi, rows, :] = l_refs[gi][0, r]
            for s in range(PAIRS):
                slab_ref[gi, s, rows, :] = o_refs[gi][0, r, :, s * LANES:(s + 1) * LANES].astype(F32)
        lses.append(lse_ref[gi])
    mx = functools.reduce(jnp.maximum, lses)
    ex = [jnp.exp(t - mx) for t in lses]
    den = functools.reduce(jnp.add, ex)
    alpha = [t / den for t in ex]
    low = lax.broadcasted_iota(jnp.int32, (1, LANES), 1) < HEAD_DIM
    for p in range(PAIRS):
        sl = slice(p * LANES, (p + 1) * LANES)
        acc = jnp.zeros((tm, LANES), F32)
        for gi in range(n_groups):
            a = jnp.where(low, alpha[gi][:, 2 * p:2 * p + 1], alpha[gi][:, 2 * p + 1:2 * p + 2])
            if DIL_PATTERNS[gi][1] == 1:
                acc = acc + a * o_refs[gi][0, 0, :, sl].astype(F32)
            else:
                acc = acc + a * slab_ref[gi, p]
        merged_ref[:, sl] = acc.astype(BF16)
    out_ref[...] = h_ref[...] + _rms(_dot(merged_ref[...], w_ref[...]), g_ref[...])


def _oproj_merge(os_, lses, w, h, g, tm, batch, seq):
    n, d = h.shape
    row = lambda i: (i, 0)
    width = os_[0].shape[-1]
    per = seq // tm
    n_groups = len(DIL_PATTERNS)

    def grouped(x, dil):
        return x.reshape(batch, dil, seq // dil, x.shape[-1])

    def grouped_spec(dil, cols):
        return pl.BlockSpec((1, dil, tm // dil, cols), lambda i: (i // per, 0, i % per, 0))

    dils = [dil for _, dil in DIL_PATTERNS]
    return pl.pallas_call(
        _oproj_merge_body,
        grid=(n // tm,),
        in_specs=[grouped_spec(dil, width) for dil in dils] + [grouped_spec(dil, LANES) for dil in dils]
        + [_const_spec(w.shape), pl.BlockSpec((tm, d), row), _const_spec((1, d))],
        out_specs=pl.BlockSpec((tm, d), row),
        out_shape=jax.ShapeDtypeStruct((n, d), F32),
        scratch_shapes=[pltpu.VMEM((n_groups, tm, LANES), F32),
                        pltpu.VMEM((n_groups, PAIRS, tm, LANES), F32),
                        pltpu.VMEM((tm, width), BF16)],
        compiler_params=_params(("parallel",)),
        name="oproj_merge",
    )(*[grouped(o, dil) for o, dil in zip(os_, dils)],
      *[grouped(t, dil) for t, dil in zip(lses, dils)], w, h, g)


MXU_TILE = 256
FFN_SPLITS = (0, 6 * MXU_TILE, D_FF)


def _ffn_body(h_ref, p_ref, g2_ref, g3_ref, win_ref, wo_ref, wproj_ref, wgate_ref, out_ref):
    h = h_ref[...]
    xn = _rms(h, g2_ref[...]).astype(BF16)
    y = None
    for a, b in zip(FFN_SPLITS[:-1], FFN_SPLITS[1:]):
        gate = _dot(xn, win_ref[:, a:b])
        up = _dot(xn, win_ref[:, D_FF + a:D_FF + b])
        act = (gate * jax.nn.sigmoid(gate) * up).astype(BF16)
        part = _dot(act, wo_ref[a:b, :])
        y = part if y is None else y + part
    h2 = h + _rms(y, g3_ref[...])
    emb = _dot(p_ref[...].astype(BF16), wproj_ref[...])
    out_ref[...] = h2 + emb * jax.nn.sigmoid(_dot(h2.astype(BF16), wgate_ref[...]))


def _resident_spec(shape):
    nd = len(shape)
    return pl.BlockSpec(shape, lambda *_: (0,) * nd, pipeline_mode=pl.Buffered(1))


def _ffn(h, p, g2, g3, w_in, w_out, w_proj, w_gate, tm):
    n, d = h.shape
    row = lambda i: (i, 0)
    return pl.pallas_call(
        _ffn_body,
        grid=(n // tm,),
        in_specs=[
            pl.BlockSpec((tm, d), row),
            pl.BlockSpec((tm, D_PLE), row),
            _const_spec((1, d)),
            _const_spec((1, d)),
            _resident_spec(w_in.shape),
            _resident_spec(w_out.shape),
            _resident_spec(w_proj.shape),
            _resident_spec(w_gate.shape),
        ],
        out_specs=pl.BlockSpec((tm, d), row),
        out_shape=jax.ShapeDtypeStruct((n, d), F32),
        compiler_params=_params(("parallel",)),
        name="ffn",
    )(h, p, g2, g3, w_in, w_out, w_proj, w_gate)


def _mla_weights(w_a, q_norm, kv_norm, w_uq, w_ukv):
    d = w_a.shape[0]
    waq = w_a[:, :MLA_Q_RANK].astype(BF16)
    wakv = w_a[:, MLA_Q_RANK:MLA_Q_RANK + MLA_KV_RANK].astype(BF16)
    war = jnp.zeros((d, LANES), F32).at[:, MLA_NOPE:MLA_NOPE + MLA_ROPE].set(
        w_a[:, MLA_Q_RANK + MLA_KV_RANK:]).astype(BF16)
    uq = w_uq.reshape(MLA_Q_RANK, HEADS, MLA_NOPE + MLA_ROPE)
    wuq = jnp.pad(uq, ((0, 0), (0, 0), (0, LANES - MLA_NOPE - MLA_ROPE)))
    wuq = wuq.reshape(MLA_Q_RANK, HEADS * LANES).astype(BF16)
    ukv = w_ukv.reshape(MLA_KV_RANK, HEADS, MLA_NOPE + HEAD_DIM)
    wuk = jnp.pad(ukv[:, :, :MLA_NOPE], ((0, 0), (0, 0), (0, LANES - MLA_NOPE)))
    wuk = wuk.reshape(MLA_KV_RANK, HEADS * LANES).astype(BF16)
    wuvt = ukv[:, :, MLA_NOPE:].reshape(MLA_KV_RANK, HEADS * HEAD_DIM).T.astype(BF16)
    inv = ROPE_THETA ** (-jnp.arange(ROPE_HALF, dtype=F32) / ROPE_HALF)
    inv_row = jnp.zeros((1, LANES), F32)
    inv_row = inv_row.at[0, MLA_NOPE:MLA_NOPE + ROPE_HALF].set(inv)
    inv_row = inv_row.at[0, MLA_NOPE + ROPE_HALF:MLA_NOPE + MLA_ROPE].set(inv)
    return dict(waq=waq, wakv=wakv, war=war, qn=q_norm.reshape(1, -1), kvn=kv_norm.reshape(1, -1),
                wuq=wuq, wuk=wuk, wuvt=wuvt, inv=inv_row)


def _q_scale_row(n_cols, q_starts, width, scale):
    row = np.ones((1, n_cols), np.float32)
    for s in q_starts:
        row[0, s:s + width] = scale
    return jnp.asarray(row)


def kernel(x, p, positions, norm_g, ffn_w_in, ffn_w_out, ple_w_proj, ple_w_gate, rel_bias, mla_w_a, mla_q_norm, mla_kv_norm, mla_w_uq, mla_w_ukv, mla_w_o, dil_w_qkv, dil_w_o, fox_w_qkvf, fox_b_f, fox_w_o):
    batch, seq, d = x.shape
    n = batch * seq
    inner = HEADS * HEAD_DIM
    h = x.reshape(n, d)
    pos = positions.reshape(n, 1)
    for i in range(N_LAYERS):
        mixer, j = i % N_MIXERS, i // N_MIXERS
        g = norm_g[i].reshape(4, 1, d)
        if mixer == 0:
            w = _mla_weights(mla_w_a[j], mla_q_norm[j], mla_kv_norm[j], mla_w_uq[j], mla_w_ukv[j])
            q, k, vt = _mla_proj(h, pos, g[0], w, 512, batch, seq)
            o = _flash(q.reshape(batch, seq, -1), k.reshape(batch, seq, -1), vt, batch, seq,
                       shared_lanes=False, q_off=0, k_off=0)
            h = _oproj(o.reshape(n, inner), mla_w_o[j].astype(BF16), h, g[1], tm=512)
        elif mixer == 1:
            n_cols = dil_w_qkv.shape[-1]
            scale = _q_scale_row(n_cols, [gi * 3 * inner for gi in range(len(DIL_PATTERNS))], inner,
                                 HEAD_DIM ** -0.5 * LOG2E)
            wq = dil_w_qkv[j]
            wvt = jnp.stack([wq[:, (3 * gi + 2) * inner:(3 * gi + 3) * inner].T
                             for gi in range(len(DIL_PATTERNS))]).astype(BF16)
            wq = wq.astype(BF16)
            table = rel_bias.reshape(REL_BUCKETS, len(DIL_PATTERNS), HEADS)
            h3 = h.reshape(batch, seq, d)
            outs, lses = [], []
            for gi in range(len(DIL_PATTERNS)):
                qk, vt = _dil_proj(h3, g[0], wq, wvt, scale, gi)
                o, lse = _dil_attention(qk, vt, table[:, gi], gi, batch, seq)
                outs.append(o)
                lses.append(lse)
            h = _oproj_merge(outs, lses, dil_w_o[j].astype(BF16), h, g[1], 512, batch, seq)
        else:
            wq = fox_w_qkvf[j]
            w = dict(
                wqk=wq[:, :2 * inner].astype(BF16),
                scale=_q_scale_row(2 * inner, [0], inner, HEAD_DIM ** -0.5 * LOG2E),
                wvt=wq[:, 2 * inner:3 * inner].T.astype(BF16),
                wf=jnp.pad(wq[:, 3 * inner:], ((0, 0), (0, LANES - HEADS))).astype(BF16),
                bf=jnp.pad(fox_b_f[j], (0, LANES - HEADS)).reshape(1, LANES),
            )
            a, vt, logf = _fox_proj(h, g[0], w, 512, batch, seq)
            c, ct = _cumsum(logf, batch, seq)
            a3 = a.reshape(batch, seq, 2 * inner)
            o = _flash(a3, a3, vt, batch, seq, shared_lanes=True, q_off=0, k_off=PAIRS, forget=(c, ct))
            h = _oproj(o.reshape(n, inner), fox_w_o[j].astype(BF16), h, g[1], tm=512)
        h = _ffn(h, p[i].reshape(n, D_PLE), g[2], g[3], ffn_w_in[i].astype(BF16),
                 ffn_w_out[i].astype(BF16), ple_w_proj[i].astype(BF16), ple_w_gate[i].astype(BF16),
                 tm=512)
    return h.reshape(batch, seq, d)
```

```python
import functools

import numpy as np
import jax
import jax.numpy as jnp
from jax import lax
from jax.experimental import pallas as pl
from jax.experimental.pallas import tpu as pltpu

F32 = jnp.float32
BF16 = jnp.bfloat16

D_MODEL = 1024
N_LAYERS = 4
N_MIXERS = 3
D_PLE = 256
EPS = 1e-6
NEG = -1e30
D_FF = 2816

HEADS = 16
HEAD_DIM = 64
LANES = 128
PAIRS = HEADS // 2
ONES_ROWS = 16
LOG2E = 1.4426950408889634

MLA_Q_RANK = 384
MLA_KV_RANK = 256
MLA_NOPE = 64
MLA_ROPE = 32
ROPE_HALF = MLA_ROPE // 2
ROPE_THETA = 10000.0

DIL_PATTERNS = ((128, 1), (512, 4), (2048, 16))
DIL_BLOCK = 128
REL_BUCKETS = 32
REL_MAX_DIST = 2048

VMEM_LIMIT = 56 * 1024 * 1024


def _params(sem):
    return pltpu.CompilerParams(dimension_semantics=sem, vmem_limit_bytes=VMEM_LIMIT)


def _rms(x, g):
    y = x * lax.rsqrt(jnp.mean(x * x, axis=-1, keepdims=True) + EPS)
    return y * g


def _dot(a, b):
    return jnp.dot(a, b, preferred_element_type=F32)


def _dot_nt(a, b):
    return lax.dot_general(a, b, (((1,), (1,)), ((), ())), preferred_element_type=F32)


def _const_spec(shape):
    nd = len(shape)
    return pl.BlockSpec(shape, lambda *_: (0,) * nd)


ROW_CHUNK = 256


def _dil_proj_body(h_ref, g_ref, w_ref, wvt_ref, scale_ref, qk_ref, vt_ref, slab_ref, perm_ref,
                   *, dilation, seq):
    c = pl.program_id(1)
    n_slab = D_MODEL // LANES

    @pl.when(c == 0)
    def _():
        part = slab_ref.shape[1]
        for h0 in range(0, seq, part):
            def norm_rows(i, _):
                rows = pl.multiple_of(i * ROW_CHUNK, ROW_CHUNK)
                xn = _rms(h_ref[0, pl.ds(h0 + rows, ROW_CHUNK), :], g_ref[...])
                if dilation == 1:
                    perm_ref[pl.ds(h0 + rows, ROW_CHUNK), :] = xn.astype(BF16)
                else:
                    for s in range(n_slab):
                        slab_ref[s, pl.ds(rows, ROW_CHUNK), :] = xn[:, s * LANES:(s + 1) * LANES]
                return 0

            lax.fori_loop(0, part // ROW_CHUNK, norm_rows, 0)
            if dilation > 1:
                sub, cnt = seq // dilation, part // dilation
                for r in range(dilation):
                    dst = r * sub + h0 // dilation
                    for s in range(n_slab):
                        perm_ref[dst:dst + cnt, s * LANES:(s + 1) * LANES] = (
                            slab_ref[s, pl.ds(r, cnt, stride=dilation), :].astype(BF16))

    @pl.when(c < 2)
    def _():
        qk_ref[0] = (_dot(perm_ref[...], w_ref[...]) * scale_ref[...]).astype(BF16)

    @pl.when(c == 2)
    def _():
        vt_ref[0] = _dot_nt(wvt_ref[0], perm_ref[...]).astype(BF16)


def _dil_proj(h3, g, w, wvt, scale, group):
    batch, seq, d = h3.shape
    dilation = DIL_PATTERNS[group][1]
    width = HEADS * HEAD_DIM
    col = lambda b, c: (0, group * 3 + jnp.minimum(c, 1))
    return pl.pallas_call(
        functools.partial(_dil_proj_body, dilation=dilation, seq=seq),
        grid=(batch, 3),
        in_specs=[
            pl.BlockSpec((1, seq, d), lambda b, c: (b, 0, 0)),
            pl.BlockSpec((1, d), lambda b, c: (0, 0)),
            pl.BlockSpec((d, width), col),
            pl.BlockSpec((1, width, d), lambda b, c: (group, 0, 0)),
            pl.BlockSpec((1, width), col),
        ],
        out_specs=[
            pl.BlockSpec((1, seq, width), lambda b, c: (b, 0, jnp.minimum(c, 1))),
            pl.BlockSpec((1, width, seq), lambda b, c: (b, 0, 0)),
        ],
        out_shape=[jax.ShapeDtypeStruct((batch, seq, 2 * width), BF16),
                   jax.ShapeDtypeStruct((batch, width, seq), BF16)],
        scratch_shapes=[pltpu.VMEM((d // LANES, seq // 2, LANES), F32), pltpu.VMEM((seq, d), BF16)],
        compiler_params=_params(("arbitrary", "arbitrary")),
        name=f"dil_proj_g{group}",
    )(h3, g, w, wvt, scale)


def _rope_table_body(pos_ref, inv_ref, c_ref, s_ref):
    tm = pos_ref.shape[-1]
    ang = inv_ref[...] * pos_ref[0].astype(F32)
    cos, sin = jnp.cos(ang), jnp.sin(ang)
    pad = LANES - MLA_NOPE - MLA_ROPE
    c_t = jnp.concatenate([jnp.ones((MLA_NOPE, tm), F32), cos, jnp.zeros((pad, tm), F32)], axis=0)
    s_t = jnp.concatenate([jnp.zeros((MLA_NOPE, tm), F32), -sin[:ROPE_HALF], sin[ROPE_HALF:],
                           jnp.zeros((pad, tm), F32)], axis=0)
    c_ref[...] = c_t.T
    s_ref[...] = s_t.T


def _rope_tables(positions, tm):
    n = positions.size
    inv = ROPE_THETA ** (-jnp.arange(ROPE_HALF, dtype=F32) / ROPE_HALF)
    inv_col = jnp.concatenate([inv, inv]).reshape(MLA_ROPE, 1)
    return pl.pallas_call(
        _rope_table_body,
        grid=(n // tm,),
        in_specs=[pl.BlockSpec((1, 1, tm), lambda i: (i, 0, 0)), _const_spec((MLA_ROPE, 1))],
        out_specs=[pl.BlockSpec((tm, LANES), lambda i: (i, 0))] * 2,
        out_shape=[jax.ShapeDtypeStruct((n, LANES), F32)] * 2,
        compiler_params=_params(("parallel",)),
        name="rope_tables",
    )(positions.reshape(n // tm, 1, tm), inv_col)


def _mla_proj_body(h_ref, ctab_ref, stab_ref, g_ref, wa_ref, qn_ref, kvn_ref,
                   wuq_ref, wuk_ref, wuvt_ref, q_ref, k_ref, vt_ref):
    xn = _rms(h_ref[...], g_ref[...]).astype(BF16)
    a = _dot(xn, wa_ref[...])
    cq = _rms(a[:, :MLA_Q_RANK], qn_ref[...]).astype(BF16)
    ckv = _rms(a[:, MLA_Q_RANK:MLA_Q_RANK + MLA_KV_RANK], kvn_ref[...]).astype(BF16)
    kr = a[:, MLA_Q_RANK + MLA_KV_RANK:]
    q = _dot(cq, wuq_ref[...])
    kn = _dot(ckv, wuk_ref[...])
    vt_ref[0] = _dot_nt(wuvt_ref[...], ckv).astype(BF16)

    lane = lax.broadcasted_iota(jnp.int32, (1, LANES), 1)
    first = (lane >= MLA_NOPE) & (lane < MLA_NOPE + ROPE_HALF)
    c_tab, s_tab = ctab_ref[...], stab_ref[...]

    def rope(t):
        other = jnp.where(first, pltpu.roll(t, LANES - ROPE_HALF, 1), pltpu.roll(t, ROPE_HALF, 1))
        return t * c_tab + other * s_tab

    kr = rope(kr)
    scale = (MLA_NOPE + MLA_ROPE) ** -0.5 * LOG2E
    for hh in range(HEADS):
        sl = slice(hh * LANES, (hh + 1) * LANES)
        q_ref[:, sl] = (rope(q[:, sl]) * scale).astype(BF16)
        k_ref[:, sl] = (kn[:, sl] + kr).astype(BF16)


def _vt_spec(tm, seq, width):
    per = seq // tm
    return pl.BlockSpec((1, width, tm), lambda i: (i // per, 0, i % per))


def _mla_proj(h, tabs, g, w, tm, batch, seq):
    n, d = h.shape
    row = lambda i: (i, 0)
    width = HEADS * HEAD_DIM
    return pl.pallas_call(
        _mla_proj_body,
        grid=(n // tm,),
        in_specs=[
            pl.BlockSpec((tm, d), row),
            pl.BlockSpec((tm, LANES), row),
            pl.BlockSpec((tm, LANES), row),
            _const_spec((1, d)),
            _const_spec(w["wa"].shape),
            _const_spec((1, MLA_Q_RANK)), _const_spec((1, MLA_KV_RANK)),
            _const_spec(w["wuq"].shape), _const_spec(w["wuk"].shape), _const_spec(w["wuvt"].shape),
        ],
        out_specs=[
            pl.BlockSpec((tm, HEADS * LANES), row),
            pl.BlockSpec((tm, HEADS * LANES), row),
            _vt_spec(tm, seq, width),
        ],
        out_shape=[
            jax.ShapeDtypeStruct((n, HEADS * LANES), BF16),
            jax.ShapeDtypeStruct((n, HEADS * LANES), BF16),
            jax.ShapeDtypeStruct((batch, width, seq), BF16),
        ],
        compiler_params=_params(("parallel",)),
        name="mla_proj",
    )(h, tabs[0], tabs[1], g, w["wa"], w["qn"], w["kvn"],
      w["wuq"], w["wuk"], w["wuvt"])


def _fox_proj_body(h_ref, g_ref, w_ref, scale_ref, wvt_ref, wf_ref, bf_ref, a_ref, vt_ref, logf_ref):
    xn = _rms(h_ref[...], g_ref[...]).astype(BF16)
    a_ref[...] = (_dot(xn, w_ref[...]) * scale_ref[...]).astype(BF16)
    vt_ref[0] = _dot_nt(wvt_ref[...], xn).astype(BF16)
    f = _dot(xn, wf_ref[...]) + bf_ref[...]
    logf_ref[...] = jnp.minimum(f, 0.0) - jnp.log1p(jnp.exp(-jnp.abs(f)))


def _fox_proj(h, g, w, tm, batch, seq):
    n, d = h.shape
    nout = w["wqk"].shape[1]
    width = HEADS * HEAD_DIM
    row = lambda i: (i, 0)
    return pl.pallas_call(
        _fox_proj_body,
        grid=(n // tm,),
        in_specs=[
            pl.BlockSpec((tm, d), row),
            _const_spec((1, d)),
            _const_spec((d, nout)),
            _const_spec((1, nout)),
            _const_spec((width, d)),
            _const_spec((d, LANES)),
            _const_spec((1, LANES)),
        ],
        out_specs=[pl.BlockSpec((tm, nout), row), _vt_spec(tm, seq, width),
                   pl.BlockSpec((tm, LANES), row)],
        out_shape=[jax.ShapeDtypeStruct((n, nout), BF16),
                   jax.ShapeDtypeStruct((batch, width, seq), BF16),
                   jax.ShapeDtypeStruct((n, LANES), F32)],
        compiler_params=_params(("parallel",)),
        name="fox_proj",
    )(h, g, w["wqk"], w["scale"], w["wvt"], w["wf"], w["bf"])


def _cumsum_body(x_ref, c_ref, ct_ref, *, seq, blk):
    r = lax.broadcasted_iota(jnp.int32, (blk, blk), 0)
    c = lax.broadcasted_iota(jnp.int32, (blk, blk), 1)
    tri = (c <= r).astype(F32)
    carry = jnp.zeros((1, LANES), F32)
    for b in range(seq // blk):
        xs = x_ref[0, b * blk:(b + 1) * blk, :]
        cs = lax.dot_general(tri, xs, (((1,), (0,)), ((), ())), precision=lax.Precision.HIGHEST,
                             preferred_element_type=F32) + carry
        c_ref[0, b * blk:(b + 1) * blk, :] = cs
        carry = cs[blk - 1:blk, :]
    ct_ref[0] = c_ref[0].T


def _cumsum(logf, batch, seq):
    x = logf.reshape(batch, seq, LANES)
    return pl.pallas_call(
        functools.partial(_cumsum_body, seq=seq, blk=256),
        grid=(batch,),
        in_specs=[pl.BlockSpec((1, seq, LANES), lambda b: (b, 0, 0))],
        out_specs=[pl.BlockSpec((1, seq, LANES), lambda b: (b, 0, 0)),
                   pl.BlockSpec((1, LANES, seq), lambda b: (b, 0, 0))],
        out_shape=[jax.ShapeDtypeStruct((batch, seq, LANES), F32),
                   jax.ShapeDtypeStruct((batch, LANES, seq), F32)],
        compiler_params=_params(("parallel",)),
        name="fox_cumsum",
    )(x)


def _flash_body(*refs, seq, tq, forget, shared_lanes):
    if forget:
        q_ref, k_ref, vt_ref, c_ref, ct_ref, o_ref, va_ref, ck_ref = refs
    else:
        q_ref, k_ref, vt_ref, o_ref, va_ref = refs
    pair = pl.program_id(1)
    lane = lax.broadcasted_iota(jnp.int32, (1, LANES), 1)
    low = lane < HEAD_DIM
    causal = (lax.broadcasted_iota(jnp.int32, (tq, tq), 0)
              <= lax.broadcasted_iota(jnp.int32, (tq, tq), 1))
    for e in range(2):
        va_ref[e, :HEAD_DIM, :] = vt_ref[0, e * HEAD_DIM:(e + 1) * HEAD_DIM, :]
        va_ref[e, HEAD_DIM:, :] = jnp.ones((ONES_ROWS, seq), BF16)
    if forget:
        for e in range(2):
            col = jnp.sum(jnp.where(lane == 2 * pair + e, c_ref[0], 0.0), axis=1, keepdims=True)
            ck_ref[e] = jnp.broadcast_to(col * LOG2E, (seq, tq))

    class Chain:
        pass

    def start(qi, e):
        ch = Chain()
        qs = qi * tq
        ch.qi, ch.e = qi, e
        if shared_lanes:
            qp = q_ref[0, qs:qs + tq, :]
            ch.q = jnp.where(low, qp, 0) if e == 0 else jnp.where(low, 0, qp)
            ch.ksl = slice(0, LANES)
        else:
            ch.ksl = slice(e * LANES, (e + 1) * LANES)
            ch.q = q_ref[0, qs:qs + tq, ch.ksl]
        if forget:
            ch.cq = ct_ref[0, pl.ds(2 * pair + e, 1), qs:qs + tq] * LOG2E
        ch.s, ch.m, ch.acc = [], None, None
        return ch

    def pass1(ch, c):
        ks = c * tq
        s = _dot_nt(k_ref[0, ks:ks + tq, ch.ksl], ch.q)
        if forget:
            s = s + (ch.cq - ck_ref[ch.e, ks:ks + tq, :])
        if c == ch.qi:
            s = jnp.where(causal, s, NEG)
        mc = jnp.max(s, axis=0, keepdims=True)
        ch.m = mc if ch.m is None else jnp.maximum(ch.m, mc)
        ch.s.append(s)

    def pass2(ch, c):
        ks = c * tq
        p = jnp.exp2(ch.s[c] - ch.m)
        ac = _dot(va_ref[ch.e, :, ks:ks + tq], p.astype(BF16))
        ch.acc = ac if ch.acc is None else ch.acc + ac

    outs = []

    def finish(ch):
        outs.append(ch.acc[:HEAD_DIM] / ch.acc[HEAD_DIM:HEAD_DIM + 1])
        if ch.e == 1:
            qs = ch.qi * tq
            o_t = jnp.concatenate(outs, axis=0)
            o_ref[0, qs:qs + tq, :] = o_t.T.astype(o_ref.dtype)
            outs.clear()

    prev = ()
    for qi in range(seq // tq + 1):
        cur = tuple(start(qi, e) for e in range(2)) if qi < seq // tq else ()
        for c in range(qi + 1):
            for ch in cur:
                pass1(ch, c)
            if c < qi:
                for ch in prev:
                    pass2(ch, c)
        for ch in prev:
            finish(ch)
        prev = cur


def _flash(q, k, vt, batch, seq, *, shared_lanes, q_off, k_off, forget=None, tq=256):
    qk_w = LANES if shared_lanes else 2 * LANES
    in_specs = [
        pl.BlockSpec((1, seq, qk_w), lambda b, p: (b, 0, q_off + p)),
        pl.BlockSpec((1, seq, qk_w), lambda b, p: (b, 0, k_off + p)),
        pl.BlockSpec((1, LANES, seq), lambda b, p: (b, p, 0)),
    ]
    args = [q, k, vt]
    scratch = [pltpu.VMEM((2, HEAD_DIM + ONES_ROWS, seq), BF16)]
    if forget is not None:
        c, ct = forget
        in_specs += [pl.BlockSpec((1, seq, LANES), lambda b, p: (b, 0, 0)),
                     pl.BlockSpec((1, HEADS, seq), lambda b, p: (b, 0, 0))]
        args += [c, ct]
        scratch += [pltpu.VMEM((2, seq, tq), F32)]
    return pl.pallas_call(
        functools.partial(_flash_body, seq=seq, tq=tq, forget=forget is not None,
                          shared_lanes=shared_lanes),
        grid=(batch, PAIRS),
        in_specs=in_specs,
        out_specs=pl.BlockSpec((1, seq, LANES), lambda b, p: (b, 0, p)),
        out_shape=jax.ShapeDtypeStruct((batch, seq, HEADS * HEAD_DIM), BF16),
        scratch_shapes=scratch,
        compiler_params=_params(("parallel", "parallel")),
        name="flash_fox" if forget is not None else "flash_mla",
    )(*args)


def _t5_bucket_np(dist):
    max_exact = REL_BUCKETS // 2
    n = np.maximum(dist.astype(np.float32), np.float32(1.0))
    large = max_exact + (np.log(n / np.float32(max_exact)) / np.float32(np.log(REL_MAX_DIST / max_exact))
                         * np.float32(REL_BUCKETS - max_exact)).astype(np.int32)
    large = np.minimum(large, REL_BUCKETS - 1)
    return np.where(dist < max_exact, dist, large).astype(np.int32)


def _dil_bucket_map(dilation, use_prev):
    qry = np.arange(DIL_BLOCK)
    key = np.arange(2 * DIL_BLOCK)
    rel = DIL_BLOCK + qry[None, :] - key[:, None]
    bk = _t5_bucket_np(np.clip(rel, 0, None) * dilation)
    if not use_prev:
        bk = bk[DIL_BLOCK:]
    return np.concatenate([bk, bk], axis=1)


def _dil_body(tab_ref, bucket_ref, q_ref, k_ref, vt_ref, o_ref, lse_ref, bias_ref, *, span, n_blk, seq):
    blk = DIL_BLOCK
    use_prev = n_blk > 1
    nkeys = 2 * blk if use_prev else blk

    @pl.when(pl.program_id(0) == 0)
    def _():
        bk = bucket_ref[...]
        second = lax.broadcasted_iota(jnp.int32, (1, 2 * blk), 1) >= blk

        def fill(p, _):
            acc = jnp.zeros((nkeys, 2 * blk), F32)
            for bb in range(REL_BUCKETS):
                val = jnp.where(second, tab_ref[bb, 2 * p + 1], tab_ref[bb, 2 * p])
                acc = jnp.where(bk == bb, val * LOG2E, acc)
            bias_ref[p] = acc
            return 0

        lax.fori_loop(0, PAIRS, fill, 0)

    key = lax.broadcasted_iota(jnp.int32, (nkeys, 2 * blk), 0) + (0 if use_prev else blk)
    qry = lax.broadcasted_iota(jnp.int32, (nkeys, 2 * blk), 1) & (blk - 1)
    rel = blk + qry - key
    band = (rel >= 0) & (rel <= span)
    low = lax.broadcasted_iota(jnp.int32, (1, LANES), 1) < HEAD_DIM
    head_row = lax.broadcasted_iota(jnp.int32, (LANES, blk), 0)

    def block(s, _):
        cur = pl.ds(pl.multiple_of(s * blk, blk), blk)
        if use_prev:
            n = s % n_blk
            prv = pl.ds(pl.multiple_of(jnp.where(n > 0, s - 1, s) * blk, blk), blk)
            valid = band & ((n > 0) | (key >= blk))
        else:
            valid = band
        def logits(p):
            sl = slice(p * LANES, (p + 1) * LANES)
            qp = q_ref[0, cur, sl]
            qq = jnp.concatenate([jnp.where(low, qp, 0), jnp.where(low, 0, qp)], axis=0)
            if use_prev:
                kk = jnp.concatenate([k_ref[0, prv, sl], k_ref[0, cur, sl]], axis=0)
            else:
                kk = k_ref[0, cur, sl]
            return _dot_nt(kk, qq)

        def masked(p, raw):
            st = jnp.where(valid, raw + bias_ref[p], NEG)
            return st, jnp.max(st, axis=0, keepdims=True)

        def weighted(p, st, m):
            sl = slice(p * LANES, (p + 1) * LANES)
            if use_prev:
                vv = jnp.concatenate([vt_ref[0, sl, prv], vt_ref[0, sl, cur]], axis=1)
            else:
                vv = vt_ref[0, sl, cur]
            vv = jnp.concatenate([vv, jnp.ones((ONES_ROWS, nkeys), BF16)], axis=0)
            return _dot(vv, jnp.exp2(st - m).astype(BF16))

        def finish(p, ot, m, lse_t):
            sl = slice(p * LANES, (p + 1) * LANES)
            l = ot[LANES:LANES + 1]
            ot = ot[:LANES] / l
            o_t = jnp.concatenate([ot[:HEAD_DIM, :blk], ot[HEAD_DIM:, blk:]], axis=0)
            o_ref[0, cur, sl] = o_t.T.astype(o_ref.dtype)
            lse = m * (1.0 / LOG2E) + jnp.log(l)
            return jnp.where(head_row == 2 * p, lse[:, :blk],
                             jnp.where(head_row == 2 * p + 1, lse[:, blk:], lse_t))

        raws = [logits(p) for p in range(PAIRS)]
        sms = [masked(p, raws[p]) for p in range(PAIRS)]
        ots = [weighted(p, *sms[p]) for p in range(PAIRS)]
        lse_t = jnp.zeros((LANES, blk), F32)
        for p in range(PAIRS):
            lse_t = finish(p, ots[p], sms[p][1], lse_t)
        lse_ref[0, cur, :] = lse_t.T
        return 0

    lax.fori_loop(0, seq // blk, block, 0, unroll=2)


def _dil_attention(qk, vt, table, group, batch, seq):
    window, dilation = DIL_PATTERNS[group]
    span = window // dilation
    sub = seq // dilation
    assert sub % DIL_BLOCK == 0
    n_blk = sub // DIL_BLOCK
    width = HEADS * HEAD_DIM
    bucket = jnp.asarray(_dil_bucket_map(dilation, n_blk > 1))
    return pl.pallas_call(
        functools.partial(_dil_body, span=span, n_blk=n_blk, seq=seq),
        grid=(batch,),
        in_specs=[
            pl.BlockSpec(memory_space=pltpu.SMEM),
            _const_spec(bucket.shape),
            pl.BlockSpec((1, seq, width), lambda b: (b, 0, 0)),
            pl.BlockSpec((1, seq, width), lambda b: (b, 0, 1)),
            pl.BlockSpec((1, width, seq), lambda b: (b, 0, 0)),
        ],
        out_specs=[pl.BlockSpec((1, seq, width), lambda b: (b, 0, 0)),
                   pl.BlockSpec((1, seq, LANES), lambda b: (b, 0, 0))],
        out_shape=[jax.ShapeDtypeStruct((batch, seq, width), BF16),
                   jax.ShapeDtypeStruct((batch, seq, LANES), F32)],
        scratch_shapes=[pltpu.VMEM((PAIRS,) + bucket.shape, F32)],
        compiler_params=_params(("arbitrary",)),
        name=f"dil_attn_g{group}",
    )(table, bucket, qk, qk, vt)


def _oproj_merge_body(o0_ref, o1_ref, o2_ref, l0_ref, l1_ref, l2_ref, w_ref, h_ref, g_ref, out_ref,
                      lse_ref, slab_ref, merged_ref):
    o_refs = (o0_ref, o1_ref, o2_ref)
    l_refs = (l0_ref, l1_ref, l2_ref)
    n_groups = len(DIL_PATTERNS)
    tm = h_ref.shape[0]
    lses = []
    for gi in range(n_groups):
        dil = DIL_PATTERNS[gi][1]
        if dil == 1:
            lses.append(l_refs[gi][0, 0])
            continue
        for r in range(dil):
            rows = pl.ds(r, tm // dil, stride=dil)
            lse_ref[gi, rows, :] = l_refs[gi][0, r]
            for s in range(PAIRS):
                slab_ref[gi, s, rows, :] = o_refs[gi][0, r, :, s * LANES:(s + 1) * LANES].astype(F32)
        lses.append(lse_ref[gi])
    mx = functools.reduce(jnp.maximum, lses)
    ex = [jnp.exp(t - mx) for t in lses]
    den = functools.reduce(jnp.add, ex)
    alpha = [t / den for t in ex]
    low = lax.broadcasted_iota(jnp.int32, (1, LANES), 1) < HEAD_DIM
    for p in range(PAIRS):
        sl = slice(p * LANES, (p + 1) * LANES)
        acc = jnp.zeros((tm, LANES), F32)
        for gi in range(n_groups):
            a = jnp.where(low, alpha[gi][:, 2 * p:2 * p + 1], alpha[gi][:, 2 * p + 1:2 * p + 2])
            if DIL_PATTERNS[gi][1] == 1:
                acc = acc + a * o_refs[gi][0, 0, :, sl].astype(F32)
            else:
                acc = acc + a * slab_ref[gi, p]
        merged_ref[:, sl] = acc.astype(BF16)
    out_ref[...] = h_ref[...] + _rms(_dot(merged_ref[...], w_ref[...]), g_ref[...])


def _oproj_merge(os_, lses, w, h, g, tm, batch, seq):
    n, d = h.shape
    row = lambda i: (i, 0)
    width = os_[0].shape[-1]
    per = seq // tm
    n_groups = len(DIL_PATTERNS)

    def grouped(x, dil):
        return x.reshape(batch, dil, seq // dil, x.shape[-1])

    def grouped_spec(dil, cols):
        return pl.BlockSpec((1, dil, tm // dil, cols), lambda i: (i // per, 0, i % per, 0))

    dils = [dil for _, dil in DIL_PATTERNS]
    return pl.pallas_call(
        _oproj_merge_body,
        grid=(n // tm,),
        in_specs=[grouped_spec(dil, width) for dil in dils] + [grouped_spec(dil, LANES) for dil in dils]
        + [_const_spec(w.shape), pl.BlockSpec((tm, d), row), _const_spec((1, d))],
        out_specs=pl.BlockSpec((tm, d), row),
        out_shape=jax.ShapeDtypeStruct((n, d), F32),
        scratch_shapes=[pltpu.VMEM((n_groups, tm, LANES), F32),
                        pltpu.VMEM((n_groups, PAIRS, tm, LANES), F32),
                        pltpu.VMEM((tm, width), BF16)],
        compiler_params=_params(("parallel",)),
        name="oproj_merge",
    )(*[grouped(o, dil) for o, dil in zip(os_, dils)],
      *[grouped(t, dil) for t, dil in zip(lses, dils)], w, h, g)


MXU_TILE = 256
FFN_SPLITS = (0, 6 * MXU_TILE, D_FF)


def _ffn_body(*refs, with_oproj):
    if with_oproj:
        o_ref, wattn_ref, g1_ref, *refs = refs
    h_ref, p_ref, g2_ref, g3_ref, win_ref, wo_ref, wproj_ref, wgate_ref, out_ref = refs
    h = h_ref[...]
    if with_oproj:
        h = h + _rms(_dot(o_ref[...], wattn_ref[...]), g1_ref[...])
    xn = _rms(h, g2_ref[...]).astype(BF16)
    y = None
    for a, b in zip(FFN_SPLITS[:-1], FFN_SPLITS[1:]):
        gate = _dot(xn, win_ref[:, a:b])
        up = _dot(xn, win_ref[:, D_FF + a:D_FF + b])
        act = (gate * jax.nn.sigmoid(gate) * up).astype(BF16)
        part = _dot(act, wo_ref[a:b, :])
        y = part if y is None else y + part
    h2 = h + _rms(y, g3_ref[...])
    emb = _dot(p_ref[...].astype(BF16), wproj_ref[...])
    out_ref[...] = h2 + emb * jax.nn.sigmoid(_dot(h2.astype(BF16), wgate_ref[...]))


def _resident_spec(shape):
    nd = len(shape)
    return pl.BlockSpec(shape, lambda *_: (0,) * nd, pipeline_mode=pl.Buffered(1))


def _ffn(h, p, g2, g3, w_in, w_out, w_proj, w_gate, tm, oproj=None):
    n, d = h.shape
    row = lambda i: (i, 0)
    head_specs, head_args = [], []
    if oproj is not None:
        o, w_attn, g1 = oproj
        head_specs = [pl.BlockSpec((tm, o.shape[1]), row), _resident_spec(w_attn.shape), _const_spec((1, d))]
        head_args = [o, w_attn, g1]
    return pl.pallas_call(
        functools.partial(_ffn_body, with_oproj=oproj is not None),
        grid=(n // tm,),
        in_specs=head_specs + [
            pl.BlockSpec((tm, d), row),
            pl.BlockSpec((tm, D_PLE), row),
            _const_spec((1, d)),
            _const_spec((1, d)),
            _resident_spec(w_in.shape),
            _resident_spec(w_out.shape),
            _resident_spec(w_proj.shape),
            _resident_spec(w_gate.shape),
        ],
        out_specs=pl.BlockSpec((tm, d), row),
        out_shape=jax.ShapeDtypeStruct((n, d), F32),
        compiler_params=_params(("parallel",)),
        name="ffn",
    )(*head_args, h, p, g2, g3, w_in, w_out, w_proj, w_gate)


def _mla_weights(w_a, q_norm, kv_norm, w_uq, w_ukv):
    rank = MLA_Q_RANK + MLA_KV_RANK
    wa = jnp.pad(w_a, ((0, 0), (0, LANES - MLA_ROPE)))
    wa = jnp.concatenate([wa[:, :rank], jnp.roll(wa[:, rank:], MLA_NOPE, axis=1)], axis=1).astype(BF16)
    uq = w_uq.reshape(MLA_Q_RANK, HEADS, MLA_NOPE + MLA_ROPE)
    wuq = jnp.pad(uq, ((0, 0), (0, 0), (0, LANES - MLA_NOPE - MLA_ROPE)))
    wuq = wuq.reshape(MLA_Q_RANK, HEADS * LANES).astype(BF16)
    ukv = w_ukv.reshape(MLA_KV_RANK, HEADS, MLA_NOPE + HEAD_DIM)
    wuk = jnp.pad(ukv[:, :, :MLA_NOPE], ((0, 0), (0, 0), (0, LANES - MLA_NOPE)))
    wuk = wuk.reshape(MLA_KV_RANK, HEADS * LANES).astype(BF16)
    wuvt = ukv[:, :, MLA_NOPE:].reshape(MLA_KV_RANK, HEADS * HEAD_DIM).T.astype(BF16)
    return dict(wa=wa, qn=q_norm.reshape(1, -1), kvn=kv_norm.reshape(1, -1),
                wuq=wuq, wuk=wuk, wuvt=wuvt)


def _q_scale_row(n_cols, q_starts, width, scale):
    row = np.ones((1, n_cols), np.float32)
    for s in q_starts:
        row[0, s:s + width] = scale
    return jnp.asarray(row)


def kernel(x, p, positions, norm_g, ffn_w_in, ffn_w_out, ple_w_proj, ple_w_gate, rel_bias, mla_w_a, mla_q_norm, mla_kv_norm, mla_w_uq, mla_w_ukv, mla_w_o, dil_w_qkv, dil_w_o, fox_w_qkvf, fox_b_f, fox_w_o):
    batch, seq, d = x.shape
    n = batch * seq
    inner = HEADS * HEAD_DIM
    h = x.reshape(n, d)
    rope_tabs = _rope_tables(positions, 512)
    for i in range(N_LAYERS):
        mixer, j = i % N_MIXERS, i // N_MIXERS
        g = norm_g[i].reshape(4, 1, d)
        if mixer == 0:
            w = _mla_weights(mla_w_a[j], mla_q_norm[j], mla_kv_norm[j], mla_w_uq[j], mla_w_ukv[j])
            q, k, vt = _mla_proj(h, rope_tabs, g[0], w, 512, batch, seq)
            o = _flash(q.reshape(batch, seq, -1), k.reshape(batch, seq, -1), vt, batch, seq,
                       shared_lanes=False, q_off=0, k_off=0, tq=512)
            oproj = (o.reshape(n, inner), mla_w_o[j].astype(BF16), g[1])
        elif mixer == 1:
            n_cols = dil_w_qkv.shape[-1]
            scale = _q_scale_row(n_cols, [gi * 3 * inner for gi in range(len(DIL_PATTERNS))], inner,
                                 HEAD_DIM ** -0.5 * LOG2E)
            wq = dil_w_qkv[j]
            wvt = jnp.stack([wq[:, (3 * gi + 2) * inner:(3 * gi + 3) * inner].T
                             for gi in range(len(DIL_PATTERNS))]).astype(BF16)
            wq = wq.astype(BF16)
            table = rel_bias.reshape(REL_BUCKETS, len(DIL_PATTERNS), HEADS)
            h3 = h.reshape(batch, seq, d)
            outs, lses = [], []
            for gi in range(len(DIL_PATTERNS)):
                qk, vt = _dil_proj(h3, g[0], wq, wvt, scale, gi)
                o, lse = _dil_attention(qk, vt, table[:, gi], gi, batch, seq)
                outs.append(o)
                lses.append(lse)
            h = _oproj_merge(outs, lses, dil_w_o[j].astype(BF16), h, g[1], 512, batch, seq)
            oproj = None
        else:
            wq = fox_w_qkvf[j]
            w = dict(
                wqk=wq[:, :2 * inner].astype(BF16),
                scale=_q_scale_row(2 * inner, [0], inner, HEAD_DIM ** -0.5 * LOG2E),
                wvt=wq[:, 2 * inner:3 * inner].T.astype(BF16),
                wf=jnp.pad(wq[:, 3 * inner:], ((0, 0), (0, LANES - HEADS))).astype(BF16),
                bf=jnp.pad(fox_b_f[j], (0, LANES - HEADS)).reshape(1, LANES),
            )
            a, vt, logf = _fox_proj(h, g[0], w, 512, batch, seq)
            c, ct = _cumsum(logf, batch, seq)
            a3 = a.reshape(batch, seq, 2 * inner)
            o = _flash(a3, a3, vt, batch, seq, shared_lanes=True, q_off=0, k_off=PAIRS, forget=(c, ct))
            oproj = (o.reshape(n, inner), fox_w_o[j].astype(BF16), g[1])
        h = _ffn(h, p[i].reshape(n, D_PLE), g[2], g[3], ffn_w_in[i].astype(BF16),
                 ffn_w_out[i].astype(BF16), ple_w_proj[i].astype(BF16), ple_w_gate[i].astype(BF16),
                 tm=512, oproj=oproj)
    return h.reshape(batch, seq, d)
```

```python
import functools

import numpy as np
import jax
import jax.numpy as jnp
from jax import lax
from jax.experimental import pallas as pl
from jax.experimental.pallas import tpu as pltpu

F32 = jnp.float32
BF16 = jnp.bfloat16

D_MODEL = 1024
N_LAYERS = 4
N_MIXERS = 3
D_PLE = 256
EPS = 1e-6
NEG = -1e30
D_FF = 2816

HEADS = 16
HEAD_DIM = 64
LANES = 128
PAIRS = HEADS // 2
ONES_ROWS = 16
LOG2E = 1.4426950408889634

MLA_Q_RANK = 384
MLA_KV_RANK = 256
MLA_NOPE = 64
MLA_ROPE = 32
ROPE_HALF = MLA_ROPE // 2
ROPE_THETA = 10000.0

DIL_PATTERNS = ((128, 1), (512, 4), (2048, 16))
DIL_BLOCK = 128
REL_BUCKETS = 32
REL_MAX_DIST = 2048

VMEM_LIMIT = 56 * 1024 * 1024


def _params(sem):
    return pltpu.CompilerParams(dimension_semantics=sem, vmem_limit_bytes=VMEM_LIMIT)


def _rms(x, g):
    y = x * lax.rsqrt(jnp.mean(x * x, axis=-1, keepdims=True) + EPS)
    return y * g


def _dot(a, b):
    return jnp.dot(a, b, preferred_element_type=F32)


def _dot_nt(a, b):
    return lax.dot_general(a, b, (((1,), (1,)), ((), ())), preferred_element_type=F32)


def _const_spec(shape):
    nd = len(shape)
    return pl.BlockSpec(shape, lambda *_: (0,) * nd)


ROW_CHUNK = 256


def _dil_proj_body(h_ref, g_ref, w_ref, wvt_ref, scale_ref, qk_ref, vt_ref, slab_ref, perm_ref,
                   *, dilation, seq):
    c = pl.program_id(1)
    n_slab = D_MODEL // LANES

    @pl.when(c == 0)
    def _():
        part = slab_ref.shape[1]
        for h0 in range(0, seq, part):
            def norm_rows(i, _):
                rows = pl.multiple_of(i * ROW_CHUNK, ROW_CHUNK)
                xn = _rms(h_ref[0, pl.ds(h0 + rows, ROW_CHUNK), :], g_ref[...])
                if dilation == 1:
                    perm_ref[pl.ds(h0 + rows, ROW_CHUNK), :] = xn.astype(BF16)
                else:
                    for s in range(n_slab):
                        slab_ref[s, pl.ds(rows, ROW_CHUNK), :] = xn[:, s * LANES:(s + 1) * LANES]
                return 0

            lax.fori_loop(0, part // ROW_CHUNK, norm_rows, 0)
            if dilation > 1:
                sub, cnt = seq // dilation, part // dilation
                for r in range(dilation):
                    dst = r * sub + h0 // dilation
                    for s in range(n_slab):
                        perm_ref[dst:dst + cnt, s * LANES:(s + 1) * LANES] = (
                            slab_ref[s, pl.ds(r, cnt, stride=dilation), :].astype(BF16))

    @pl.when(c < 2)
    def _():
        qk_ref[0] = (_dot(perm_ref[...], w_ref[...]) * scale_ref[...]).astype(BF16)

    @pl.when(c == 2)
    def _():
        vt_ref[0] = _dot_nt(wvt_ref[0], perm_ref[...]).astype(BF16)


def _dil_proj(h3, g, w, wvt, scale, group):
    batch, seq, d = h3.shape
    dilation = DIL_PATTERNS[group][1]
    width = HEADS * HEAD_DIM
    col = lambda b, c: (0, group * 3 + jnp.minimum(c, 1))
    return pl.pallas_call(
        functools.partial(_dil_proj_body, dilation=dilation, seq=seq),
        grid=(batch, 3),
        in_specs=[
            pl.BlockSpec((1, seq, d), lambda b, c: (b, 0, 0)),
            pl.BlockSpec((1, d), lambda b, c: (0, 0)),
            pl.BlockSpec((d, width), col),
            pl.BlockSpec((1, width, d), lambda b, c: (group, 0, 0)),
            pl.BlockSpec((1, width), col),
        ],
        out_specs=[
            pl.BlockSpec((1, seq, width), lambda b, c: (b, 0, jnp.minimum(c, 1))),
            pl.BlockSpec((1, width, seq), lambda b, c: (b, 0, 0)),
        ],
        out_shape=[jax.ShapeDtypeStruct((batch, seq, 2 * width), BF16),
                   jax.ShapeDtypeStruct((batch, width, seq), BF16)],
        scratch_shapes=[pltpu.VMEM((d // LANES, seq // 2, LANES), F32), pltpu.VMEM((seq, d), BF16)],
        compiler_params=_params(("arbitrary", "arbitrary")),
        name=f"dil_proj_g{group}",
    )(h3, g, w, wvt, scale)


def _rope_table_body(pos_ref, inv_ref, c_ref, s_ref):
    tm = pos_ref.shape[-1]
    ang = inv_ref[...] * pos_ref[0].astype(F32)
    cos, sin = jnp.cos(ang), jnp.sin(ang)
    pad = LANES - MLA_NOPE - MLA_ROPE
    c_t = jnp.concatenate([jnp.ones((MLA_NOPE, tm), F32), cos, jnp.zeros((pad, tm), F32)], axis=0)
    s_t = jnp.concatenate([jnp.zeros((MLA_NOPE, tm), F32), -sin[:ROPE_HALF], sin[ROPE_HALF:],
                           jnp.zeros((pad, tm), F32)], axis=0)
    c_ref[...] = c_t.T
    s_ref[...] = s_t.T


def _rope_tables(positions, tm):
    n = positions.size
    inv = ROPE_THETA ** (-jnp.arange(ROPE_HALF, dtype=F32) / ROPE_HALF)
    inv_col = jnp.concatenate([inv, inv]).reshape(MLA_ROPE, 1)
    return pl.pallas_call(
        _rope_table_body,
        grid=(n // tm,),
        in_specs=[pl.BlockSpec((1, 1, tm), lambda i: (i, 0, 0)), _const_spec((MLA_ROPE, 1))],
        out_specs=[pl.BlockSpec((tm, LANES), lambda i: (i, 0))] * 2,
        out_shape=[jax.ShapeDtypeStruct((n, LANES), F32)] * 2,
        compiler_params=_params(("parallel",)),
        name="rope_tables",
    )(positions.reshape(n // tm, 1, tm), inv_col)


def _mla_proj_body(h_ref, ctab_ref, stab_ref, g_ref, wa_ref, qn_ref, kvn_ref,
                   wuq_ref, wuk_ref, wuvt_ref, q_ref, k_ref, vt_ref):
    xn = _rms(h_ref[...], g_ref[...]).astype(BF16)
    a = _dot(xn, wa_ref[...])
    cq = _rms(a[:, :MLA_Q_RANK], qn_ref[...]).astype(BF16)
    ckv = _rms(a[:, MLA_Q_RANK:MLA_Q_RANK + MLA_KV_RANK], kvn_ref[...]).astype(BF16)
    kr = a[:, MLA_Q_RANK + MLA_KV_RANK:]
    q = _dot(cq, wuq_ref[...])
    kn = _dot(ckv, wuk_ref[...])
    vt_ref[0] = _dot_nt(wuvt_ref[...], ckv).astype(BF16)

    lane = lax.broadcasted_iota(jnp.int32, (1, LANES), 1)
    first = (lane >= MLA_NOPE) & (lane < MLA_NOPE + ROPE_HALF)
    c_tab, s_tab = ctab_ref[...], stab_ref[...]

    def rope(t):
        other = jnp.where(first, pltpu.roll(t, LANES - ROPE_HALF, 1), pltpu.roll(t, ROPE_HALF, 1))
        return t * c_tab + other * s_tab

    kr = rope(kr)
    scale = (MLA_NOPE + MLA_ROPE) ** -0.5 * LOG2E
    for hh in range(HEADS):
        sl = slice(hh * LANES, (hh + 1) * LANES)
        q_ref[:, sl] = (rope(q[:, sl]) * scale).astype(BF16)
        k_ref[:, sl] = (kn[:, sl] + kr).astype(BF16)


def _vt_spec(tm, seq, width):
    per = seq // tm
    return pl.BlockSpec((1, width, tm), lambda i: (i // per, 0, i % per))


def _mla_proj(h, tabs, g, w, tm, batch, seq):
    n, d = h.shape
    row = lambda i: (i, 0)
    width = HEADS * HEAD_DIM
    return pl.pallas_call(
        _mla_proj_body,
        grid=(n // tm,),
        in_specs=[
            pl.BlockSpec((tm, d), row),
            pl.BlockSpec((tm, LANES), row),
            pl.BlockSpec((tm, LANES), row),
            _const_spec((1, d)),
            _const_spec(w["wa"].shape),
            _const_spec((1, MLA_Q_RANK)), _const_spec((1, MLA_KV_RANK)),
            _const_spec(w["wuq"].shape), _const_spec(w["wuk"].shape), _const_spec(w["wuvt"].shape),
        ],
        out_specs=[
            pl.BlockSpec((tm, HEADS * LANES), row),
            pl.BlockSpec((tm, HEADS * LANES), row),
            _vt_spec(tm, seq, width),
        ],
        out_shape=[
            jax.ShapeDtypeStruct((n, HEADS * LANES), BF16),
            jax.ShapeDtypeStruct((n, HEADS * LANES), BF16),
            jax.ShapeDtypeStruct((batch, width, seq), BF16),
        ],
        compiler_params=_params(("parallel",)),
        name="mla_proj",
    )(h, tabs[0], tabs[1], g, w["wa"], w["qn"], w["kvn"],
      w["wuq"], w["wuk"], w["wuvt"])


def _fox_proj_body(h_ref, g_ref, w_ref, scale_ref, wvt_ref, wf_ref, bf_ref, a_ref, vt_ref, logf_ref):
    xn = _rms(h_ref[...], g_ref[...]).astype(BF16)
    a_ref[...] = (_dot(xn, w_ref[...]) * scale_ref[...]).astype(BF16)
    vt_ref[0] = _dot_nt(wvt_ref[...], xn).astype(BF16)
    f = _dot(xn, wf_ref[...]) + bf_ref[...]
    logf_ref[...] = jnp.minimum(f, 0.0) - jnp.log1p(jnp.exp(-jnp.abs(f)))


def _fox_proj(h, g, w, tm, batch, seq):
    n, d = h.shape
    nout = w["wqk"].shape[1]
    width = HEADS * HEAD_DIM
    row = lambda i: (i, 0)
    return pl.pallas_call(
        _fox_proj_body,
        grid=(n // tm,),
        in_specs=[
            pl.BlockSpec((tm, d), row),
            _const_spec((1, d)),
            _const_spec((d, nout)),
            _const_spec((1, nout)),
            _const_spec((width, d)),
            _const_spec((d, LANES)),
            _const_spec((1, LANES)),
        ],
        out_specs=[pl.BlockSpec((tm, nout), row), _vt_spec(tm, seq, width),
                   pl.BlockSpec((tm, LANES), row)],
        out_shape=[jax.ShapeDtypeStruct((n, nout), BF16),
                   jax.ShapeDtypeStruct((batch, width, seq), BF16),
                   jax.ShapeDtypeStruct((n, LANES), F32)],
        compiler_params=_params(("parallel",)),
        name="fox_proj",
    )(h, g, w["wqk"], w["scale"], w["wvt"], w["wf"], w["bf"])


def _cumsum_body(x_ref, c_ref, ct_ref, *, seq, blk):
    r = lax.broadcasted_iota(jnp.int32, (blk, blk), 0)
    c = lax.broadcasted_iota(jnp.int32, (blk, blk), 1)
    tri = (c <= r).astype(F32)
    carry = jnp.zeros((1, LANES), F32)
    for b in range(seq // blk):
        xs = x_ref[0, b * blk:(b + 1) * blk, :]
        cs = lax.dot_general(tri, xs, (((1,), (0,)), ((), ())), precision=lax.Precision.HIGHEST,
                             preferred_element_type=F32) + carry
        c_ref[0, b * blk:(b + 1) * blk, :] = cs
        carry = cs[blk - 1:blk, :]
    ct_ref[0] = c_ref[0].T


def _cumsum(logf, batch, seq):
    x = logf.reshape(batch, seq, LANES)
    return pl.pallas_call(
        functools.partial(_cumsum_body, seq=seq, blk=256),
        grid=(batch,),
        in_specs=[pl.BlockSpec((1, seq, LANES), lambda b: (b, 0, 0))],
        out_specs=[pl.BlockSpec((1, seq, LANES), lambda b: (b, 0, 0)),
                   pl.BlockSpec((1, LANES, seq), lambda b: (b, 0, 0))],
        out_shape=[jax.ShapeDtypeStruct((batch, seq, LANES), F32),
                   jax.ShapeDtypeStruct((batch, LANES, seq), F32)],
        compiler_params=_params(("parallel",)),
        name="fox_cumsum",
    )(x)


def _flash_body(*refs, seq, tq, forget, shared_lanes):
    if forget:
        q_ref, k_ref, vt_ref, c_ref, ct_ref, o_ref, va_ref, ck_ref = refs
    else:
        q_ref, k_ref, vt_ref, o_ref, va_ref = refs
    pair = pl.program_id(1)
    lane = lax.broadcasted_iota(jnp.int32, (1, LANES), 1)
    low = lane < HEAD_DIM
    causal = (lax.broadcasted_iota(jnp.int32, (tq, tq), 0)
              <= lax.broadcasted_iota(jnp.int32, (tq, tq), 1))
    for e in range(2):
        va_ref[e, :HEAD_DIM, :] = vt_ref[0, e * HEAD_DIM:(e + 1) * HEAD_DIM, :]
        va_ref[e, HEAD_DIM:, :] = jnp.ones((ONES_ROWS, seq), BF16)
    if forget:
        for e in range(2):
            col = jnp.sum(jnp.where(lane == 2 * pair + e, c_ref[0], 0.0), axis=1, keepdims=True)
            ck_ref[e] = jnp.broadcast_to(col * LOG2E, (seq, tq))

    class Chain:
        pass

    def start(qi, e):
        ch = Chain()
        qs = qi * tq
        ch.qi, ch.e = qi, e
        if shared_lanes:
            qp = q_ref[0, qs:qs + tq, :]
            ch.q = jnp.where(low, qp, 0) if e == 0 else jnp.where(low, 0, qp)
            ch.ksl = slice(0, LANES)
        else:
            ch.ksl = slice(e * LANES, (e + 1) * LANES)
            ch.q = q_ref[0, qs:qs + tq, ch.ksl]
        if forget:
            ch.cq = ct_ref[0, pl.ds(2 * pair + e, 1), qs:qs + tq] * LOG2E
        ch.s, ch.m, ch.acc = [], None, None
        return ch

    def pass1(ch, c):
        ks = c * tq
        s = _dot_nt(k_ref[0, ks:ks + tq, ch.ksl], ch.q)
        if forget:
            s = s + (ch.cq - ck_ref[ch.e, ks:ks + tq, :])
        if c == ch.qi:
            s = jnp.where(causal, s, NEG)
        mc = jnp.max(s, axis=0, keepdims=True)
        ch.m = mc if ch.m is None else jnp.maximum(ch.m, mc)
        ch.s.append(s)

    def pass2(ch, c):
        ks = c * tq
        p = jnp.exp2(ch.s[c] - ch.m)
        ac = _dot(va_ref[ch.e, :, ks:ks + tq], p.astype(BF16))
        ch.acc = ac if ch.acc is None else ch.acc + ac

    outs = []

    def finish(ch):
        outs.append(ch.acc[:HEAD_DIM] / ch.acc[HEAD_DIM:HEAD_DIM + 1])
        if ch.e == 1:
            qs = ch.qi * tq
            o_t = jnp.concatenate(outs, axis=0)
            o_ref[0, qs:qs + tq, :] = o_t.T.astype(o_ref.dtype)
            outs.clear()

    prev = ()
    for qi in range(seq // tq + 1):
        cur = tuple(start(qi, e) for e in range(2)) if qi < seq // tq else ()
        for c in range(qi + 1):
            for ch in cur:
                pass1(ch, c)
            if c < qi:
                for ch in prev:
                    pass2(ch, c)
        for ch in prev:
            finish(ch)
        prev = cur


def _flash(q, k, vt, batch, seq, *, shared_lanes, q_off, k_off, forget=None, tq=256):
    qk_w = LANES if shared_lanes else 2 * LANES
    in_specs = [
        pl.BlockSpec((1, seq, qk_w), lambda b, p: (b, 0, q_off + p)),
        pl.BlockSpec((1, seq, qk_w), lambda b, p: (b, 0, k_off + p)),
        pl.BlockSpec((1, LANES, seq), lambda b, p: (b, p, 0)),
    ]
    args = [q, k, vt]
    scratch = [pltpu.VMEM((2, HEAD_DIM + ONES_ROWS, seq), BF16)]
    if forget is not None:
        c, ct = forget
        in_specs += [pl.BlockSpec((1, seq, LANES), lambda b, p: (b, 0, 0)),
                     pl.BlockSpec((1, HEADS, seq), lambda b, p: (b, 0, 0))]
        args += [c, ct]
        scratch += [pltpu.VMEM((2, seq, tq), F32)]
    return pl.pallas_call(
        functools.partial(_flash_body, seq=seq, tq=tq, forget=forget is not None,
                          shared_lanes=shared_lanes),
        grid=(batch, PAIRS),
        in_specs=in_specs,
        out_specs=pl.BlockSpec((1, seq, LANES), lambda b, p: (b, 0, p)),
        out_shape=jax.ShapeDtypeStruct((batch, seq, HEADS * HEAD_DIM), BF16),
        scratch_shapes=scratch,
        compiler_params=_params(("parallel", "parallel")),
        name="flash_fox" if forget is not None else "flash_mla",
    )(*args)


def _t5_bucket_np(dist):
    max_exact = REL_BUCKETS // 2
    n = np.maximum(dist.astype(np.float32), np.float32(1.0))
    large = max_exact + (np.log(n / np.float32(max_exact)) / np.float32(np.log(REL_MAX_DIST / max_exact))
                         * np.float32(REL_BUCKETS - max_exact)).astype(np.int32)
    large = np.minimum(large, REL_BUCKETS - 1)
    return np.where(dist < max_exact, dist, large).astype(np.int32)


def _dil_bucket_map(dilation, use_prev):
    qry = np.arange(DIL_BLOCK)
    key = np.arange(2 * DIL_BLOCK)
    rel = DIL_BLOCK + qry[None, :] - key[:, None]
    bk = _t5_bucket_np(np.clip(rel, 0, None) * dilation)
    if not use_prev:
        bk = bk[DIL_BLOCK:]
    return np.concatenate([bk, bk], axis=1)


def _dil_body(tab_ref, bucket_ref, q_ref, k_ref, vt_ref, o_ref, lse_ref, bias_ref, *, span, n_blk, seq):
    blk = DIL_BLOCK
    use_prev = n_blk > 1
    nkeys = 2 * blk if use_prev else blk

    @pl.when(pl.program_id(0) == 0)
    def _():
        bk = bucket_ref[...]
        second = lax.broadcasted_iota(jnp.int32, (1, 2 * blk), 1) >= blk
        key = lax.broadcasted_iota(jnp.int32, (nkeys, 2 * blk), 0) + (0 if use_prev else blk)
        qry = lax.broadcasted_iota(jnp.int32, (nkeys, 2 * blk), 1) & (blk - 1)
        rel = blk + qry - key
        band = (rel >= 0) & (rel <= span)

        def fill(p, _):
            acc = jnp.zeros((nkeys, 2 * blk), F32)
            for bb in range(REL_BUCKETS):
                val = jnp.where(second, tab_ref[bb, 2 * p + 1], tab_ref[bb, 2 * p])
                acc = jnp.where(bk == bb, val * LOG2E, acc)
            bias_ref[0, p] = jnp.where(band & (key >= blk), acc, NEG)
            bias_ref[1, p] = jnp.where(band, acc, NEG)
            return 0

        lax.fori_loop(0, PAIRS, fill, 0)

    low = lax.broadcasted_iota(jnp.int32, (1, LANES), 1) < HEAD_DIM
    head_row = lax.broadcasted_iota(jnp.int32, (LANES, blk), 0)

    def block(s, _):
        cur = pl.ds(pl.multiple_of(s * blk, blk), blk)
        if use_prev:
            n = s % n_blk
            prv = pl.ds(pl.multiple_of(jnp.where(n > 0, s - 1, s) * blk, blk), blk)
            table = jnp.where(n > 0, 1, 0)
        else:
            table = 1
        def logits(p):
            sl = slice(p * LANES, (p + 1) * LANES)
            qp = q_ref[0, cur, sl]
            qq = jnp.concatenate([jnp.where(low, qp, 0), jnp.where(low, 0, qp)], axis=0)
            if use_prev:
                kk = jnp.concatenate([k_ref[0, prv, sl], k_ref[0, cur, sl]], axis=0)
            else:
                kk = k_ref[0, cur, sl]
            return _dot_nt(kk, qq)

        def masked(p, raw):
            st = raw + bias_ref[table, p]
            return st, jnp.max(st, axis=0, keepdims=True)

        def weighted(p, st, m):
            sl = slice(p * LANES, (p + 1) * LANES)
            if use_prev:
                vv = jnp.concatenate([vt_ref[0, sl, prv], vt_ref[0, sl, cur]], axis=1)
            else:
                vv = vt_ref[0, sl, cur]
            vv = jnp.concatenate([vv, jnp.ones((ONES_ROWS, nkeys), BF16)], axis=0)
            return _dot(vv, jnp.exp2(st - m).astype(BF16))

        def finish(p, ot, m, lse_t):
            sl = slice(p * LANES, (p + 1) * LANES)
            l = ot[LANES:LANES + 1]
            ot = ot[:LANES] / l
            o_t = jnp.concatenate([ot[:HEAD_DIM, :blk], ot[HEAD_DIM:, blk:]], axis=0)
            o_ref[0, cur, sl] = o_t.T.astype(o_ref.dtype)
            lse = m * (1.0 / LOG2E) + jnp.log(l)
            return jnp.where(head_row == 2 * p, lse[:, :blk],
                             jnp.where(head_row == 2 * p + 1, lse[:, blk:], lse_t))

        raws = [logits(p) for p in range(PAIRS)]
        sms = [masked(p, raws[p]) for p in range(PAIRS)]
        ots = [weighted(p, *sms[p]) for p in range(PAIRS)]
        lse_t = jnp.zeros((LANES, blk), F32)
        for p in range(PAIRS):
            lse_t = finish(p, ots[p], sms[p][1], lse_t)
        lse_ref[0, cur, :] = lse_t.T
        return 0

    lax.fori_loop(0, seq // blk, block, 0, unroll=2)


def _dil_attention(qk, vt, table, group, batch, seq):
    window, dilation = DIL_PATTERNS[group]
    span = window // dilation
    sub = seq // dilation
    assert sub % DIL_BLOCK == 0
    n_blk = sub // DIL_BLOCK
    width = HEADS * HEAD_DIM
    bucket = jnp.asarray(_dil_bucket_map(dilation, n_blk > 1))
    return pl.pallas_call(
        functools.partial(_dil_body, span=span, n_blk=n_blk, seq=seq),
        grid=(batch,),
        in_specs=[
            pl.BlockSpec(memory_space=pltpu.SMEM),
            _const_spec(bucket.shape),
            pl.BlockSpec((1, seq, width), lambda b: (b, 0, 0)),
            pl.BlockSpec((1, seq, width), lambda b: (b, 0, 1)),
            pl.BlockSpec((1, width, seq), lambda b: (b, 0, 0)),
        ],
        out_specs=[pl.BlockSpec((1, seq, width), lambda b: (b, 0, 0)),
                   pl.BlockSpec((1, seq, LANES), lambda b: (b, 0, 0))],
        out_shape=[jax.ShapeDtypeStruct((batch, seq, width), BF16),
                   jax.ShapeDtypeStruct((batch, seq, LANES), F32)],
        scratch_shapes=[pltpu.VMEM((2, PAIRS) + bucket.shape, F32)],
        compiler_params=_params(("arbitrary",)),
        name=f"dil_attn_g{group}",
    )(table, bucket, qk, qk, vt)


def _oproj_merge_body(o0_ref, o1_ref, o2_ref, l0_ref, l1_ref, l2_ref, w_ref, h_ref, g_ref, out_ref,
                      lse_ref, slab_ref, merged_ref):
    o_refs = (o0_ref, o1_ref, o2_ref)
    l_refs = (l0_ref, l1_ref, l2_ref)
    n_groups = len(DIL_PATTERNS)
    tm = h_ref.shape[0]
    lses = []
    for gi in range(n_groups):
        dil = DIL_PATTERNS[gi][1]
        if dil == 1:
            lses.append(l_refs[gi][0, 0])
            continue
        for r in range(dil):
            rows = pl.ds(r, tm // dil, stride=dil)
            lse_ref[gi, rows, :] = l_refs[gi][0, r]
            for s in range(PAIRS):
                slab_ref[gi, s, rows, :] = o_refs[gi][0, r, :, s * LANES:(s + 1) * LANES].astype(F32)
        lses.append(lse_ref[gi])
    mx = functools.reduce(jnp.maximum, lses)
    ex = [jnp.exp(t - mx) for t in lses]
    den = functools.reduce(jnp.add, ex)
    alpha = [t / den for t in ex]
    low = lax.broadcasted_iota(jnp.int32, (1, LANES), 1) < HEAD_DIM
    for p in range(PAIRS):
        sl = slice(p * LANES, (p + 1) * LANES)
        acc = jnp.zeros((tm, LANES), F32)
        for gi in range(n_groups):
            a = jnp.where(low, alpha[gi][:, 2 * p:2 * p + 1], alpha[gi][:, 2 * p + 1:2 * p + 2])
            if DIL_PATTERNS[gi][1] == 1:
                acc = acc + a * o_refs[gi][0, 0, :, sl].astype(F32)
            else:
                acc = acc + a * slab_ref[gi, p]
        merged_ref[:, sl] = acc.astype(BF16)
    out_ref[...] = h_ref[...] + _rms(_dot(merged_ref[...], w_ref[...]), g_ref[...])


def _oproj_merge(os_, lses, w, h, g, tm, batch, seq):
    n, d = h.shape
    row = lambda i: (i, 0)
    width = os_[0].shape[-1]
    per = seq // tm
    n_groups = len(DIL_PATTERNS)

    def grouped(x, dil):
        return x.reshape(batch, dil, seq // dil, x.shape[-1])

    def grouped_spec(dil, cols):
        return pl.BlockSpec((1, dil, tm // dil, cols), lambda i: (i // per, 0, i % per, 0))

    dils = [dil for _, dil in DIL_PATTERNS]
    return pl.pallas_call(
        _oproj_merge_body,
        grid=(n // tm,),
        in_specs=[grouped_spec(dil, width) for dil in dils] + [grouped_spec(dil, LANES) for dil in dils]
        + [_const_spec(w.shape), pl.BlockSpec((tm, d), row), _const_spec((1, d))],
        out_specs=pl.BlockSpec((tm, d), row),
        out_shape=jax.ShapeDtypeStruct((n, d), F32),
        scratch_shapes=[pltpu.VMEM((n_groups, tm, LANES), F32),
                        pltpu.VMEM((n_groups, PAIRS, tm, LANES), F32),
                        pltpu.VMEM((tm, width), BF16)],
        compiler_params=_params(("parallel",)),
        name="oproj_merge",
    )(*[grouped(o, dil) for o, dil in zip(os_, dils)],
      *[grouped(t, dil) for t, dil in zip(lses, dils)], w, h, g)


MXU_TILE = 256
FFN_SPLITS = (0, 6 * MXU_TILE, D_FF)


def _ffn_body(*refs, with_oproj):
    if with_oproj:
        o_ref, wattn_ref, g1_ref, *refs = refs
    h_ref, p_ref, g2_ref, g3_ref, win_ref, wo_ref, wproj_ref, wgate_ref, out_ref = refs
    h = h_ref[...]
    if with_oproj:
        h = h + _rms(_dot(o_ref[...], wattn_ref[0]), g1_ref[...])
    xn = _rms(h, g2_ref[...]).astype(BF16)
    y = None
    for a, b in zip(FFN_SPLITS[:-1], FFN_SPLITS[1:]):
        gate = _dot(xn, win_ref[0, :, a:b])
        up = _dot(xn, win_ref[0, :, D_FF + a:D_FF + b])
        act = (gate * jax.nn.sigmoid(gate) * up).astype(BF16)
        part = _dot(act, wo_ref[0, a:b, :])
        y = part if y is None else y + part
    h2 = h + _rms(y, g3_ref[...])
    emb = _dot(p_ref[0].astype(BF16), wproj_ref[0])
    out_ref[...] = h2 + emb * jax.nn.sigmoid(_dot(h2.astype(BF16), wgate_ref[0]))


def _layer_spec(stacked, layer):
    nd = stacked.ndim - 1
    return pl.BlockSpec((1,) + stacked.shape[1:], lambda *_: (layer,) + (0,) * nd,
                        pipeline_mode=pl.Buffered(1))


def _ffn(h, p, layer, g2, g3, w_in, w_out, w_proj, w_gate, tm, oproj=None):
    n, d = h.shape
    row = lambda i: (i, 0)
    head_specs, head_args = [], []
    if oproj is not None:
        o, w_attn, attn_layer, g1 = oproj
        head_specs = [pl.BlockSpec((tm, o.shape[1]), row), _layer_spec(w_attn, attn_layer),
                      _const_spec((1, d))]
        head_args = [o, w_attn, g1]
    return pl.pallas_call(
        functools.partial(_ffn_body, with_oproj=oproj is not None),
        grid=(n // tm,),
        in_specs=head_specs + [
            pl.BlockSpec((tm, d), row),
            pl.BlockSpec((1, tm, D_PLE), lambda i: (layer, i, 0)),
            _const_spec((1, d)),
            _const_spec((1, d)),
            _layer_spec(w_in, layer),
            _layer_spec(w_out, layer),
            _layer_spec(w_proj, layer),
            _layer_spec(w_gate, layer),
        ],
        out_specs=pl.BlockSpec((tm, d), row),
        out_shape=jax.ShapeDtypeStruct((n, d), F32),
        compiler_params=_params(("parallel",)),
        name="ffn",
    )(*head_args, h, p, g2, g3, w_in, w_out, w_proj, w_gate)


def _mla_weights(w_a, q_norm, kv_norm, w_uq, w_ukv):
    rank = MLA_Q_RANK + MLA_KV_RANK
    wa = jnp.pad(w_a, ((0, 0), (0, LANES - MLA_ROPE)))
    wa = jnp.concatenate([wa[:, :rank], jnp.roll(wa[:, rank:], MLA_NOPE, axis=1)], axis=1).astype(BF16)
    uq = w_uq.reshape(MLA_Q_RANK, HEADS, MLA_NOPE + MLA_ROPE)
    wuq = jnp.pad(uq, ((0, 0), (0, 0), (0, LANES - MLA_NOPE - MLA_ROPE)))
    wuq = wuq.reshape(MLA_Q_RANK, HEADS * LANES).astype(BF16)
    ukv = w_ukv.reshape(MLA_KV_RANK, HEADS, MLA_NOPE + HEAD_DIM)
    wuk = jnp.pad(ukv[:, :, :MLA_NOPE], ((0, 0), (0, 0), (0, LANES - MLA_NOPE)))
    wuk = wuk.reshape(MLA_KV_RANK, HEADS * LANES).astype(BF16)
    wuvt = ukv[:, :, MLA_NOPE:].reshape(MLA_KV_RANK, HEADS * HEAD_DIM).T.astype(BF16)
    return dict(wa=wa, qn=q_norm.reshape(1, -1), kvn=kv_norm.reshape(1, -1),
                wuq=wuq, wuk=wuk, wuvt=wuvt)


def _q_scale_row(n_cols, q_starts, width, scale):
    row = np.ones((1, n_cols), np.float32)
    for s in q_starts:
        row[0, s:s + width] = scale
    return jnp.asarray(row)


def kernel(x, p, positions, norm_g, ffn_w_in, ffn_w_out, ple_w_proj, ple_w_gate, rel_bias, mla_w_a, mla_q_norm, mla_kv_norm, mla_w_uq, mla_w_ukv, mla_w_o, dil_w_qkv, dil_w_o, fox_w_qkvf, fox_b_f, fox_w_o):
    batch, seq, d = x.shape
    n = batch * seq
    inner = HEADS * HEAD_DIM
    h = x.reshape(n, d)
    rope_tabs = _rope_tables(positions, 512)
    w_in, w_out = ffn_w_in.astype(BF16), ffn_w_out.astype(BF16)
    w_proj, w_gate = ple_w_proj.astype(BF16), ple_w_gate.astype(BF16)
    mla_wo, fox_wo = mla_w_o.astype(BF16), fox_w_o.astype(BF16)
    p_rows = p.reshape(N_LAYERS, n, D_PLE)
    for i in range(N_LAYERS):
        mixer, j = i % N_MIXERS, i // N_MIXERS
        g = norm_g[i].reshape(4, 1, d)
        if mixer == 0:
            w = _mla_weights(mla_w_a[j], mla_q_norm[j], mla_kv_norm[j], mla_w_uq[j], mla_w_ukv[j])
            q, k, vt = _mla_proj(h, rope_tabs, g[0], w, 512, batch, seq)
            o = _flash(q.reshape(batch, seq, -1), k.reshape(batch, seq, -1), vt, batch, seq,
                       shared_lanes=False, q_off=0, k_off=0, tq=512)
            oproj = (o.reshape(n, inner), mla_wo, j, g[1])
        elif mixer == 1:
            n_cols = dil_w_qkv.shape[-1]
            scale = _q_scale_row(n_cols, [gi * 3 * inner for gi in range(len(DIL_PATTERNS))], inner,
                                 HEAD_DIM ** -0.5 * LOG2E)
            wq = dil_w_qkv[j]
            wvt = jnp.stack([wq[:, (3 * gi + 2) * inner:(3 * gi + 3) * inner].T
                             for gi in range(len(DIL_PATTERNS))]).astype(BF16)
            wq = wq.astype(BF16)
            table = rel_bias.reshape(REL_BUCKETS, len(DIL_PATTERNS), HEADS)
            h3 = h.reshape(batch, seq, d)
            outs, lses = [], []
            for gi in range(len(DIL_PATTERNS)):
                qk, vt = _dil_proj(h3, g[0], wq, wvt, scale, gi)
                o, lse = _dil_attention(qk, vt, table[:, gi], gi, batch, seq)
                outs.append(o)
                lses.append(lse)
            h = _oproj_merge(outs, lses, dil_w_o[j].astype(BF16), h, g[1], 512, batch, seq)
            oproj = None
        else:
            wq = fox_w_qkvf[j]
            w = dict(
                wqk=wq[:, :2 * inner].astype(BF16),
                scale=_q_scale_row(2 * inner, [0], inner, HEAD_DIM ** -0.5 * LOG2E),
                wvt=wq[:, 2 * inner:3 * inner].T.astype(BF16),
                wf=jnp.pad(wq[:, 3 * inner:], ((0, 0), (0, LANES - HEADS))).astype(BF16),
                bf=jnp.pad(fox_b_f[j], (0, LANES - HEADS)).reshape(1, LANES),
            )
            a, vt, logf = _fox_proj(h, g[0], w, 512, batch, seq)
            c, ct = _cumsum(logf, batch, seq)
            a3 = a.reshape(batch, seq, 2 * inner)
            o = _flash(a3, a3, vt, batch, seq, shared_lanes=True, q_off=0, k_off=PAIRS, forget=(c, ct))
            oproj = (o.reshape(n, inner), fox_wo, j, g[1])
        h = _ffn(h, p_rows, i, g[2], g[3], w_in, w_out, w_proj, w_gate, tm=512, oproj=oproj)
    return h.reshape(batch, seq, d)
```

```python
import functools

import numpy as np
import jax
import jax.numpy as jnp
from jax import lax
from jax.experimental import pallas as pl
from jax.experimental.pallas import tpu as pltpu

F32 = jnp.float32
BF16 = jnp.bfloat16

D_MODEL = 1024
N_LAYERS = 4
N_MIXERS = 3
D_PLE = 256
EPS = 1e-6
NEG = -1e30
D_FF = 2816

HEADS = 16
HEAD_DIM = 64
LANES = 128
PAIRS = HEADS // 2
ONES_ROWS = 16
LOG2E = 1.4426950408889634

MLA_Q_RANK = 384
MLA_KV_RANK = 256
MLA_NOPE = 64
MLA_ROPE = 32
ROPE_HALF = MLA_ROPE // 2
ROPE_THETA = 10000.0

DIL_PATTERNS = ((128, 1), (512, 4), (2048, 16))
DIL_BLOCK = 128
REL_BUCKETS = 32
REL_MAX_DIST = 2048

VMEM_LIMIT = 56 * 1024 * 1024


def _params(sem):
    return pltpu.CompilerParams(dimension_semantics=sem, vmem_limit_bytes=VMEM_LIMIT)


def _rms(x, g):
    y = x * lax.rsqrt(jnp.mean(x * x, axis=-1, keepdims=True) + EPS)
    return y * g


def _dot(a, b):
    return jnp.dot(a, b, preferred_element_type=F32)


def _dot_nt(a, b):
    return lax.dot_general(a, b, (((1,), (1,)), ((), ())), preferred_element_type=F32)


def _const_spec(shape):
    nd = len(shape)
    return pl.BlockSpec(shape, lambda *_: (0,) * nd)


ROW_CHUNK = 256


def _dil_proj_body(h_ref, g_ref, w_ref, wvt_ref, scale_ref, qk_ref, vt_ref, slab_ref, perm_ref,
                   *, dilation, seq):
    c = pl.program_id(1)
    n_slab = D_MODEL // LANES

    @pl.when(c == 0)
    def _():
        part = slab_ref.shape[1]
        for h0 in range(0, seq, part):
            def norm_rows(i, _):
                rows = pl.multiple_of(i * ROW_CHUNK, ROW_CHUNK)
                xn = _rms(h_ref[0, pl.ds(h0 + rows, ROW_CHUNK), :], g_ref[...])
                if dilation == 1:
                    perm_ref[pl.ds(h0 + rows, ROW_CHUNK), :] = xn.astype(BF16)
                else:
                    for s in range(n_slab):
                        slab_ref[s, pl.ds(rows, ROW_CHUNK), :] = xn[:, s * LANES:(s + 1) * LANES]
                return 0

            lax.fori_loop(0, part // ROW_CHUNK, norm_rows, 0)
            if dilation > 1:
                sub, cnt = seq // dilation, part // dilation
                for r in range(dilation):
                    dst = r * sub + h0 // dilation
                    for s in range(n_slab):
                        perm_ref[dst:dst + cnt, s * LANES:(s + 1) * LANES] = (
                            slab_ref[s, pl.ds(r, cnt, stride=dilation), :].astype(BF16))

    @pl.when(c < 2)
    def _():
        qk_ref[0] = (_dot(perm_ref[...], w_ref[...]) * scale_ref[...]).astype(BF16)

    @pl.when(c == 2)
    def _():
        vt_ref[0] = _dot_nt(wvt_ref[0], perm_ref[...]).astype(BF16)


def _dil_proj(h3, g, w, wvt, scale, group):
    batch, seq, d = h3.shape
    dilation = DIL_PATTERNS[group][1]
    width = HEADS * HEAD_DIM
    col = lambda b, c: (0, group * 3 + jnp.minimum(c, 1))
    return pl.pallas_call(
        functools.partial(_dil_proj_body, dilation=dilation, seq=seq),
        grid=(batch, 3),
        in_specs=[
            pl.BlockSpec((1, seq, d), lambda b, c: (b, 0, 0)),
            pl.BlockSpec((1, d), lambda b, c: (0, 0)),
            pl.BlockSpec((d, width), col),
            pl.BlockSpec((1, width, d), lambda b, c: (group, 0, 0)),
            pl.BlockSpec((1, width), col),
        ],
        out_specs=[
            pl.BlockSpec((1, seq, width), lambda b, c: (b, 0, jnp.minimum(c, 1))),
            pl.BlockSpec((1, width, seq), lambda b, c: (b, 0, 0)),
        ],
        out_shape=[jax.ShapeDtypeStruct((batch, seq, 2 * width), BF16),
                   jax.ShapeDtypeStruct((batch, width, seq), BF16)],
        scratch_shapes=[pltpu.VMEM((d // LANES, seq // 2, LANES), F32), pltpu.VMEM((seq, d), BF16)],
        compiler_params=_params(("arbitrary", "arbitrary")),
        name=f"dil_proj_g{group}",
    )(h3, g, w, wvt, scale)


def _rope_table_body(pos_ref, inv_ref, c_ref, s_ref):
    tm = pos_ref.shape[-1]
    ang = inv_ref[...] * pos_ref[0].astype(F32)
    cos, sin = jnp.cos(ang), jnp.sin(ang)
    pad = LANES - MLA_NOPE - MLA_ROPE
    c_t = jnp.concatenate([jnp.ones((MLA_NOPE, tm), F32), cos, jnp.zeros((pad, tm), F32)], axis=0)
    s_t = jnp.concatenate([jnp.zeros((MLA_NOPE, tm), F32), -sin[:ROPE_HALF], sin[ROPE_HALF:],
                           jnp.zeros((pad, tm), F32)], axis=0)
    c_ref[...] = c_t.T
    s_ref[...] = s_t.T


def _rope_tables(positions, tm):
    n = positions.size
    inv = ROPE_THETA ** (-jnp.arange(ROPE_HALF, dtype=F32) / ROPE_HALF)
    inv_col = jnp.concatenate([inv, inv]).reshape(MLA_ROPE, 1)
    return pl.pallas_call(
        _rope_table_body,
        grid=(n // tm,),
        in_specs=[pl.BlockSpec((1, 1, tm), lambda i: (i, 0, 0)), _const_spec((MLA_ROPE, 1))],
        out_specs=[pl.BlockSpec((tm, LANES), lambda i: (i, 0))] * 2,
        out_shape=[jax.ShapeDtypeStruct((n, LANES), F32)] * 2,
        compiler_params=_params(("parallel",)),
        name="rope_tables",
    )(positions.reshape(n // tm, 1, tm), inv_col)


def _mla_proj_body(h_ref, ctab_ref, stab_ref, g_ref, wa_ref, qn_ref, kvn_ref,
                   wuq_ref, wuk_ref, wuvt_ref, q_ref, k_ref, vt_ref):
    xn = _rms(h_ref[...], g_ref[...]).astype(BF16)
    a = _dot(xn, wa_ref[...])
    cq = _rms(a[:, :MLA_Q_RANK], qn_ref[...]).astype(BF16)
    ckv = _rms(a[:, MLA_Q_RANK:MLA_Q_RANK + MLA_KV_RANK], kvn_ref[...]).astype(BF16)
    kr = a[:, MLA_Q_RANK + MLA_KV_RANK:]
    q = _dot(cq, wuq_ref[...])
    kn = _dot(ckv, wuk_ref[...])
    vt_ref[0] = _dot_nt(wuvt_ref[...], ckv).astype(BF16)

    lane = lax.broadcasted_iota(jnp.int32, (1, LANES), 1)
    first = (lane >= MLA_NOPE) & (lane < MLA_NOPE + ROPE_HALF)
    c_tab, s_tab = ctab_ref[...], stab_ref[...]

    def rope(t):
        other = jnp.where(first, pltpu.roll(t, LANES - ROPE_HALF, 1), pltpu.roll(t, ROPE_HALF, 1))
        return t * c_tab + other * s_tab

    kr = rope(kr)
    scale = (MLA_NOPE + MLA_ROPE) ** -0.5 * LOG2E
    for hh in range(HEADS):
        sl = slice(hh * LANES, (hh + 1) * LANES)
        q_ref[:, sl] = (rope(q[:, sl]) * scale).astype(BF16)
        k_ref[:, sl] = (kn[:, sl] + kr).astype(BF16)


def _vt_spec(tm, seq, width):
    per = seq // tm
    return pl.BlockSpec((1, width, tm), lambda i: (i // per, 0, i % per))


def _mla_proj(h, tabs, g, w, tm, batch, seq):
    n, d = h.shape
    row = lambda i: (i, 0)
    width = HEADS * HEAD_DIM
    return pl.pallas_call(
        _mla_proj_body,
        grid=(n // tm,),
        in_specs=[
            pl.BlockSpec((tm, d), row),
            pl.BlockSpec((tm, LANES), row),
            pl.BlockSpec((tm, LANES), row),
            _const_spec((1, d)),
            _const_spec(w["wa"].shape),
            _const_spec((1, MLA_Q_RANK)), _const_spec((1, MLA_KV_RANK)),
            _const_spec(w["wuq"].shape), _const_spec(w["wuk"].shape), _const_spec(w["wuvt"].shape),
        ],
        out_specs=[
            pl.BlockSpec((tm, HEADS * LANES), row),
            pl.BlockSpec((tm, HEADS * LANES), row),
            _vt_spec(tm, seq, width),
        ],
        out_shape=[
            jax.ShapeDtypeStruct((n, HEADS * LANES), BF16),
            jax.ShapeDtypeStruct((n, HEADS * LANES), BF16),
            jax.ShapeDtypeStruct((batch, width, seq), BF16),
        ],
        compiler_params=_params(("parallel",)),
        name="mla_proj",
    )(h, tabs[0], tabs[1], g, w["wa"], w["qn"], w["kvn"],
      w["wuq"], w["wuk"], w["wuvt"])


def _fox_proj_body(h_ref, g_ref, w_ref, scale_ref, wvt_ref, wf_ref, bf_ref, a_ref, vt_ref, logf_ref):
    xn = _rms(h_ref[...], g_ref[...]).astype(BF16)
    a_ref[...] = (_dot(xn, w_ref[...]) * scale_ref[...]).astype(BF16)
    vt_ref[0] = _dot_nt(wvt_ref[...], xn).astype(BF16)
    f = _dot(xn, wf_ref[...]) + bf_ref[...]
    logf_ref[...] = jnp.minimum(f, 0.0) - jnp.log1p(jnp.exp(-jnp.abs(f)))


def _fox_proj(h, g, w, tm, batch, seq):
    n, d = h.shape
    nout = w["wqk"].shape[1]
    width = HEADS * HEAD_DIM
    row = lambda i: (i, 0)
    return pl.pallas_call(
        _fox_proj_body,
        grid=(n // tm,),
        in_specs=[
            pl.BlockSpec((tm, d), row),
            _const_spec((1, d)),
            _const_spec((d, nout)),
            _const_spec((1, nout)),
            _const_spec((width, d)),
            _const_spec((d, LANES)),
            _const_spec((1, LANES)),
        ],
        out_specs=[pl.BlockSpec((tm, nout), row), _vt_spec(tm, seq, width),
                   pl.BlockSpec((tm, LANES), row)],
        out_shape=[jax.ShapeDtypeStruct((n, nout), BF16),
                   jax.ShapeDtypeStruct((batch, width, seq), BF16),
                   jax.ShapeDtypeStruct((n, LANES), F32)],
        compiler_params=_params(("parallel",)),
        name="fox_proj",
    )(h, g, w["wqk"], w["scale"], w["wvt"], w["wf"], w["bf"])


def _cumsum_body(x_ref, c_ref, ct_ref, *, seq, blk):
    r = lax.broadcasted_iota(jnp.int32, (blk, blk), 0)
    c = lax.broadcasted_iota(jnp.int32, (blk, blk), 1)
    tri = (c <= r).astype(F32)
    carry = jnp.zeros((1, LANES), F32)
    for b in range(seq // blk):
        xs = x_ref[0, b * blk:(b + 1) * blk, :]
        cs = lax.dot_general(tri, xs, (((1,), (0,)), ((), ())), precision=lax.Precision.HIGHEST,
                             preferred_element_type=F32) + carry
        c_ref[0, b * blk:(b + 1) * blk, :] = cs
        carry = cs[blk - 1:blk, :]
    ct_ref[0] = c_ref[0].T


def _cumsum(logf, batch, seq):
    x = logf.reshape(batch, seq, LANES)
    return pl.pallas_call(
        functools.partial(_cumsum_body, seq=seq, blk=256),
        grid=(batch,),
        in_specs=[pl.BlockSpec((1, seq, LANES), lambda b: (b, 0, 0))],
        out_specs=[pl.BlockSpec((1, seq, LANES), lambda b: (b, 0, 0)),
                   pl.BlockSpec((1, LANES, seq), lambda b: (b, 0, 0))],
        out_shape=[jax.ShapeDtypeStruct((batch, seq, LANES), F32),
                   jax.ShapeDtypeStruct((batch, LANES, seq), F32)],
        compiler_params=_params(("parallel",)),
        name="fox_cumsum",
    )(x)


def _flash_body(*refs, seq, tq, forget, shared_lanes):
    if forget:
        q_ref, k_ref, vt_ref, c_ref, ct_ref, o_ref, va_ref, ck_ref = refs
    else:
        q_ref, k_ref, vt_ref, o_ref, va_ref = refs
    pair = pl.program_id(1)
    lane = lax.broadcasted_iota(jnp.int32, (1, LANES), 1)
    low = lane < HEAD_DIM
    causal = (lax.broadcasted_iota(jnp.int32, (tq, tq), 0)
              <= lax.broadcasted_iota(jnp.int32, (tq, tq), 1))
    for e in range(2):
        va_ref[e, :HEAD_DIM, :] = vt_ref[0, e * HEAD_DIM:(e + 1) * HEAD_DIM, :]
        va_ref[e, HEAD_DIM:, :] = jnp.ones((ONES_ROWS, seq), BF16)
    if forget:
        for e in range(2):
            col = jnp.sum(jnp.where(lane == 2 * pair + e, c_ref[0], 0.0), axis=1, keepdims=True)
            ck_ref[e] = jnp.broadcast_to(col * LOG2E, (seq, tq))

    class Chain:
        pass

    def start(qi, e):
        ch = Chain()
        qs = qi * tq
        ch.qi, ch.e = qi, e
        if shared_lanes:
            qp = q_ref[0, qs:qs + tq, :]
            ch.q = jnp.where(low, qp, 0) if e == 0 else jnp.where(low, 0, qp)
            ch.ksl = slice(0, LANES)
        else:
            ch.ksl = slice(e * LANES, (e + 1) * LANES)
            ch.q = q_ref[0, qs:qs + tq, ch.ksl]
        if forget:
            ch.cq = ct_ref[0, pl.ds(2 * pair + e, 1), qs:qs + tq] * LOG2E
        ch.s, ch.m, ch.acc = [], None, None
        return ch

    def pass1(ch, c):
        ks = c * tq
        s = _dot_nt(k_ref[0, ks:ks + tq, ch.ksl], ch.q)
        if forget:
            s = s + (ch.cq - ck_ref[ch.e, ks:ks + tq, :])
        if c == ch.qi:
            s = jnp.where(causal, s, NEG)
        mc = jnp.max(s, axis=0, keepdims=True)
        ch.m = mc if ch.m is None else jnp.maximum(ch.m, mc)
        ch.s.append(s)

    def pass2(ch, c):
        ks = c * tq
        p = jnp.exp2(ch.s[c] - ch.m)
        ac = _dot(va_ref[ch.e, :, ks:ks + tq], p.astype(BF16))
        ch.acc = ac if ch.acc is None else ch.acc + ac

    outs = []

    def finish(ch):
        outs.append(ch.acc[:HEAD_DIM] / ch.acc[HEAD_DIM:HEAD_DIM + 1])
        if ch.e == 1:
            qs = ch.qi * tq
            o_t = jnp.concatenate(outs, axis=0)
            o_ref[0, qs:qs + tq, :] = o_t.T.astype(o_ref.dtype)
            outs.clear()

    prev = ()
    for qi in range(seq // tq + 1):
        cur = tuple(start(qi, e) for e in range(2)) if qi < seq // tq else ()
        for c in range(qi + 1):
            for ch in cur:
                pass1(ch, c)
            if c < qi:
                for ch in prev:
                    pass2(ch, c)
        for ch in prev:
            finish(ch)
        prev = cur


def _flash(q, k, vt, batch, seq, *, shared_lanes, q_off, k_off, forget=None, tq=256):
    qk_w = LANES if shared_lanes else 2 * LANES
    in_specs = [
        pl.BlockSpec((1, seq, qk_w), lambda b, p: (b, 0, q_off + p)),
        pl.BlockSpec((1, seq, qk_w), lambda b, p: (b, 0, k_off + p)),
        pl.BlockSpec((1, LANES, seq), lambda b, p: (b, p, 0)),
    ]
    args = [q, k, vt]
    scratch = [pltpu.VMEM((2, HEAD_DIM + ONES_ROWS, seq), BF16)]
    if forget is not None:
        c, ct = forget
        in_specs += [pl.BlockSpec((1, seq, LANES), lambda b, p: (b, 0, 0)),
                     pl.BlockSpec((1, HEADS, seq), lambda b, p: (b, 0, 0))]
        args += [c, ct]
        scratch += [pltpu.VMEM((2, seq, tq), F32)]
    return pl.pallas_call(
        functools.partial(_flash_body, seq=seq, tq=tq, forget=forget is not None,
                          shared_lanes=shared_lanes),
        grid=(batch, PAIRS),
        in_specs=in_specs,
        out_specs=pl.BlockSpec((1, seq, LANES), lambda b, p: (b, 0, p)),
        out_shape=jax.ShapeDtypeStruct((batch, seq, HEADS * HEAD_DIM), BF16),
        scratch_shapes=scratch,
        compiler_params=_params(("parallel", "parallel")),
        name="flash_fox" if forget is not None else "flash_mla",
    )(*args)


def _t5_bucket_np(dist):
    max_exact = REL_BUCKETS // 2
    n = np.maximum(dist.astype(np.float32), np.float32(1.0))
    large = max_exact + (np.log(n / np.float32(max_exact)) / np.float32(np.log(REL_MAX_DIST / max_exact))
                         * np.float32(REL_BUCKETS - max_exact)).astype(np.int32)
    large = np.minimum(large, REL_BUCKETS - 1)
    return np.where(dist < max_exact, dist, large).astype(np.int32)


def _dil_bucket_map(dilation, use_prev):
    qry = np.arange(DIL_BLOCK)
    key = np.arange(2 * DIL_BLOCK)
    rel = DIL_BLOCK + qry[None, :] - key[:, None]
    bk = _t5_bucket_np(np.clip(rel, 0, None) * dilation)
    if not use_prev:
        bk = bk[DIL_BLOCK:]
    return np.concatenate([bk, bk], axis=1)


def _dil_body(tab_ref, bucket_ref, q_ref, k_ref, vt_ref, o_ref, lse_ref, bias_ref, *, span, n_blk, seq):
    blk = DIL_BLOCK
    use_prev = n_blk > 1
    nkeys = 2 * blk if use_prev else blk

    @pl.when(pl.program_id(0) == 0)
    def _():
        bk = bucket_ref[...]
        second = lax.broadcasted_iota(jnp.int32, (1, 2 * blk), 1) >= blk
        key = lax.broadcasted_iota(jnp.int32, (nkeys, 2 * blk), 0) + (0 if use_prev else blk)
        qry = lax.broadcasted_iota(jnp.int32, (nkeys, 2 * blk), 1) & (blk - 1)
        rel = blk + qry - key
        band = (rel >= 0) & (rel <= span)

        def fill(p, _):
            acc = jnp.zeros((nkeys, 2 * blk), F32)
            for bb in range(REL_BUCKETS):
                val = jnp.where(second, tab_ref[bb, 2 * p + 1], tab_ref[bb, 2 * p])
                acc = jnp.where(bk == bb, val * LOG2E, acc)
            bias_ref[0, p] = jnp.where(band & (key >= blk), acc, NEG)
            bias_ref[1, p] = jnp.where(band, acc, NEG)
            return 0

        lax.fori_loop(0, PAIRS, fill, 0)

    low = lax.broadcasted_iota(jnp.int32, (1, LANES), 1) < HEAD_DIM
    head_row = lax.broadcasted_iota(jnp.int32, (LANES, blk), 0)

    def block(s, _):
        cur = pl.ds(pl.multiple_of(s * blk, blk), blk)
        cls, rows = s // n_blk, pl.ds(pl.multiple_of((s % n_blk) * blk, blk), blk)
        if use_prev:
            n = s % n_blk
            prv = pl.ds(pl.multiple_of(jnp.where(n > 0, s - 1, s) * blk, blk), blk)
            table = jnp.where(n > 0, 1, 0)
        else:
            table = 1
        def logits(p):
            sl = slice(p * LANES, (p + 1) * LANES)
            qp = q_ref[0, cur, sl]
            qq = jnp.concatenate([jnp.where(low, qp, 0), jnp.where(low, 0, qp)], axis=0)
            if use_prev:
                kk = jnp.concatenate([k_ref[0, prv, sl], k_ref[0, cur, sl]], axis=0)
            else:
                kk = k_ref[0, cur, sl]
            return _dot_nt(kk, qq)

        def masked(p, raw):
            st = raw + bias_ref[table, p]
            return st, jnp.max(st, axis=0, keepdims=True)

        def weighted(p, st, m):
            sl = slice(p * LANES, (p + 1) * LANES)
            if use_prev:
                vv = jnp.concatenate([vt_ref[0, sl, prv], vt_ref[0, sl, cur]], axis=1)
            else:
                vv = vt_ref[0, sl, cur]
            vv = jnp.concatenate([vv, jnp.ones((ONES_ROWS, nkeys), BF16)], axis=0)
            return _dot(vv, jnp.exp2(st - m).astype(BF16))

        def finish(p, ot, m, lse_t):
            sl = slice(p * LANES, (p + 1) * LANES)
            l = ot[LANES:LANES + 1]
            ot = ot[:LANES] / l
            o_t = jnp.concatenate([ot[:HEAD_DIM, :blk], ot[HEAD_DIM:, blk:]], axis=0)
            o_ref[0, cls, rows, sl] = o_t.T.astype(o_ref.dtype)
            lse = m * (1.0 / LOG2E) + jnp.log(l)
            return jnp.where(head_row == 2 * p, lse[:, :blk],
                             jnp.where(head_row == 2 * p + 1, lse[:, blk:], lse_t))

        raws = [logits(p) for p in range(PAIRS)]
        sms = [masked(p, raws[p]) for p in range(PAIRS)]
        ots = [weighted(p, *sms[p]) for p in range(PAIRS)]
        lse_t = jnp.zeros((LANES, blk), F32)
        for p in range(PAIRS):
            lse_t = finish(p, ots[p], sms[p][1], lse_t)
        lse_ref[0, cls, rows, :] = lse_t.T
        return 0

    lax.fori_loop(0, seq // blk, block, 0, unroll=2)


def _dil_attention(qk, vt, table, group, batch, seq):
    window, dilation = DIL_PATTERNS[group]
    span = window // dilation
    sub = seq // dilation
    assert sub % DIL_BLOCK == 0
    n_blk = sub // DIL_BLOCK
    width = HEADS * HEAD_DIM
    bucket = jnp.asarray(_dil_bucket_map(dilation, n_blk > 1))
    return pl.pallas_call(
        functools.partial(_dil_body, span=span, n_blk=n_blk, seq=seq),
        grid=(batch,),
        in_specs=[
            pl.BlockSpec(memory_space=pltpu.SMEM),
            _const_spec(bucket.shape),
            pl.BlockSpec((1, seq, width), lambda b: (b, 0, 0)),
            pl.BlockSpec((1, seq, width), lambda b: (b, 0, 1)),
            pl.BlockSpec((1, width, seq), lambda b: (b, 0, 0)),
        ],
        out_specs=[pl.BlockSpec((1, dilation, sub, width), lambda b: (b, 0, 0, 0)),
                   pl.BlockSpec((1, dilation, sub, LANES), lambda b: (b, 0, 0, 0))],
        out_shape=[jax.ShapeDtypeStruct((batch, dilation, sub, width), BF16),
                   jax.ShapeDtypeStruct((batch, dilation, sub, LANES), F32)],
        scratch_shapes=[pltpu.VMEM((2, PAIRS) + bucket.shape, F32)],
        compiler_params=_params(("arbitrary",)),
        name=f"dil_attn_g{group}",
    )(table, bucket, qk, qk, vt)


def _oproj_merge_body(o0_ref, o1_ref, o2_ref, l0_ref, l1_ref, l2_ref, spread_ref, w_ref, h_ref, g_ref,
                      out_ref, lse_ref, slab_ref, merged_ref):
    o_refs = (o0_ref, o1_ref, o2_ref)
    l_refs = (l0_ref, l1_ref, l2_ref)
    n_groups = len(DIL_PATTERNS)
    tm = h_ref.shape[0]
    lses = []
    for gi in range(n_groups):
        dil = DIL_PATTERNS[gi][1]
        if dil == 1:
            lses.append(l_refs[gi][0, 0])
            continue
        for r in range(dil):
            rows = pl.ds(r, tm // dil, stride=dil)
            lse_ref[gi, rows, :] = l_refs[gi][0, r]
            for s in range(PAIRS):
                slab_ref[gi, s, rows, :] = o_refs[gi][0, r, :, s * LANES:(s + 1) * LANES].astype(F32)
        lses.append(lse_ref[gi])
    mx = functools.reduce(jnp.maximum, lses)
    ex = [jnp.exp(t - mx) for t in lses]
    den = functools.reduce(jnp.add, ex)
    wide = []
    for t in ex:
        a = t / den
        a_hi = a.astype(BF16)
        a_lo = (a - a_hi.astype(F32)).astype(BF16)
        wide.append(_dot(a_hi, spread_ref[...]) + _dot(a_lo, spread_ref[...]))
    for p in range(PAIRS):
        sl = slice(p * LANES, (p + 1) * LANES)
        acc = jnp.zeros((tm, LANES), F32)
        for gi in range(n_groups):
            if DIL_PATTERNS[gi][1] == 1:
                acc = acc + wide[gi][:, sl] * o_refs[gi][0, 0, :, sl].astype(F32)
            else:
                acc = acc + wide[gi][:, sl] * slab_ref[gi, p]
        merged_ref[:, sl] = acc.astype(BF16)
    out_ref[...] = h_ref[...] + _rms(_dot(merged_ref[...], w_ref[...]), g_ref[...])


def _oproj_merge(os_, lses, w, h, g, tm, batch, seq):
    n, d = h.shape
    row = lambda i: (i, 0)
    width = os_[0].shape[-1]
    per = seq // tm
    n_groups = len(DIL_PATTERNS)

    def grouped_spec(dil, cols):
        return pl.BlockSpec((1, dil, tm // dil, cols), lambda i: (i // per, 0, i % per, 0))

    dils = [dil for _, dil in DIL_PATTERNS]
    spread = np.zeros((LANES, width), np.float32)
    for hh in range(HEADS):
        spread[hh, hh * HEAD_DIM:(hh + 1) * HEAD_DIM] = 1.0
    spread = jnp.asarray(spread, BF16)
    return pl.pallas_call(
        _oproj_merge_body,
        grid=(n // tm,),
        in_specs=[grouped_spec(dil, width) for dil in dils] + [grouped_spec(dil, LANES) for dil in dils]
        + [_const_spec(spread.shape), _const_spec(w.shape), pl.BlockSpec((tm, d), row), _const_spec((1, d))],
        out_specs=pl.BlockSpec((tm, d), row),
        out_shape=jax.ShapeDtypeStruct((n, d), F32),
        scratch_shapes=[pltpu.VMEM((n_groups, tm, LANES), F32),
                        pltpu.VMEM((n_groups, PAIRS, tm, LANES), F32),
                        pltpu.VMEM((tm, width), BF16)],
        compiler_params=_params(("parallel",)),
        name="oproj_merge",
    )(*os_, *lses, spread, w, h, g)


MXU_TILE = 256
FFN_SPLITS = (0, 6 * MXU_TILE, D_FF)


def _ffn_body(*refs, with_oproj):
    if with_oproj:
        o_ref, wattn_ref, g1_ref, *refs = refs
    h_ref, p_ref, g2_ref, g3_ref, win_ref, wo_ref, wproj_ref, wgate_ref, out_ref = refs
    h = h_ref[...]
    if with_oproj:
        h = h + _rms(_dot(o_ref[...], wattn_ref[0]), g1_ref[...])
    xn = _rms(h, g2_ref[...]).astype(BF16)
    y = None
    for a, b in zip(FFN_SPLITS[:-1], FFN_SPLITS[1:]):
        gate = _dot(xn, win_ref[0, :, a:b])
        up = _dot(xn, win_ref[0, :, D_FF + a:D_FF + b])
        act = (gate * jax.nn.sigmoid(gate) * up).astype(BF16)
        part = _dot(act, wo_ref[0, a:b, :])
        y = part if y is None else y + part
    h2 = h + _rms(y, g3_ref[...])
    emb = _dot(p_ref[0].astype(BF16), wproj_ref[0])
    out_ref[...] = h2 + emb * jax.nn.sigmoid(_dot(h2.astype(BF16), wgate_ref[0]))


def _layer_spec(stacked, layer):
    nd = stacked.ndim - 1
    return pl.BlockSpec((1,) + stacked.shape[1:], lambda *_: (layer,) + (0,) * nd,
                        pipeline_mode=pl.Buffered(1))


def _ffn(h, p, layer, g2, g3, w_in, w_out, w_proj, w_gate, tm, oproj=None):
    n, d = h.shape
    row = lambda i: (i, 0)
    head_specs, head_args = [], []
    if oproj is not None:
        o, w_attn, attn_layer, g1 = oproj
        head_specs = [pl.BlockSpec((tm, o.shape[1]), row), _layer_spec(w_attn, attn_layer),
                      _const_spec((1, d))]
        head_args = [o, w_attn, g1]
    return pl.pallas_call(
        functools.partial(_ffn_body, with_oproj=oproj is not None),
        grid=(n // tm,),
        in_specs=head_specs + [
            pl.BlockSpec((tm, d), row),
            pl.BlockSpec((1, tm, D_PLE), lambda i: (layer, i, 0)),
            _const_spec((1, d)),
            _const_spec((1, d)),
            _layer_spec(w_in, layer),
            _layer_spec(w_out, layer),
            _layer_spec(w_proj, layer),
            _layer_spec(w_gate, layer),
        ],
        out_specs=pl.BlockSpec((tm, d), row),
        out_shape=jax.ShapeDtypeStruct((n, d), F32),
        compiler_params=_params(("parallel",)),
        name="ffn",
    )(*head_args, h, p, g2, g3, w_in, w_out, w_proj, w_gate)


def _mla_weights(w_a, q_norm, kv_norm, w_uq, w_ukv):
    rank = MLA_Q_RANK + MLA_KV_RANK
    wa = jnp.pad(w_a, ((0, 0), (0, LANES - MLA_ROPE)))
    wa = jnp.concatenate([wa[:, :rank], jnp.roll(wa[:, rank:], MLA_NOPE, axis=1)], axis=1).astype(BF16)
    uq = w_uq.reshape(MLA_Q_RANK, HEADS, MLA_NOPE + MLA_ROPE)
    wuq = jnp.pad(uq, ((0, 0), (0, 0), (0, LANES - MLA_NOPE - MLA_ROPE)))
    wuq = wuq.reshape(MLA_Q_RANK, HEADS * LANES).astype(BF16)
    ukv = w_ukv.reshape(MLA_KV_RANK, HEADS, MLA_NOPE + HEAD_DIM)
    wuk = jnp.pad(ukv[:, :, :MLA_NOPE], ((0, 0), (0, 0), (0, LANES - MLA_NOPE)))
    wuk = wuk.reshape(MLA_KV_RANK, HEADS * LANES).astype(BF16)
    wuvt = ukv[:, :, MLA_NOPE:].reshape(MLA_KV_RANK, HEADS * HEAD_DIM).T.astype(BF16)
    return dict(wa=wa, qn=q_norm.reshape(1, -1), kvn=kv_norm.reshape(1, -1),
                wuq=wuq, wuk=wuk, wuvt=wuvt)


def _q_scale_row(n_cols, q_starts, width, scale):
    row = np.ones((1, n_cols), np.float32)
    for s in q_starts:
        row[0, s:s + width] = scale
    return jnp.asarray(row)


def kernel(x, p, positions, norm_g, ffn_w_in, ffn_w_out, ple_w_proj, ple_w_gate, rel_bias, mla_w_a, mla_q_norm, mla_kv_norm, mla_w_uq, mla_w_ukv, mla_w_o, dil_w_qkv, dil_w_o, fox_w_qkvf, fox_b_f, fox_w_o):
    batch, seq, d = x.shape
    n = batch * seq
    inner = HEADS * HEAD_DIM
    h = x.reshape(n, d)
    rope_tabs = _rope_tables(positions, 512)
    w_in, w_out = ffn_w_in.astype(BF16), ffn_w_out.astype(BF16)
    w_proj, w_gate = ple_w_proj.astype(BF16), ple_w_gate.astype(BF16)
    mla_wo, fox_wo = mla_w_o.astype(BF16), fox_w_o.astype(BF16)
    p_rows = p.reshape(N_LAYERS, n, D_PLE)
    for i in range(N_LAYERS):
        mixer, j = i % N_MIXERS, i // N_MIXERS
        g = norm_g[i].reshape(4, 1, d)
        if mixer == 0:
            w = _mla_weights(mla_w_a[j], mla_q_norm[j], mla_kv_norm[j], mla_w_uq[j], mla_w_ukv[j])
            q, k, vt = _mla_proj(h, rope_tabs, g[0], w, 512, batch, seq)
            o = _flash(q.reshape(batch, seq, -1), k.reshape(batch, seq, -1), vt, batch, seq,
                       shared_lanes=False, q_off=0, k_off=0, tq=512)
            oproj = (o.reshape(n, inner), mla_wo, j, g[1])
        elif mixer == 1:
            n_cols = dil_w_qkv.shape[-1]
            scale = _q_scale_row(n_cols, [gi * 3 * inner for gi in range(len(DIL_PATTERNS))], inner,
                                 HEAD_DIM ** -0.5 * LOG2E)
            wq = dil_w_qkv[j]
            wvt = jnp.stack([wq[:, (3 * gi + 2) * inner:(3 * gi + 3) * inner].T
                             for gi in range(len(DIL_PATTERNS))]).astype(BF16)
            wq = wq.astype(BF16)
            table = rel_bias.reshape(REL_BUCKETS, len(DIL_PATTERNS), HEADS)
            h3 = h.reshape(batch, seq, d)
            outs, lses = [], []
            for gi in range(len(DIL_PATTERNS)):
                qk, vt = _dil_proj(h3, g[0], wq, wvt, scale, gi)
                o, lse = _dil_attention(qk, vt, table[:, gi], gi, batch, seq)
                outs.append(o)
                lses.append(lse)
            h = _oproj_merge(outs, lses, dil_w_o[j].astype(BF16), h, g[1], 512, batch, seq)
            oproj = None
        else:
            wq = fox_w_qkvf[j]
            w = dict(
                wqk=wq[:, :2 * inner].astype(BF16),
                scale=_q_scale_row(2 * inner, [0], inner, HEAD_DIM ** -0.5 * LOG2E),
                wvt=wq[:, 2 * inner:3 * inner].T.astype(BF16),
                wf=jnp.pad(wq[:, 3 * inner:], ((0, 0), (0, LANES - HEADS))).astype(BF16),
                bf=jnp.pad(fox_b_f[j], (0, LANES - HEADS)).reshape(1, LANES),
            )
            a, vt, logf = _fox_proj(h, g[0], w, 512, batch, seq)
            c, ct = _cumsum(logf, batch, seq)
            a3 = a.reshape(batch, seq, 2 * inner)
            o = _flash(a3, a3, vt, batch, seq, shared_lanes=True, q_off=0, k_off=PAIRS, forget=(c, ct))
            oproj = (o.reshape(n, inner), fox_wo, j, g[1])
        h = _ffn(h, p_rows, i, g[2], g[3], w_in, w_out, w_proj, w_gate, tm=512, oproj=oproj)
    return h.reshape(batch, seq, d)
```

```python
import functools

import numpy as np
import jax
import jax.numpy as jnp
from jax import lax
from jax.experimental import pallas as pl
from jax.experimental.pallas import tpu as pltpu

F32 = jnp.float32
BF16 = jnp.bfloat16

D_MODEL = 1024
N_LAYERS = 4
N_MIXERS = 3
D_PLE = 256
EPS = 1e-6
NEG = -1e30
D_FF = 2816

HEADS = 16
HEAD_DIM = 64
LANES = 128
PAIRS = HEADS // 2
ONES_ROWS = 16
LOG2E = 1.4426950408889634

MLA_Q_RANK = 384
MLA_KV_RANK = 256
MLA_NOPE = 64
MLA_ROPE = 32
ROPE_HALF = MLA_ROPE // 2
ROPE_THETA = 10000.0

DIL_PATTERNS = ((128, 1), (512, 4), (2048, 16))
DIL_BLOCK = 128
REL_BUCKETS = 32
REL_MAX_DIST = 2048

VMEM_LIMIT = 56 * 1024 * 1024


def _params(sem):
    return pltpu.CompilerParams(dimension_semantics=sem, vmem_limit_bytes=VMEM_LIMIT)


def _rms(x, g):
    y = x * lax.rsqrt(jnp.mean(x * x, axis=-1, keepdims=True) + EPS)
    return y * g


def _dot(a, b):
    return jnp.dot(a, b, preferred_element_type=F32)


def _dot_nt(a, b):
    return lax.dot_general(a, b, (((1,), (1,)), ((), ())), preferred_element_type=F32)


def _dot_tt(a, b):
    return lax.dot_general(a, b, (((0,), (1,)), ((), ())), preferred_element_type=F32)


def _const_spec(shape):
    nd = len(shape)
    return pl.BlockSpec(shape, lambda *_: (0,) * nd)


ROW_CHUNK = 256


def _dil_proj_body(h_ref, g_ref, w_ref, scale_ref, qk_ref, vt_ref, slab_ref, perm_ref,
                   *, dilation, seq):
    c = pl.program_id(1)
    n_slab = D_MODEL // LANES

    @pl.when(c == 0)
    def _():
        part = slab_ref.shape[1]
        for h0 in range(0, seq, part):
            def norm_rows(i, _):
                rows = pl.multiple_of(i * ROW_CHUNK, ROW_CHUNK)
                xn = _rms(h_ref[0, pl.ds(h0 + rows, ROW_CHUNK), :], g_ref[...])
                if dilation == 1:
                    perm_ref[pl.ds(h0 + rows, ROW_CHUNK), :] = xn.astype(BF16)
                else:
                    for s in range(n_slab):
                        slab_ref[s, pl.ds(rows, ROW_CHUNK), :] = xn[:, s * LANES:(s + 1) * LANES]
                return 0

            lax.fori_loop(0, part // ROW_CHUNK, norm_rows, 0)
            if dilation > 1:
                sub, cnt = seq // dilation, part // dilation
                for r in range(dilation):
                    dst = r * sub + h0 // dilation
                    for s in range(n_slab):
                        perm_ref[dst:dst + cnt, s * LANES:(s + 1) * LANES] = (
                            slab_ref[s, pl.ds(r, cnt, stride=dilation), :].astype(BF16))

    @pl.when(c < 2)
    def _():
        qk_ref[0] = (_dot(perm_ref[...], w_ref[...]) * scale_ref[...]).astype(BF16)

    @pl.when(c == 2)
    def _():
        vt_ref[0] = _dot_tt(w_ref[...], perm_ref[...]).astype(BF16)


def _dil_proj(h3, g, w, scale, group):
    batch, seq, d = h3.shape
    dilation = DIL_PATTERNS[group][1]
    width = HEADS * HEAD_DIM
    col = lambda b, c: (0, group * 3 + c)
    return pl.pallas_call(
        functools.partial(_dil_proj_body, dilation=dilation, seq=seq),
        grid=(batch, 3),
        in_specs=[
            pl.BlockSpec((1, seq, d), lambda b, c: (b, 0, 0)),
            pl.BlockSpec((1, d), lambda b, c: (0, 0)),
            pl.BlockSpec((d, width), col),
            pl.BlockSpec((1, width), col),
        ],
        out_specs=[
            pl.BlockSpec((1, seq, width), lambda b, c: (b, 0, jnp.minimum(c, 1))),
            pl.BlockSpec((1, width, seq), lambda b, c: (b, 0, 0)),
        ],
        out_shape=[jax.ShapeDtypeStruct((batch, seq, 2 * width), BF16),
                   jax.ShapeDtypeStruct((batch, width, seq), BF16)],
        scratch_shapes=[pltpu.VMEM((d // LANES, seq // 2, LANES), F32), pltpu.VMEM((seq, d), BF16)],
        compiler_params=_params(("arbitrary", "arbitrary")),
        name=f"dil_proj_g{group}",
    )(h3, g, w, scale)


def _rope_table_body(pos_ref, inv_ref, c_ref, s_ref):
    tm = pos_ref.shape[-1]
    ang = inv_ref[...] * pos_ref[0].astype(F32)
    cos, sin = jnp.cos(ang), jnp.sin(ang)
    pad = LANES - MLA_NOPE - MLA_ROPE
    c_t = jnp.concatenate([jnp.ones((MLA_NOPE, tm), F32), cos, jnp.zeros((pad, tm), F32)], axis=0)
    s_t = jnp.concatenate([jnp.zeros((MLA_NOPE, tm), F32), -sin[:ROPE_HALF], sin[ROPE_HALF:],
                           jnp.zeros((pad, tm), F32)], axis=0)
    c_ref[...] = c_t.T
    s_ref[...] = s_t.T


def _rope_tables(positions, tm):
    n = positions.size
    inv = ROPE_THETA ** (-jnp.arange(ROPE_HALF, dtype=F32) / ROPE_HALF)
    inv_col = jnp.concatenate([inv, inv]).reshape(MLA_ROPE, 1)
    return pl.pallas_call(
        _rope_table_body,
        grid=(n // tm,),
        in_specs=[pl.BlockSpec((1, 1, tm), lambda i: (i, 0, 0)), _const_spec((MLA_ROPE, 1))],
        out_specs=[pl.BlockSpec((tm, LANES), lambda i: (i, 0))] * 2,
        out_shape=[jax.ShapeDtypeStruct((n, LANES), F32)] * 2,
        compiler_params=_params(("parallel",)),
        name="rope_tables",
    )(positions.reshape(n // tm, 1, tm), inv_col)


def _mla_proj_body(h_ref, ctab_ref, stab_ref, g_ref, wa_ref, qn_ref, kvn_ref,
                   wuq_ref, wuk_ref, wuv_ref, q_ref, k_ref, vt_ref):
    xn = _rms(h_ref[...], g_ref[...]).astype(BF16)
    a = _dot(xn, wa_ref[...])
    cq = _rms(a[:, :MLA_Q_RANK], qn_ref[...]).astype(BF16)
    ckv = _rms(a[:, MLA_Q_RANK:MLA_Q_RANK + MLA_KV_RANK], kvn_ref[...]).astype(BF16)
    kr = a[:, MLA_Q_RANK + MLA_KV_RANK:]
    q = _dot(cq, wuq_ref[...])
    kn = _dot(ckv, wuk_ref[...])
    vt_ref[0] = _dot_tt(wuv_ref[...], ckv).astype(BF16)

    lane = lax.broadcasted_iota(jnp.int32, (1, LANES), 1)
    first = (lane >= MLA_NOPE) & (lane < MLA_NOPE + ROPE_HALF)
    c_tab, s_tab = ctab_ref[...], stab_ref[...]

    def rope(t):
        other = jnp.where(first, pltpu.roll(t, LANES - ROPE_HALF, 1), pltpu.roll(t, ROPE_HALF, 1))
        return t * c_tab + other * s_tab

    kr = rope(kr)
    scale = (MLA_NOPE + MLA_ROPE) ** -0.5 * LOG2E
    for hh in range(HEADS):
        sl = slice(hh * LANES, (hh + 1) * LANES)
        q_ref[:, sl] = (rope(q[:, sl]) * scale).astype(BF16)
        k_ref[:, sl] = (kn[:, sl] + kr).astype(BF16)


def _vt_spec(tm, seq, width):
    per = seq // tm
    return pl.BlockSpec((1, width, tm), lambda i: (i // per, 0, i % per))


def _mla_proj(h, tabs, g, w, tm, batch, seq):
    n, d = h.shape
    row = lambda i: (i, 0)
    width = HEADS * HEAD_DIM
    return pl.pallas_call(
        _mla_proj_body,
        grid=(n // tm,),
        in_specs=[
            pl.BlockSpec((tm, d), row),
            pl.BlockSpec((tm, LANES), row),
            pl.BlockSpec((tm, LANES), row),
            _const_spec((1, d)),
            _const_spec(w["wa"].shape),
            _const_spec((1, MLA_Q_RANK)), _const_spec((1, MLA_KV_RANK)),
            _const_spec(w["wuq"].shape), _const_spec(w["wuk"].shape), _const_spec(w["wuv"].shape),
        ],
        out_specs=[
            pl.BlockSpec((tm, HEADS * LANES), row),
            pl.BlockSpec((tm, HEADS * LANES), row),
            _vt_spec(tm, seq, width),
        ],
        out_shape=[
            jax.ShapeDtypeStruct((n, HEADS * LANES), BF16),
            jax.ShapeDtypeStruct((n, HEADS * LANES), BF16),
            jax.ShapeDtypeStruct((batch, width, seq), BF16),
        ],
        compiler_params=_params(("parallel",)),
        name="mla_proj",
    )(h, tabs[0], tabs[1], g, w["wa"], w["qn"], w["kvn"],
      w["wuq"], w["wuk"], w["wuv"])


def _fox_proj_body(h_ref, g_ref, w_ref, scale_ref, wf_ref, bf_ref, a_ref, vt_ref, logf_ref):
    xn = _rms(h_ref[...], g_ref[...]).astype(BF16)
    nqk = a_ref.shape[1]
    a_ref[...] = (_dot(xn, w_ref[:, :nqk]) * scale_ref[...]).astype(BF16)
    vt_ref[0] = _dot_tt(w_ref[:, nqk:], xn).astype(BF16)
    f = _dot(xn, wf_ref[...]) + bf_ref[...]
    logf_ref[...] = jnp.minimum(f, 0.0) - jnp.log1p(jnp.exp(-jnp.abs(f)))


def _fox_proj(h, g, w, tm, batch, seq):
    n, d = h.shape
    width = HEADS * HEAD_DIM
    nout = 2 * width
    row = lambda i: (i, 0)
    return pl.pallas_call(
        _fox_proj_body,
        grid=(n // tm,),
        in_specs=[
            pl.BlockSpec((tm, d), row),
            _const_spec((1, d)),
            _const_spec((d, 3 * width)),
            _const_spec((1, nout)),
            _const_spec((d, LANES)),
            _const_spec((1, LANES)),
        ],
        out_specs=[pl.BlockSpec((tm, nout), row), _vt_spec(tm, seq, width),
                   pl.BlockSpec((tm, LANES), row)],
        out_shape=[jax.ShapeDtypeStruct((n, nout), BF16),
                   jax.ShapeDtypeStruct((batch, width, seq), BF16),
                   jax.ShapeDtypeStruct((n, LANES), F32)],
        compiler_params=_params(("parallel",)),
        name="fox_proj",
    )(h, g, w["wqkv"], w["scale"], w["wf"], w["bf"])


def _cumsum_body(x_ref, c_ref, ct_ref, *, seq, blk):
    r = lax.broadcasted_iota(jnp.int32, (blk, blk), 0)
    c = lax.broadcasted_iota(jnp.int32, (blk, blk), 1)
    tri = (c <= r).astype(F32)
    carry = jnp.zeros((1, LANES), F32)
    for b in range(seq // blk):
        xs = x_ref[0, b * blk:(b + 1) * blk, :]
        cs = lax.dot_general(tri, xs, (((1,), (0,)), ((), ())), precision=lax.Precision.HIGHEST,
                             preferred_element_type=F32) + carry
        c_ref[0, b * blk:(b + 1) * blk, :] = cs
        carry = cs[blk - 1:blk, :]
    ct_ref[0] = c_ref[0].T


def _cumsum(logf, batch, seq):
    x = logf.reshape(batch, seq, LANES)
    return pl.pallas_call(
        functools.partial(_cumsum_body, seq=seq, blk=256),
        grid=(batch,),
        in_specs=[pl.BlockSpec((1, seq, LANES), lambda b: (b, 0, 0))],
        out_specs=[pl.BlockSpec((1, seq, LANES), lambda b: (b, 0, 0)),
                   pl.BlockSpec((1, LANES, seq), lambda b: (b, 0, 0))],
        out_shape=[jax.ShapeDtypeStruct((batch, seq, LANES), F32),
                   jax.ShapeDtypeStruct((batch, LANES, seq), F32)],
        compiler_params=_params(("parallel",)),
        name="fox_cumsum",
    )(x)


def _flash_body(*refs, seq, tq, forget, shared_lanes):
    if forget:
        q_ref, k_ref, vt_ref, c_ref, ct_ref, o_ref, va_ref, ck_ref = refs
    else:
        q_ref, k_ref, vt_ref, o_ref, va_ref = refs
    pair = pl.program_id(1)
    lane = lax.broadcasted_iota(jnp.int32, (1, LANES), 1)
    low = lane < HEAD_DIM
    causal = (lax.broadcasted_iota(jnp.int32, (tq, tq), 0)
              <= lax.broadcasted_iota(jnp.int32, (tq, tq), 1))
    for e in range(2):
        va_ref[e, :HEAD_DIM, :] = vt_ref[0, e * HEAD_DIM:(e + 1) * HEAD_DIM, :]
        va_ref[e, HEAD_DIM:, :] = jnp.ones((ONES_ROWS, seq), BF16)
    if forget:
        for e in range(2):
            col = jnp.sum(jnp.where(lane == 2 * pair + e, c_ref[0], 0.0), axis=1, keepdims=True)
            ck_ref[e] = jnp.broadcast_to(col * LOG2E, (seq, tq))

    class Chain:
        pass

    def start(qi, e):
        ch = Chain()
        qs = qi * tq
        ch.qi, ch.e = qi, e
        if shared_lanes:
            qp = q_ref[0, qs:qs + tq, :]
            ch.q = jnp.where(low, qp, 0) if e == 0 else jnp.where(low, 0, qp)
            ch.ksl = slice(0, LANES)
        else:
            ch.ksl = slice(e * LANES, (e + 1) * LANES)
            ch.q = q_ref[0, qs:qs + tq, ch.ksl]
        if forget:
            ch.cq = ct_ref[0, pl.ds(2 * pair + e, 1), qs:qs + tq] * LOG2E
        ch.s, ch.m, ch.acc = [], None, None
        return ch

    def pass1(ch, c):
        ks = c * tq
        s = _dot_nt(k_ref[0, ks:ks + tq, ch.ksl], ch.q)
        if forget:
            s = s + (ch.cq - ck_ref[ch.e, ks:ks + tq, :])
        if c == ch.qi:
            s = jnp.where(causal, s, NEG)
        mc = jnp.max(s, axis=0, keepdims=True)
        ch.m = mc if ch.m is None else jnp.maximum(ch.m, mc)
        ch.s.append(s)

    def pass2(ch, c):
        ks = c * tq
        p = jnp.exp2(ch.s[c] - ch.m)
        ac = _dot(va_ref[ch.e, :, ks:ks + tq], p.astype(BF16))
        ch.acc = ac if ch.acc is None else ch.acc + ac

    outs = []

    def finish(ch):
        outs.append(ch.acc[:HEAD_DIM] / ch.acc[HEAD_DIM:HEAD_DIM + 1])
        if ch.e == 1:
            qs = ch.qi * tq
            o_t = jnp.concatenate(outs, axis=0)
            o_ref[0, qs:qs + tq, :] = o_t.T.astype(o_ref.dtype)
            outs.clear()

    prev = ()
    for qi in range(seq // tq + 1):
        cur = tuple(start(qi, e) for e in range(2)) if qi < seq // tq else ()
        for c in range(qi + 1):
            for ch in cur:
                pass1(ch, c)
            if c < qi:
                for ch in prev:
                    pass2(ch, c)
        for ch in prev:
            finish(ch)
        prev = cur


def _flash(q, k, vt, batch, seq, *, shared_lanes, q_off, k_off, forget=None, tq=256):
    qk_w = LANES if shared_lanes else 2 * LANES
    in_specs = [
        pl.BlockSpec((1, seq, qk_w), lambda b, p: (b, 0, q_off + p)),
        pl.BlockSpec((1, seq, qk_w), lambda b, p: (b, 0, k_off + p)),
        pl.BlockSpec((1, LANES, seq), lambda b, p: (b, p, 0)),
    ]
    args = [q, k, vt]
    scratch = [pltpu.VMEM((2, HEAD_DIM + ONES_ROWS, seq), BF16)]
    if forget is not None:
        c, ct = forget
        in_specs += [pl.BlockSpec((1, seq, LANES), lambda b, p: (b, 0, 0)),
                     pl.BlockSpec((1, HEADS, seq), lambda b, p: (b, 0, 0))]
        args += [c, ct]
        scratch += [pltpu.VMEM((2, seq, tq), F32)]
    return pl.pallas_call(
        functools.partial(_flash_body, seq=seq, tq=tq, forget=forget is not None,
                          shared_lanes=shared_lanes),
        grid=(batch, PAIRS),
        in_specs=in_specs,
        out_specs=pl.BlockSpec((1, seq, LANES), lambda b, p: (b, 0, p)),
        out_shape=jax.ShapeDtypeStruct((batch, seq, HEADS * HEAD_DIM), BF16),
        scratch_shapes=scratch,
        compiler_params=_params(("parallel", "parallel")),
        name="flash_fox" if forget is not None else "flash_mla",
    )(*args)


def _t5_bucket_np(dist):
    max_exact = REL_BUCKETS // 2
    n = np.maximum(dist.astype(np.float32), np.float32(1.0))
    large = max_exact + (np.log(n / np.float32(max_exact)) / np.float32(np.log(REL_MAX_DIST / max_exact))
                         * np.float32(REL_BUCKETS - max_exact)).astype(np.int32)
    large = np.minimum(large, REL_BUCKETS - 1)
    return np.where(dist < max_exact, dist, large).astype(np.int32)


def _dil_bucket_map(dilation, use_prev):
    qry = np.arange(DIL_BLOCK)
    key = np.arange(2 * DIL_BLOCK)
    rel = DIL_BLOCK + qry[None, :] - key[:, None]
    bk = _t5_bucket_np(np.clip(rel, 0, None) * dilation)
    if not use_prev:
        bk = bk[DIL_BLOCK:]
    return np.concatenate([bk, bk], axis=1)


def _dil_body(tab_ref, bucket_ref, q_ref, k_ref, vt_ref, o_ref, lse_ref, bias_ref, *, span, n_blk, seq):
    blk = DIL_BLOCK
    use_prev = n_blk > 1
    nkeys = 2 * blk if use_prev else blk

    @pl.when(pl.program_id(0) == 0)
    def _():
        bk = bucket_ref[...]
        second = lax.broadcasted_iota(jnp.int32, (1, 2 * blk), 1) >= blk
        key = lax.broadcasted_iota(jnp.int32, (nkeys, 2 * blk), 0) + (0 if use_prev else blk)
        qry = lax.broadcasted_iota(jnp.int32, (nkeys, 2 * blk), 1) & (blk - 1)
        rel = blk + qry - key
        band = (rel >= 0) & (rel <= span)

        def fill(p, _):
            acc = jnp.zeros((nkeys, 2 * blk), F32)
            for bb in range(REL_BUCKETS):
                val = jnp.where(second, tab_ref[bb, 2 * p + 1], tab_ref[bb, 2 * p])
                acc = jnp.where(bk == bb, val * LOG2E, acc)
            bias_ref[0, p] = jnp.where(band & (key >= blk), acc, NEG)
            bias_ref[1, p] = jnp.where(band, acc, NEG)
            return 0

        lax.fori_loop(0, PAIRS, fill, 0)

    low = lax.broadcasted_iota(jnp.int32, (1, LANES), 1) < HEAD_DIM
    head_row = lax.broadcasted_iota(jnp.int32, (LANES, blk), 0)

    def block(s, _):
        cur = pl.ds(pl.multiple_of(s * blk, blk), blk)
        cls, rows = s // n_blk, pl.ds(pl.multiple_of((s % n_blk) * blk, blk), blk)
        if use_prev:
            n = s % n_blk
            prv = pl.ds(pl.multiple_of(jnp.where(n > 0, s - 1, s) * blk, blk), blk)
            table = jnp.where(n > 0, 1, 0)
        else:
            table = 1
        def logits(p):
            sl = slice(p * LANES, (p + 1) * LANES)
            qp = q_ref[0, cur, sl]
            qq = jnp.concatenate([jnp.where(low, qp, 0), jnp.where(low, 0, qp)], axis=0)
            if use_prev:
                kk = jnp.concatenate([k_ref[0, prv, sl], k_ref[0, cur, sl]], axis=0)
            else:
                kk = k_ref[0, cur, sl]
            return _dot_nt(kk, qq)

        def masked(p, raw):
            st = raw + bias_ref[table, p]
            return st, jnp.max(st, axis=0, keepdims=True)

        def weighted(p, st, m):
            sl = slice(p * LANES, (p + 1) * LANES)
            if use_prev:
                vv = jnp.concatenate([vt_ref[0, sl, prv], vt_ref[0, sl, cur]], axis=1)
            else:
                vv = vt_ref[0, sl, cur]
            vv = jnp.concatenate([vv, jnp.ones((ONES_ROWS, nkeys), BF16)], axis=0)
            return _dot(vv, jnp.exp2(st - m).astype(BF16))

        def finish(p, ot, m, lse_t):
            sl = slice(p * LANES, (p + 1) * LANES)
            l = ot[LANES:LANES + 1]
            ot = ot[:LANES] / l
            o_t = jnp.concatenate([ot[:HEAD_DIM, :blk], ot[HEAD_DIM:, blk:]], axis=0)
            o_ref[0, cls, rows, sl] = o_t.T.astype(o_ref.dtype)
            lse = m * (1.0 / LOG2E) + jnp.log(l)
            return jnp.where(head_row == 2 * p, lse[:, :blk],
                             jnp.where(head_row == 2 * p + 1, lse[:, blk:], lse_t))

        raws = [logits(p) for p in range(PAIRS)]
        sms = [masked(p, raws[p]) for p in range(PAIRS)]
        ots = [weighted(p, *sms[p]) for p in range(PAIRS)]
        lse_t = jnp.zeros((LANES, blk), F32)
        for p in range(PAIRS):
            lse_t = finish(p, ots[p], sms[p][1], lse_t)
        lse_ref[0, cls, rows, :] = lse_t.T
        return 0

    lax.fori_loop(0, seq // blk, block, 0, unroll=2)


def _dil_attention(qk, vt, table, group, batch, seq):
    window, dilation = DIL_PATTERNS[group]
    span = window // dilation
    sub = seq // dilation
    assert sub % DIL_BLOCK == 0
    n_blk = sub // DIL_BLOCK
    width = HEADS * HEAD_DIM
    bucket = jnp.asarray(_dil_bucket_map(dilation, n_blk > 1))
    return pl.pallas_call(
        functools.partial(_dil_body, span=span, n_blk=n_blk, seq=seq),
        grid=(batch,),
        in_specs=[
            pl.BlockSpec(memory_space=pltpu.SMEM),
            _const_spec(bucket.shape),
            pl.BlockSpec((1, seq, width), lambda b: (b, 0, 0)),
            pl.BlockSpec((1, seq, width), lambda b: (b, 0, 1)),
            pl.BlockSpec((1, width, seq), lambda b: (b, 0, 0)),
        ],
        out_specs=[pl.BlockSpec((1, dilation, sub, width), lambda b: (b, 0, 0, 0)),
                   pl.BlockSpec((1, dilation, sub, LANES), lambda b: (b, 0, 0, 0))],
        out_shape=[jax.ShapeDtypeStruct((batch, dilation, sub, width), BF16),
                   jax.ShapeDtypeStruct((batch, dilation, sub, LANES), F32)],
        scratch_shapes=[pltpu.VMEM((2, PAIRS) + bucket.shape, F32)],
        compiler_params=_params(("arbitrary",)),
        name=f"dil_attn_g{group}",
    )(table, bucket, qk, qk, vt)


def _oproj_merge_body(o0_ref, o1_ref, o2_ref, l0_ref, l1_ref, l2_ref, spread_ref, w_ref, h_ref, g_ref,
                      out_ref, lse_ref, slab_ref, merged_ref):
    o_refs = (o0_ref, o1_ref, o2_ref)
    l_refs = (l0_ref, l1_ref, l2_ref)
    n_groups = len(DIL_PATTERNS)
    tm = h_ref.shape[0]
    lses = []
    for gi in range(n_groups):
        dil = DIL_PATTERNS[gi][1]
        if dil == 1:
            lses.append(l_refs[gi][0, 0])
            continue
        for r in range(dil):
            rows = pl.ds(r, tm // dil, stride=dil)
            lse_ref[gi, rows, :] = l_refs[gi][0, r]
            for s in range(PAIRS):
                slab_ref[gi, s, rows, :] = o_refs[gi][0, r, :, s * LANES:(s + 1) * LANES].astype(F32)
        lses.append(lse_ref[gi])
    mx = functools.reduce(jnp.maximum, lses)
    ex = [jnp.exp(t - mx) for t in lses]
    den = functools.reduce(jnp.add, ex)
    wide = []
    for t in ex:
        a = t / den
        a_hi = a.astype(BF16)
        a_lo = (a - a_hi.astype(F32)).astype(BF16)
        wide.append(_dot(a_hi, spread_ref[...]) + _dot(a_lo, spread_ref[...]))
    for p in range(PAIRS):
        sl = slice(p * LANES, (p + 1) * LANES)
        acc = jnp.zeros((tm, LANES), F32)
        for gi in range(n_groups):
            if DIL_PATTERNS[gi][1] == 1:
                acc = acc + wide[gi][:, sl] * o_refs[gi][0, 0, :, sl].astype(F32)
            else:
                acc = acc + wide[gi][:, sl] * slab_ref[gi, p]
        merged_ref[:, sl] = acc.astype(BF16)
    out_ref[...] = h_ref[...] + _rms(_dot(merged_ref[...], w_ref[...]), g_ref[...])


def _oproj_merge(os_, lses, w, h, g, tm, batch, seq):
    n, d = h.shape
    row = lambda i: (i, 0)
    width = os_[0].shape[-1]
    per = seq // tm
    n_groups = len(DIL_PATTERNS)

    def grouped_spec(dil, cols):
        return pl.BlockSpec((1, dil, tm // dil, cols), lambda i: (i // per, 0, i % per, 0))

    dils = [dil for _, dil in DIL_PATTERNS]
    spread = np.zeros((LANES, width), np.float32)
    for hh in range(HEADS):
        spread[hh, hh * HEAD_DIM:(hh + 1) * HEAD_DIM] = 1.0
    spread = jnp.asarray(spread, BF16)
    return pl.pallas_call(
        _oproj_merge_body,
        grid=(n // tm,),
        in_specs=[grouped_spec(dil, width) for dil in dils] + [grouped_spec(dil, LANES) for dil in dils]
        + [_const_spec(spread.shape), _const_spec(w.shape), pl.BlockSpec((tm, d), row), _const_spec((1, d))],
        out_specs=pl.BlockSpec((tm, d), row),
        out_shape=jax.ShapeDtypeStruct((n, d), F32),
        scratch_shapes=[pltpu.VMEM((n_groups, tm, LANES), F32),
                        pltpu.VMEM((n_groups, PAIRS, tm, LANES), F32),
                        pltpu.VMEM((tm, width), BF16)],
        compiler_params=_params(("parallel",)),
        name="oproj_merge",
    )(*os_, *lses, spread, w, h, g)


MXU_TILE = 256
FFN_SPLITS = (0, 6 * MXU_TILE, D_FF)


def _ffn_body(*refs, with_oproj):
    if with_oproj:
        o_ref, wattn_ref, g1_ref, *refs = refs
    h_ref, p_ref, g2_ref, g3_ref, win_ref, wo_ref, wproj_ref, wgate_ref, out_ref = refs
    h = h_ref[...]
    if with_oproj:
        h = h + _rms(_dot(o_ref[...], wattn_ref[0]), g1_ref[...])
    xn = _rms(h, g2_ref[...]).astype(BF16)
    y = None
    for a, b in zip(FFN_SPLITS[:-1], FFN_SPLITS[1:]):
        gate = _dot(xn, win_ref[0, :, a:b])
        up = _dot(xn, win_ref[0, :, D_FF + a:D_FF + b])
        act = (gate * jax.nn.sigmoid(gate) * up).astype(BF16)
        part = _dot(act, wo_ref[0, a:b, :])
        y = part if y is None else y + part
    h2 = h + _rms(y, g3_ref[...])
    emb = _dot(p_ref[0].astype(BF16), wproj_ref[0])
    out_ref[...] = h2 + emb * jax.nn.sigmoid(_dot(h2.astype(BF16), wgate_ref[0]))


def _layer_spec(stacked, layer):
    nd = stacked.ndim - 1
    return pl.BlockSpec((1,) + stacked.shape[1:], lambda *_: (layer,) + (0,) * nd,
                        pipeline_mode=pl.Buffered(1))


def _ffn(h, p, layer, g2, g3, w_in, w_out, w_proj, w_gate, tm, oproj=None):
    n, d = h.shape
    row = lambda i: (i, 0)
    head_specs, head_args = [], []
    if oproj is not None:
        o, w_attn, attn_layer, g1 = oproj
        head_specs = [pl.BlockSpec((tm, o.shape[1]), row), _layer_spec(w_attn, attn_layer),
                      _const_spec((1, d))]
        head_args = [o, w_attn, g1]
    return pl.pallas_call(
        functools.partial(_ffn_body, with_oproj=oproj is not None),
        grid=(n // tm,),
        in_specs=head_specs + [
            pl.BlockSpec((tm, d), row),
            pl.BlockSpec((1, tm, D_PLE), lambda i: (layer, i, 0)),
            _const_spec((1, d)),
            _const_spec((1, d)),
            _layer_spec(w_in, layer),
            _layer_spec(w_out, layer),
            _layer_spec(w_proj, layer),
            _layer_spec(w_gate, layer),
        ],
        out_specs=pl.BlockSpec((tm, d), row),
        out_shape=jax.ShapeDtypeStruct((n, d), F32),
        compiler_params=_params(("parallel",)),
        name="ffn",
    )(*head_args, h, p, g2, g3, w_in, w_out, w_proj, w_gate)


def _mla_weights(w_a, q_norm, kv_norm, w_uq, w_ukv):
    rank = MLA_Q_RANK + MLA_KV_RANK
    wa = jnp.pad(w_a, ((0, 0), (0, LANES - MLA_ROPE)))
    wa = jnp.concatenate([wa[:, :rank], jnp.roll(wa[:, rank:], MLA_NOPE, axis=1)], axis=1).astype(BF16)
    uq = w_uq.reshape(MLA_Q_RANK, HEADS, MLA_NOPE + MLA_ROPE)
    wuq = jnp.pad(uq, ((0, 0), (0, 0), (0, LANES - MLA_NOPE - MLA_ROPE)))
    wuq = wuq.reshape(MLA_Q_RANK, HEADS * LANES).astype(BF16)
    ukv = w_ukv.reshape(MLA_KV_RANK, HEADS, MLA_NOPE + HEAD_DIM)
    wuk = jnp.pad(ukv[:, :, :MLA_NOPE], ((0, 0), (0, 0), (0, LANES - MLA_NOPE)))
    wuk = wuk.reshape(MLA_KV_RANK, HEADS * LANES).astype(BF16)
    wuv = ukv[:, :, MLA_NOPE:].reshape(MLA_KV_RANK, HEADS * HEAD_DIM).astype(BF16)
    return dict(wa=wa, qn=q_norm.reshape(1, -1), kvn=kv_norm.reshape(1, -1),
                wuq=wuq, wuk=wuk, wuv=wuv)


def _q_scale_row(n_cols, q_starts, width, scale):
    row = np.ones((1, n_cols), np.float32)
    for s in q_starts:
        row[0, s:s + width] = scale
    return jnp.asarray(row)


def kernel(x, p, positions, norm_g, ffn_w_in, ffn_w_out, ple_w_proj, ple_w_gate, rel_bias, mla_w_a, mla_q_norm, mla_kv_norm, mla_w_uq, mla_w_ukv, mla_w_o, dil_w_qkv, dil_w_o, fox_w_qkvf, fox_b_f, fox_w_o):
    batch, seq, d = x.shape
    n = batch * seq
    inner = HEADS * HEAD_DIM
    h = x.reshape(n, d)
    rope_tabs = _rope_tables(positions, 512)
    w_in, w_out = ffn_w_in.astype(BF16), ffn_w_out.astype(BF16)
    w_proj, w_gate = ple_w_proj.astype(BF16), ple_w_gate.astype(BF16)
    mla_wo, fox_wo = mla_w_o.astype(BF16), fox_w_o.astype(BF16)
    p_rows = p.reshape(N_LAYERS, n, D_PLE)
    for i in range(N_LAYERS):
        mixer, j = i % N_MIXERS, i // N_MIXERS
        g = norm_g[i].reshape(4, 1, d)
        if mixer == 0:
            w = _mla_weights(mla_w_a[j], mla_q_norm[j], mla_kv_norm[j], mla_w_uq[j], mla_w_ukv[j])
            q, k, vt = _mla_proj(h, rope_tabs, g[0], w, 512, batch, seq)
            o = _flash(q.reshape(batch, seq, -1), k.reshape(batch, seq, -1), vt, batch, seq,
                       shared_lanes=False, q_off=0, k_off=0, tq=512)
            oproj = (o.reshape(n, inner), mla_wo, j, g[1])
        elif mixer == 1:
            n_cols = dil_w_qkv.shape[-1]
            scale = _q_scale_row(n_cols, [gi * 3 * inner for gi in range(len(DIL_PATTERNS))], inner,
                                 HEAD_DIM ** -0.5 * LOG2E)
            wq = dil_w_qkv[j].astype(BF16)
            table = rel_bias.reshape(REL_BUCKETS, len(DIL_PATTERNS), HEADS)
            h3 = h.reshape(batch, seq, d)
            outs, lses = [], []
            for gi in range(len(DIL_PATTERNS)):
                qk, vt = _dil_proj(h3, g[0], wq, scale, gi)
                o, lse = _dil_attention(qk, vt, table[:, gi], gi, batch, seq)
                outs.append(o)
                lses.append(lse)
            h = _oproj_merge(outs, lses, dil_w_o[j].astype(BF16), h, g[1], 512, batch, seq)
            oproj = None
        else:
            wq = fox_w_qkvf[j]
            w = dict(
                wqkv=wq[:, :3 * inner].astype(BF16),
                scale=_q_scale_row(2 * inner, [0], inner, HEAD_DIM ** -0.5 * LOG2E),
                wf=jnp.pad(wq[:, 3 * inner:], ((0, 0), (0, LANES - HEADS))).astype(BF16),
                bf=jnp.pad(fox_b_f[j], (0, LANES - HEADS)).reshape(1, LANES),
            )
            a, vt, logf = _fox_proj(h, g[0], w, 512, batch, seq)
            c, ct = _cumsum(logf, batch, seq)
            a3 = a.reshape(batch, seq, 2 * inner)
            o = _flash(a3, a3, vt, batch, seq, shared_lanes=True, q_off=0, k_off=PAIRS, forget=(c, ct))
            oproj = (o.reshape(n, inner), fox_wo, j, g[1])
        h = _ffn(h, p_rows, i, g[2], g[3], w_in, w_out, w_proj, w_gate, tm=512, oproj=oproj)
    return h.reshape(batch, seq, d)
```

```python
import functools

import numpy as np
import jax
import jax.numpy as jnp
from jax import lax
from jax.experimental import pallas as pl
from jax.experimental.pallas import tpu as pltpu

F32 = jnp.float32
BF16 = jnp.bfloat16

D_MODEL = 1024
N_LAYERS = 4
N_MIXERS = 3
D_PLE = 256
EPS = 1e-6
NEG = -1e30
D_FF = 2816

HEADS = 16
HEAD_DIM = 64
LANES = 128
PAIRS = HEADS // 2
ONES_ROWS = 16
LOG2E = 1.4426950408889634

MLA_Q_RANK = 384
MLA_KV_RANK = 256
MLA_NOPE = 64
MLA_ROPE = 32
ROPE_HALF = MLA_ROPE // 2
ROPE_THETA = 10000.0

DIL_PATTERNS = ((128, 1), (512, 4), (2048, 16))
DIL_BLOCK = 128
REL_BUCKETS = 32
REL_MAX_DIST = 2048

VMEM_LIMIT = 56 * 1024 * 1024
ROW_TILE = 512


def _params(sem):
    return pltpu.CompilerParams(dimension_semantics=sem, vmem_limit_bytes=VMEM_LIMIT)


def _rms(x, g):
    y = x * lax.rsqrt(jnp.mean(x * x, axis=-1, keepdims=True) + EPS)
    return y * g


def _dot(a, b):
    return jnp.dot(a, b, preferred_element_type=F32)


def _dot_nt(a, b):
    return lax.dot_general(a, b, (((1,), (1,)), ((), ())), preferred_element_type=F32)


def _dot_tt(a, b):
    return lax.dot_general(a, b, (((0,), (1,)), ((), ())), preferred_element_type=F32)


def _const_spec(shape):
    nd = len(shape)
    return pl.BlockSpec(shape, lambda *_: (0,) * nd)


ROW_CHUNK = 256


def _dil_proj_body(h_ref, g_ref, w_ref, q_ref, k_ref, vt_ref, slab_ref, perm_ref, *, seq):
    j = pl.program_id(1)
    n_slab = D_MODEL // LANES

    def regroup(dilation):
        part = slab_ref.shape[1]
        for h0 in range(0, seq, part):
            def norm_rows(i, _):
                rows = pl.multiple_of(i * ROW_CHUNK, ROW_CHUNK)
                xn = _rms(h_ref[0, pl.ds(h0 + rows, ROW_CHUNK), :], g_ref[...])
                if dilation == 1:
                    perm_ref[pl.ds(h0 + rows, ROW_CHUNK), :] = xn.astype(BF16)
                else:
                    for s in range(n_slab):
                        slab_ref[s, pl.ds(rows, ROW_CHUNK), :] = xn[:, s * LANES:(s + 1) * LANES]
                return 0

            lax.fori_loop(0, part // ROW_CHUNK, norm_rows, 0)
            if dilation > 1:
                sub, cnt = seq // dilation, part // dilation
                for r in range(dilation):
                    dst = r * sub + h0 // dilation
                    for s in range(n_slab):
                        perm_ref[dst:dst + cnt, s * LANES:(s + 1) * LANES] = (
                            slab_ref[s, pl.ds(r, cnt, stride=dilation), :].astype(BF16))

    for gi, (_, dilation) in enumerate(DIL_PATTERNS):
        pl.when(j == 3 * gi)(functools.partial(regroup, dilation))

    @pl.when(j % 3 == 0)
    def _():
        q_ref[0, 0] = (_dot(perm_ref[...], w_ref[...]) * (HEAD_DIM ** -0.5 * LOG2E)).astype(BF16)

    @pl.when(j % 3 == 1)
    def _():
        k_ref[0, 0] = _dot(perm_ref[...], w_ref[...]).astype(BF16)

    @pl.when(j % 3 == 2)
    def _():
        vt_ref[0, 0] = _dot_tt(w_ref[...], perm_ref[...]).astype(BF16)


def _dil_proj(h3, g, w):
    batch, seq, d = h3.shape
    n_groups = len(DIL_PATTERNS)
    width = HEADS * HEAD_DIM
    grouped = lambda b, j: (j // 3, b, 0, 0)
    return pl.pallas_call(
        functools.partial(_dil_proj_body, seq=seq),
        grid=(batch, 3 * n_groups),
        in_specs=[
            pl.BlockSpec((1, seq, d), lambda b, j: (b, 0, 0), pipeline_mode=pl.Buffered(1)),
            pl.BlockSpec((1, d), lambda b, j: (0, 0)),
            pl.BlockSpec((d, width), lambda b, j: (0, j)),
        ],
        out_specs=[
            pl.BlockSpec((1, 1, seq, width), grouped),
            pl.BlockSpec((1, 1, seq, width), grouped),
            pl.BlockSpec((1, 1, width, seq), grouped),
        ],
        out_shape=[jax.ShapeDtypeStruct((n_groups, batch, seq, width), BF16),
                   jax.ShapeDtypeStruct((n_groups, batch, seq, width), BF16),
                   jax.ShapeDtypeStruct((n_groups, batch, width, seq), BF16)],
        scratch_shapes=[pltpu.VMEM((d // LANES, seq // 2, LANES), F32), pltpu.VMEM((seq, d), BF16)],
        compiler_params=_params(("arbitrary", "arbitrary")),
        name="dil_proj",
    )(h3, g, w)


def _rope_table_body(pos_ref, inv_ref, c_ref, s_ref):
    tm = pos_ref.shape[-1]
    ang = inv_ref[...] * pos_ref[0].astype(F32)
    cos, sin = jnp.cos(ang), jnp.sin(ang)
    pad = LANES - MLA_NOPE - MLA_ROPE
    c_t = jnp.concatenate([jnp.ones((MLA_NOPE, tm), F32), cos, jnp.zeros((pad, tm), F32)], axis=0)
    s_t = jnp.concatenate([jnp.zeros((MLA_NOPE, tm), F32), -sin[:ROPE_HALF], sin[ROPE_HALF:],
                           jnp.zeros((pad, tm), F32)], axis=0)
    c_ref[...] = c_t.T
    s_ref[...] = s_t.T


def _rope_tables(positions, tm):
    n = positions.size
    inv = ROPE_THETA ** (-jnp.arange(ROPE_HALF, dtype=F32) / ROPE_HALF)
    inv_col = jnp.concatenate([inv, inv]).reshape(MLA_ROPE, 1)
    return pl.pallas_call(
        _rope_table_body,
        grid=(n // tm,),
        in_specs=[pl.BlockSpec((1, 1, tm), lambda i: (i, 0, 0)), _const_spec((MLA_ROPE, 1))],
        out_specs=[pl.BlockSpec((tm, LANES), lambda i: (i, 0))] * 2,
        out_shape=[jax.ShapeDtypeStruct((n, LANES), F32)] * 2,
        compiler_params=_params(("parallel",)),
        name="rope_tables",
    )(positions.reshape(n // tm, 1, tm), inv_col)


def _mla_proj_body(h_ref, ctab_ref, stab_ref, g_ref, wa_ref, qn_ref, kvn_ref,
                   wuq_ref, wuk_ref, wuv_ref, q_ref, k_ref, vt_ref):
    xn = _rms(h_ref[...], g_ref[...]).astype(BF16)
    a = _dot(xn, wa_ref[...])
    cq = _rms(a[:, :MLA_Q_RANK], qn_ref[...]).astype(BF16)
    ckv = _rms(a[:, MLA_Q_RANK:MLA_Q_RANK + MLA_KV_RANK], kvn_ref[...]).astype(BF16)
    kr = a[:, MLA_Q_RANK + MLA_KV_RANK:]
    q = _dot(cq, wuq_ref[...])
    kn = _dot(ckv, wuk_ref[...])
    vt_ref[0] = _dot_tt(wuv_ref[...], ckv).astype(BF16)

    lane = lax.broadcasted_iota(jnp.int32, (1, LANES), 1)
    first = (lane >= MLA_NOPE) & (lane < MLA_NOPE + ROPE_HALF)
    c_tab, s_tab = ctab_ref[...], stab_ref[...]

    def rope(t):
        other = jnp.where(first, pltpu.roll(t, LANES - ROPE_HALF, 1), pltpu.roll(t, ROPE_HALF, 1))
        return t * c_tab + other * s_tab

    kr = rope(kr)
    scale = (MLA_NOPE + MLA_ROPE) ** -0.5 * LOG2E
    for hh in range(HEADS):
        sl = slice(hh * LANES, (hh + 1) * LANES)
        q_ref[:, sl] = (rope(q[:, sl]) * scale).astype(BF16)
        k_ref[:, sl] = (kn[:, sl] + kr).astype(BF16)


def _vt_spec(tm, seq, width):
    per = seq // tm
    return pl.BlockSpec((1, width, tm), lambda i: (i // per, 0, i % per))


def _mla_proj(h, tabs, g, w, tm, batch, seq):
    n, d = h.shape
    row = lambda i: (i, 0)
    width = HEADS * HEAD_DIM
    return pl.pallas_call(
        _mla_proj_body,
        grid=(n // tm,),
        in_specs=[
            pl.BlockSpec((tm, d), row),
            pl.BlockSpec((tm, LANES), row),
            pl.BlockSpec((tm, LANES), row),
            _const_spec((1, d)),
            _const_spec(w["wa"].shape),
            _const_spec((1, MLA_Q_RANK)), _const_spec((1, MLA_KV_RANK)),
            _const_spec(w["wuq"].shape), _const_spec(w["wuk"].shape), _const_spec(w["wuv"].shape),
        ],
        out_specs=[
            pl.BlockSpec((tm, HEADS * LANES), row),
            pl.BlockSpec((tm, HEADS * LANES), row),
            _vt_spec(tm, seq, width),
        ],
        out_shape=[
            jax.ShapeDtypeStruct((n, HEADS * LANES), BF16),
            jax.ShapeDtypeStruct((n, HEADS * LANES), BF16),
            jax.ShapeDtypeStruct((batch, width, seq), BF16),
        ],
        compiler_params=_params(("parallel",)),
        name="mla_proj",
    )(h, tabs[0], tabs[1], g, w["wa"], w["qn"], w["kvn"],
      w["wuq"], w["wuk"], w["wuv"])


def _fox_proj_body(h_ref, g_ref, w_ref, scale_ref, wf_ref, bf_ref, a_ref, vt_ref, logf_ref):
    xn = _rms(h_ref[...], g_ref[...]).astype(BF16)
    nqk = a_ref.shape[1]
    a_ref[...] = (_dot(xn, w_ref[:, :nqk]) * scale_ref[...]).astype(BF16)
    vt_ref[0] = _dot_tt(w_ref[:, nqk:], xn).astype(BF16)
    f = _dot(xn, wf_ref[...]) + bf_ref[...]
    logf_ref[...] = jnp.minimum(f, 0.0) - jnp.log1p(jnp.exp(-jnp.abs(f)))


def _fox_proj(h, g, w, tm, batch, seq):
    n, d = h.shape
    width = HEADS * HEAD_DIM
    nout = 2 * width
    row = lambda i: (i, 0)
    return pl.pallas_call(
        _fox_proj_body,
        grid=(n // tm,),
        in_specs=[
            pl.BlockSpec((tm, d), row),
            _const_spec((1, d)),
            _const_spec((d, 3 * width)),
            _const_spec((1, nout)),
            _const_spec((d, LANES)),
            _const_spec((1, LANES)),
        ],
        out_specs=[pl.BlockSpec((tm, nout), row), _vt_spec(tm, seq, width),
                   pl.BlockSpec((tm, LANES), row)],
        out_shape=[jax.ShapeDtypeStruct((n, nout), BF16),
                   jax.ShapeDtypeStruct((batch, width, seq), BF16),
                   jax.ShapeDtypeStruct((n, LANES), F32)],
        compiler_params=_params(("parallel",)),
        name="fox_proj",
    )(h, g, w["wqkv"], w["scale"], w["wf"], w["bf"])


def _cumsum_body(x_ref, c_ref, ct_ref, *, seq, blk):
    r = lax.broadcasted_iota(jnp.int32, (blk, blk), 0)
    c = lax.broadcasted_iota(jnp.int32, (blk, blk), 1)
    tri = (c <= r).astype(F32)
    carry = jnp.zeros((1, LANES), F32)
    for b in range(seq // blk):
        xs = x_ref[0, b * blk:(b + 1) * blk, :]
        cs = lax.dot_general(tri, xs, (((1,), (0,)), ((), ())), precision=lax.Precision.HIGHEST,
                             preferred_element_type=F32) + carry
        c_ref[0, b * blk:(b + 1) * blk, :] = cs
        carry = cs[blk - 1:blk, :]
    ct_ref[0] = c_ref[0].T


def _cumsum(logf, batch, seq):
    x = logf.reshape(batch, seq, LANES)
    return pl.pallas_call(
        functools.partial(_cumsum_body, seq=seq, blk=256),
        grid=(batch,),
        in_specs=[pl.BlockSpec((1, seq, LANES), lambda b: (b, 0, 0))],
        out_specs=[pl.BlockSpec((1, seq, LANES), lambda b: (b, 0, 0)),
                   pl.BlockSpec((1, LANES, seq), lambda b: (b, 0, 0))],
        out_shape=[jax.ShapeDtypeStruct((batch, seq, LANES), F32),
                   jax.ShapeDtypeStruct((batch, LANES, seq), F32)],
        compiler_params=_params(("parallel",)),
        name="fox_cumsum",
    )(x)


def _flash_body(*refs, seq, tq, forget, shared_lanes):
    if forget:
        q_ref, k_ref, vt_ref, c_ref, ct_ref, o_ref, va_ref, ck_ref = refs
    else:
        q_ref, k_ref, vt_ref, o_ref, va_ref = refs
    pair = pl.program_id(1)
    lane = lax.broadcasted_iota(jnp.int32, (1, LANES), 1)
    low = lane < HEAD_DIM
    causal = (lax.broadcasted_iota(jnp.int32, (tq, tq), 0)
              <= lax.broadcasted_iota(jnp.int32, (tq, tq), 1))
    for e in range(2):
        va_ref[e, :HEAD_DIM, :] = vt_ref[0, e * HEAD_DIM:(e + 1) * HEAD_DIM, :]
        va_ref[e, HEAD_DIM:, :] = jnp.ones((ONES_ROWS, seq), BF16)
    if forget:
        for e in range(2):
            col = jnp.sum(jnp.where(lane == 2 * pair + e, c_ref[0], 0.0), axis=1, keepdims=True)
            ck_ref[e] = jnp.broadcast_to(col * LOG2E, (seq, tq))

    class Chain:
        pass

    def start(qi, e):
        ch = Chain()
        qs = qi * tq
        ch.qi, ch.e = qi, e
        if shared_lanes:
            qp = q_ref[0, qs:qs + tq, :]
            ch.q = jnp.where(low, qp, 0) if e == 0 else jnp.where(low, 0, qp)
            ch.ksl = slice(0, LANES)
        else:
            ch.ksl = slice(e * LANES, (e + 1) * LANES)
            ch.q = q_ref[0, qs:qs + tq, ch.ksl]
        if forget:
            ch.cq = ct_ref[0, pl.ds(2 * pair + e, 1), qs:qs + tq] * LOG2E
        ch.s, ch.m, ch.acc = [], None, None
        return ch

    def pass1(ch, c):
        ks = c * tq
        s = _dot_nt(k_ref[0, ks:ks + tq, ch.ksl], ch.q)
        if forget:
            s = s + (ch.cq - ck_ref[ch.e, ks:ks + tq, :])
        if c == ch.qi:
            s = jnp.where(causal, s, NEG)
        mc = jnp.max(s, axis=0, keepdims=True)
        ch.m = mc if ch.m is None else jnp.maximum(ch.m, mc)
        ch.s.append(s)

    def pass2(ch, c):
        ks = c * tq
        p = jnp.exp2(ch.s[c] - ch.m)
        ac = _dot(va_ref[ch.e, :, ks:ks + tq], p.astype(BF16))
        ch.acc = ac if ch.acc is None else ch.acc + ac

    outs = []

    def finish(ch):
        outs.append(ch.acc[:HEAD_DIM] / ch.acc[HEAD_DIM:HEAD_DIM + 1])
        if ch.e == 1:
            qs = ch.qi * tq
            o_t = jnp.concatenate(outs, axis=0)
            o_ref[0, qs:qs + tq, :] = o_t.T.astype(o_ref.dtype)
            outs.clear()

    prev = ()
    for qi in range(seq // tq + 1):
        cur = tuple(start(qi, e) for e in range(2)) if qi < seq // tq else ()
        for c in range(qi + 1):
            for ch in cur:
                pass1(ch, c)
            if c < qi:
                for ch in prev:
                    pass2(ch, c)
        for ch in prev:
            finish(ch)
        prev = cur


def _flash(q, k, vt, batch, seq, *, shared_lanes, k_off, forget=None, tq=256):
    qk_w = LANES if shared_lanes else 2 * LANES
    in_specs = [
        pl.BlockSpec((1, seq, qk_w), lambda b, p: (b, 0, p)),
        pl.BlockSpec((1, seq, qk_w), lambda b, p: (b, 0, k_off + p)),
        pl.BlockSpec((1, LANES, seq), lambda b, p: (b, p, 0)),
    ]
    args = [q, k, vt]
    scratch = [pltpu.VMEM((2, HEAD_DIM + ONES_ROWS, seq), BF16)]
    if forget is not None:
        c, ct = forget
        in_specs += [pl.BlockSpec((1, seq, LANES), lambda b, p: (b, 0, 0)),
                     pl.BlockSpec((1, HEADS, seq), lambda b, p: (b, 0, 0))]
        args += [c, ct]
        scratch += [pltpu.VMEM((2, seq, tq), F32)]
    return pl.pallas_call(
        functools.partial(_flash_body, seq=seq, tq=tq, forget=forget is not None,
                          shared_lanes=shared_lanes),
        grid=(batch, PAIRS),
        in_specs=in_specs,
        out_specs=pl.BlockSpec((1, seq, LANES), lambda b, p: (b, 0, p)),
        out_shape=jax.ShapeDtypeStruct((batch, seq, HEADS * HEAD_DIM), BF16),
        scratch_shapes=scratch,
        compiler_params=_params(("parallel", "parallel")),
        name="flash_fox" if forget is not None else "flash_mla",
    )(*args)


def _t5_bucket_np(dist):
    max_exact = REL_BUCKETS // 2
    n = np.maximum(dist.astype(np.float32), np.float32(1.0))
    large = max_exact + (np.log(n / np.float32(max_exact)) / np.float32(np.log(REL_MAX_DIST / max_exact))
                         * np.float32(REL_BUCKETS - max_exact)).astype(np.int32)
    large = np.minimum(large, REL_BUCKETS - 1)
    return np.where(dist < max_exact, dist, large).astype(np.int32)


def _dil_bucket_map(dilation, use_prev):
    qry = np.arange(DIL_BLOCK)
    key = np.arange(2 * DIL_BLOCK)
    rel = DIL_BLOCK + qry[None, :] - key[:, None]
    bk = _t5_bucket_np(np.clip(rel, 0, None) * dilation)
    if not use_prev:
        bk = bk[DIL_BLOCK:]
    return np.concatenate([bk, bk], axis=1)


def _dil_body(tab_ref, bucket_ref, q_ref, k_ref, vt_ref, o_ref, lse_ref, bias_ref, *, span, n_blk, seq):
    blk = DIL_BLOCK
    use_prev = n_blk > 1
    nkeys = 2 * blk if use_prev else blk

    @pl.when(pl.program_id(0) == 0)
    def _():
        bk = bucket_ref[...]
        second = lax.broadcasted_iota(jnp.int32, (1, 2 * blk), 1) >= blk
        key = lax.broadcasted_iota(jnp.int32, (nkeys, 2 * blk), 0) + (0 if use_prev else blk)
        qry = lax.broadcasted_iota(jnp.int32, (nkeys, 2 * blk), 1) & (blk - 1)
        rel = blk + qry - key
        band = (rel >= 0) & (rel <= span)

        def fill(p, _):
            acc = jnp.zeros((nkeys, 2 * blk), F32)
            for bb in range(REL_BUCKETS):
                val = jnp.where(second, tab_ref[bb, 2 * p + 1], tab_ref[bb, 2 * p])
                acc = jnp.where(bk == bb, val * LOG2E, acc)
            bias_ref[0, p] = jnp.where(band & (key >= blk), acc, NEG)
            bias_ref[1, p] = jnp.where(band, acc, NEG)
            return 0

        lax.fori_loop(0, PAIRS, fill, 0)

    low = lax.broadcasted_iota(jnp.int32, (1, LANES), 1) < HEAD_DIM
    head_row = lax.broadcasted_iota(jnp.int32, (LANES, blk), 0)

    def block(s, _):
        cur = pl.ds(pl.multiple_of(s * blk, blk), blk)
        cls, rows = s // n_blk, pl.ds(pl.multiple_of((s % n_blk) * blk, blk), blk)
        if use_prev:
            n = s % n_blk
            prv = pl.ds(pl.multiple_of(jnp.where(n > 0, s - 1, s) * blk, blk), blk)
            table = jnp.where(n > 0, 1, 0)
        else:
            table = 1
        def logits(p):
            sl = slice(p * LANES, (p + 1) * LANES)
            qp = q_ref[0, cur, sl]
            qq = jnp.concatenate([jnp.where(low, qp, 0), jnp.where(low, 0, qp)], axis=0)
            if use_prev:
                kk = jnp.concatenate([k_ref[0, prv, sl], k_ref[0, cur, sl]], axis=0)
            else:
                kk = k_ref[0, cur, sl]
            return _dot_nt(kk, qq)

        def masked(p, raw):
            st = raw + bias_ref[table, p]
            return st, jnp.max(st, axis=0, keepdims=True)

        def weighted(p, st, m):
            sl = slice(p * LANES, (p + 1) * LANES)
            if use_prev:
                vv = jnp.concatenate([vt_ref[0, sl, prv], vt_ref[0, sl, cur]], axis=1)
            else:
                vv = vt_ref[0, sl, cur]
            vv = jnp.concatenate([vv, jnp.ones((ONES_ROWS, nkeys), BF16)], axis=0)
            return _dot(vv, jnp.exp2(st - m).astype(BF16))

        def finish(p, ot, m, lse_t):
            sl = slice(p * LANES, (p + 1) * LANES)
            l = ot[LANES:LANES + 1]
            ot = ot[:LANES] / l
            o_t = jnp.concatenate([ot[:HEAD_DIM, :blk], ot[HEAD_DIM:, blk:]], axis=0)
            o_ref[0, cls, rows, sl] = o_t.T.astype(o_ref.dtype)
            lse = m * (1.0 / LOG2E) + jnp.log(l)
            return jnp.where(head_row == 2 * p, lse[:, :blk],
                             jnp.where(head_row == 2 * p + 1, lse[:, blk:], lse_t))

        raws = [logits(p) for p in range(PAIRS)]
        sms = [masked(p, raws[p]) for p in range(PAIRS)]
        ots = [weighted(p, *sms[p]) for p in range(PAIRS)]
        lse_t = jnp.zeros((LANES, blk), F32)
        for p in range(PAIRS):
            lse_t = finish(p, ots[p], sms[p][1], lse_t)
        lse_ref[0, cls, rows, :] = lse_t.T
        return 0

    lax.fori_loop(0, seq // blk, block, 0, unroll=2)


def _dil_attention(q, k, vt, table, group, batch, seq):
    window, dilation = DIL_PATTERNS[group]
    span = window // dilation
    sub = seq // dilation
    assert sub % DIL_BLOCK == 0
    n_blk = sub // DIL_BLOCK
    width = HEADS * HEAD_DIM
    bucket = jnp.asarray(_dil_bucket_map(dilation, n_blk > 1))
    return pl.pallas_call(
        functools.partial(_dil_body, span=span, n_blk=n_blk, seq=seq),
        grid=(batch,),
        in_specs=[
            pl.BlockSpec(memory_space=pltpu.SMEM),
            _const_spec(bucket.shape),
            pl.BlockSpec((None, 1, seq, width), lambda b: (group, b, 0, 0)),
            pl.BlockSpec((None, 1, seq, width), lambda b: (group, b, 0, 0)),
            pl.BlockSpec((None, 1, width, seq), lambda b: (group, b, 0, 0)),
        ],
        out_specs=[pl.BlockSpec((1, dilation, sub, width), lambda b: (b, 0, 0, 0)),
                   pl.BlockSpec((1, dilation, sub, LANES), lambda b: (b, 0, 0, 0))],
        out_shape=[jax.ShapeDtypeStruct((batch, dilation, sub, width), BF16),
                   jax.ShapeDtypeStruct((batch, dilation, sub, LANES), F32)],
        scratch_shapes=[pltpu.VMEM((2, PAIRS) + bucket.shape, F32)],
        compiler_params=_params(("arbitrary",)),
        name=f"dil_attn_g{group}",
    )(table, bucket, q, k, vt)


def _oproj_merge_body(o0_ref, o1_ref, o2_ref, l0_ref, l1_ref, l2_ref, spread_ref, w_ref, h_ref, g_ref,
                      out_ref, lse_ref, slab_ref, merged_ref):
    o_refs = (o0_ref, o1_ref, o2_ref)
    l_refs = (l0_ref, l1_ref, l2_ref)
    n_groups = len(DIL_PATTERNS)
    tm = h_ref.shape[0]
    lses = []
    for gi in range(n_groups):
        dil = DIL_PATTERNS[gi][1]
        if dil == 1:
            lses.append(l_refs[gi][0, 0])
            continue
        for r in range(dil):
            rows = pl.ds(r, tm // dil, stride=dil)
            lse_ref[gi, rows, :] = l_refs[gi][0, r]
            for s in range(PAIRS):
                slab_ref[gi, s, rows, :] = o_refs[gi][0, r, :, s * LANES:(s + 1) * LANES].astype(F32)
        lses.append(lse_ref[gi])
    mx = functools.reduce(jnp.maximum, lses)
    ex = [jnp.exp(t - mx) for t in lses]
    den = functools.reduce(jnp.add, ex)
    wide = []
    for t in ex:
        a = t / den
        a_hi = a.astype(BF16)
        a_lo = (a - a_hi.astype(F32)).astype(BF16)
        wide.append(_dot(a_hi, spread_ref[...]) + _dot(a_lo, spread_ref[...]))
    for p in range(PAIRS):
        sl = slice(p * LANES, (p + 1) * LANES)
        acc = jnp.zeros((tm, LANES), F32)
        for gi in range(n_groups):
            if DIL_PATTERNS[gi][1] == 1:
                acc = acc + wide[gi][:, sl] * o_refs[gi][0, 0, :, sl].astype(F32)
            else:
                acc = acc + wide[gi][:, sl] * slab_ref[gi, p]
        merged_ref[:, sl] = acc.astype(BF16)
    out_ref[...] = h_ref[...] + _rms(_dot(merged_ref[...], w_ref[...]), g_ref[...])


def _oproj_merge(os_, lses, w, h, g, tm, batch, seq):
    n, d = h.shape
    row = lambda i: (i, 0)
    width = os_[0].shape[-1]
    per = seq // tm
    n_groups = len(DIL_PATTERNS)

    def grouped_spec(dil, cols):
        return pl.BlockSpec((1, dil, tm // dil, cols), lambda i: (i // per, 0, i % per, 0))

    dils = [dil for _, dil in DIL_PATTERNS]
    spread = np.zeros((LANES, width), np.float32)
    for hh in range(HEADS):
        spread[hh, hh * HEAD_DIM:(hh + 1) * HEAD_DIM] = 1.0
    spread = jnp.asarray(spread, BF16)
    return pl.pallas_call(
        _oproj_merge_body,
        grid=(n // tm,),
        in_specs=[grouped_spec(dil, width) for dil in dils] + [grouped_spec(dil, LANES) for dil in dils]
        + [_const_spec(spread.shape), _const_spec(w.shape), pl.BlockSpec((tm, d), row), _const_spec((1, d))],
        out_specs=pl.BlockSpec((tm, d), row),
        out_shape=jax.ShapeDtypeStruct((n, d), F32),
        scratch_shapes=[pltpu.VMEM((n_groups, tm, LANES), F32),
                        pltpu.VMEM((n_groups, PAIRS, tm, LANES), F32),
                        pltpu.VMEM((tm, width), BF16)],
        compiler_params=_params(("parallel",)),
        name="oproj_merge",
    )(*os_, *lses, spread, w, h, g)


MXU_TILE = 256
FFN_SPLITS = (0, 6 * MXU_TILE, D_FF)


def _ffn_body(*refs, with_oproj):
    if with_oproj:
        o_ref, wattn_ref, g1_ref, *refs = refs
    h_ref, p_ref, g2_ref, g3_ref, win_ref, wo_ref, wproj_ref, wgate_ref, out_ref = refs
    h = h_ref[...]
    if with_oproj:
        h = h + _rms(_dot(o_ref[...], wattn_ref[0]), g1_ref[...])
    xn = _rms(h, g2_ref[...]).astype(BF16)
    y = None
    for a, b in zip(FFN_SPLITS[:-1], FFN_SPLITS[1:]):
        gate = _dot(xn, win_ref[0, :, a:b])
        up = _dot(xn, win_ref[0, :, D_FF + a:D_FF + b])
        act = (gate * jax.nn.sigmoid(gate) * up).astype(BF16)
        part = _dot(act, wo_ref[0, a:b, :])
        y = part if y is None else y + part
    h2 = h + _rms(y, g3_ref[...])
    emb = _dot(p_ref[0].astype(BF16), wproj_ref[0])
    out_ref[...] = h2 + emb * jax.nn.sigmoid(_dot(h2.astype(BF16), wgate_ref[0]))


def _layer_spec(stacked, layer):
    nd = stacked.ndim - 1
    return pl.BlockSpec((1,) + stacked.shape[1:], lambda *_: (layer,) + (0,) * nd,
                        pipeline_mode=pl.Buffered(1))


def _ffn(h, p, layer, g2, g3, w_in, w_out, w_proj, w_gate, tm, oproj=None):
    n, d = h.shape
    row = lambda i: (i, 0)
    head_specs, head_args = [], []
    if oproj is not None:
        o, w_attn, attn_layer, g1 = oproj
        head_specs = [pl.BlockSpec((tm, o.shape[1]), row), _layer_spec(w_attn, attn_layer),
                      _const_spec((1, d))]
        head_args = [o, w_attn, g1]
    return pl.pallas_call(
        functools.partial(_ffn_body, with_oproj=oproj is not None),
        grid=(n // tm,),
        in_specs=head_specs + [
            pl.BlockSpec((tm, d), row),
            pl.BlockSpec((1, tm, D_PLE), lambda i: (layer, i, 0)),
            _const_spec((1, d)),
            _const_spec((1, d)),
            _layer_spec(w_in, layer),
            _layer_spec(w_out, layer),
            _layer_spec(w_proj, layer),
            _layer_spec(w_gate, layer),
        ],
        out_specs=pl.BlockSpec((tm, d), row),
        out_shape=jax.ShapeDtypeStruct((n, d), F32),
        compiler_params=_params(("parallel",)),
        name="ffn",
    )(*head_args, h, p, g2, g3, w_in, w_out, w_proj, w_gate)


def _mla_weights(w_a, q_norm, kv_norm, w_uq, w_ukv):
    rank = MLA_Q_RANK + MLA_KV_RANK
    wa = jnp.pad(w_a, ((0, 0), (0, LANES - MLA_ROPE)))
    wa = jnp.concatenate([wa[:, :rank], jnp.roll(wa[:, rank:], MLA_NOPE, axis=1)], axis=1).astype(BF16)
    uq = w_uq.reshape(MLA_Q_RANK, HEADS, MLA_NOPE + MLA_ROPE)
    wuq = jnp.pad(uq, ((0, 0), (0, 0), (0, LANES - MLA_NOPE - MLA_ROPE)))
    wuq = wuq.reshape(MLA_Q_RANK, HEADS * LANES).astype(BF16)
    ukv = w_ukv.reshape(MLA_KV_RANK, HEADS, MLA_NOPE + HEAD_DIM)
    wuk = jnp.pad(ukv[:, :, :MLA_NOPE], ((0, 0), (0, 0), (0, LANES - MLA_NOPE)))
    wuk = wuk.reshape(MLA_KV_RANK, HEADS * LANES).astype(BF16)
    wuv = ukv[:, :, MLA_NOPE:].reshape(MLA_KV_RANK, HEADS * HEAD_DIM).astype(BF16)
    return dict(wa=wa, qn=q_norm.reshape(1, -1), kvn=kv_norm.reshape(1, -1),
                wuq=wuq, wuk=wuk, wuv=wuv)


def _q_scale_row(n_cols, q_starts, width, scale):
    row = np.ones((1, n_cols), np.float32)
    for s in q_starts:
        row[0, s:s + width] = scale
    return jnp.asarray(row)


def kernel(x, p, positions, norm_g, ffn_w_in, ffn_w_out, ple_w_proj, ple_w_gate, rel_bias, mla_w_a, mla_q_norm, mla_kv_norm, mla_w_uq, mla_w_ukv, mla_w_o, dil_w_qkv, dil_w_o, fox_w_qkvf, fox_b_f, fox_w_o):
    batch, seq, d = x.shape
    n = batch * seq
    inner = HEADS * HEAD_DIM
    h = x.reshape(n, d)
    rope_tabs = _rope_tables(positions, ROW_TILE)
    w_in, w_out = ffn_w_in.astype(BF16), ffn_w_out.astype(BF16)
    w_proj, w_gate = ple_w_proj.astype(BF16), ple_w_gate.astype(BF16)
    mla_wo, fox_wo = mla_w_o.astype(BF16), fox_w_o.astype(BF16)
    p_rows = p.reshape(N_LAYERS, n, D_PLE)
    for i in range(N_LAYERS):
        mixer, j = i % N_MIXERS, i // N_MIXERS
        g = norm_g[i].reshape(4, 1, d)
        if mixer == 0:
            w = _mla_weights(mla_w_a[j], mla_q_norm[j], mla_kv_norm[j], mla_w_uq[j], mla_w_ukv[j])
            q, k, vt = _mla_proj(h, rope_tabs, g[0], w, ROW_TILE, batch, seq)
            o = _flash(q.reshape(batch, seq, -1), k.reshape(batch, seq, -1), vt, batch, seq,
                       shared_lanes=False, k_off=0, tq=512)
            oproj = (o.reshape(n, inner), mla_wo, j, g[1])
        elif mixer == 1:
            table = rel_bias.reshape(REL_BUCKETS, len(DIL_PATTERNS), HEADS)
            q, k, vt = _dil_proj(h.reshape(batch, seq, d), g[0], dil_w_qkv[j].astype(BF16))
            outs, lses = [], []
            for gi in range(len(DIL_PATTERNS)):
                o, lse = _dil_attention(q, k, vt, table[:, gi], gi, batch, seq)
                outs.append(o)
                lses.append(lse)
            h = _oproj_merge(outs, lses, dil_w_o[j].astype(BF16), h, g[1], ROW_TILE, batch, seq)
            oproj = None
        else:
            wq = fox_w_qkvf[j]
            w = dict(
                wqkv=wq[:, :3 * inner].astype(BF16),
                scale=_q_scale_row(2 * inner, [0], inner, HEAD_DIM ** -0.5 * LOG2E),
                wf=jnp.pad(wq[:, 3 * inner:], ((0, 0), (0, LANES - HEADS))).astype(BF16),
                bf=jnp.pad(fox_b_f[j], (0, LANES - HEADS)).reshape(1, LANES),
            )
            a, vt, logf = _fox_proj(h, g[0], w, ROW_TILE, batch, seq)
            c, ct = _cumsum(logf, batch, seq)
            a3 = a.reshape(batch, seq, 2 * inner)
            o = _flash(a3, a3, vt, batch, seq, shared_lanes=True, k_off=PAIRS, forget=(c, ct))
            oproj = (o.reshape(n, inner), fox_wo, j, g[1])
        h = _ffn(h, p_rows, i, g[2], g[3], w_in, w_out, w_proj, w_gate, tm=ROW_TILE, oproj=oproj)
    return h.reshape(batch, seq, d)
```

```python
import functools

import numpy as np
import jax
import jax.numpy as jnp
from jax import lax
from jax.experimental import pallas as pl
from jax.experimental.pallas import tpu as pltpu

F32 = jnp.float32
BF16 = jnp.bfloat16

D_MODEL = 1024
N_LAYERS = 4
N_MIXERS = 3
D_PLE = 256
EPS = 1e-6
NEG = -1e30
D_FF = 2816

HEADS = 16
HEAD_DIM = 64
LANES = 128
PAIRS = HEADS // 2
ONES_ROWS = 16
LOG2E = 1.4426950408889634

MLA_Q_RANK = 384
MLA_KV_RANK = 256
MLA_NOPE = 64
MLA_ROPE = 32
ROPE_HALF = MLA_ROPE // 2
ROPE_THETA = 10000.0

DIL_PATTERNS = ((128, 1), (512, 4), (2048, 16))
DIL_BLOCK = 128
REL_BUCKETS = 32
REL_MAX_DIST = 2048

VMEM_LIMIT = 56 * 1024 * 1024
ROW_TILE = 512


def _params(sem):
    return pltpu.CompilerParams(dimension_semantics=sem, vmem_limit_bytes=VMEM_LIMIT)


def _rms(x, g):
    y = x * lax.rsqrt(jnp.mean(x * x, axis=-1, keepdims=True) + EPS)
    return y * g


def _dot(a, b):
    return jnp.dot(a, b, preferred_element_type=F32)


def _dot_nt(a, b):
    return lax.dot_general(a, b, (((1,), (1,)), ((), ())), preferred_element_type=F32)


def _dot_tt(a, b):
    return lax.dot_general(a, b, (((0,), (1,)), ((), ())), preferred_element_type=F32)


def _const_spec(shape):
    nd = len(shape)
    return pl.BlockSpec(shape, lambda *_: (0,) * nd)


ROW_CHUNK = 256


def _dil_proj_body(h_ref, g_ref, w_ref, qk_ref, vt_ref, slab_ref, perm_ref, *, dilation, seq):
    c = pl.program_id(1)
    n_slab = D_MODEL // LANES

    @pl.when(c == 0)
    def _():
        part = slab_ref.shape[1]
        for h0 in range(0, seq, part):
            def norm_rows(i, _):
                rows = pl.multiple_of(i * ROW_CHUNK, ROW_CHUNK)
                xn = _rms(h_ref[0, pl.ds(h0 + rows, ROW_CHUNK), :], g_ref[...])
                if dilation == 1:
                    perm_ref[pl.ds(h0 + rows, ROW_CHUNK), :] = xn.astype(BF16)
                else:
                    for s in range(n_slab):
                        slab_ref[s, pl.ds(rows, ROW_CHUNK), :] = xn[:, s * LANES:(s + 1) * LANES]
                return 0

            lax.fori_loop(0, part // ROW_CHUNK, norm_rows, 0)
            if dilation > 1:
                sub, cnt = seq // dilation, part // dilation
                for r in range(dilation):
                    dst = r * sub + h0 // dilation
                    for s in range(n_slab):
                        perm_ref[dst:dst + cnt, s * LANES:(s + 1) * LANES] = (
                            slab_ref[s, pl.ds(r, cnt, stride=dilation), :].astype(BF16))

    @pl.when(c == 0)
    def _():
        qk_ref[0] = (_dot(perm_ref[...], w_ref[...]) * (HEAD_DIM ** -0.5 * LOG2E)).astype(BF16)

    @pl.when(c == 1)
    def _():
        qk_ref[0] = _dot(perm_ref[...], w_ref[...]).astype(BF16)

    @pl.when(c == 2)
    def _():
        vt_ref[0] = _dot_tt(w_ref[...], perm_ref[...]).astype(BF16)


def _dil_proj(h3, g, w, group):
    batch, seq, d = h3.shape
    dilation = DIL_PATTERNS[group][1]
    width = HEADS * HEAD_DIM
    return pl.pallas_call(
        functools.partial(_dil_proj_body, dilation=dilation, seq=seq),
        grid=(batch, 3),
        in_specs=[
            pl.BlockSpec((1, seq, d), lambda b, c: (b, 0, 0)),
            pl.BlockSpec((1, d), lambda b, c: (0, 0)),
            pl.BlockSpec((d, width), lambda b, c: (0, group * 3 + c)),
        ],
        out_specs=[
            pl.BlockSpec((1, seq, width), lambda b, c: (b, 0, jnp.minimum(c, 1))),
            pl.BlockSpec((1, width, seq), lambda b, c: (b, 0, 0)),
        ],
        out_shape=[jax.ShapeDtypeStruct((batch, seq, 2 * width), BF16),
                   jax.ShapeDtypeStruct((batch, width, seq), BF16)],
        scratch_shapes=[pltpu.VMEM((d // LANES, seq // 2, LANES), F32), pltpu.VMEM((seq, d), BF16)],
        compiler_params=_params(("arbitrary", "arbitrary")),
        name=f"dil_proj_g{group}",
    )(h3, g, w)


def _rope_table_body(pos_ref, inv_ref, c_ref, s_ref):
    tm = pos_ref.shape[-1]
    ang = inv_ref[...] * pos_ref[0].astype(F32)
    cos, sin = jnp.cos(ang), jnp.sin(ang)
    pad = LANES - MLA_NOPE - MLA_ROPE
    c_t = jnp.concatenate([jnp.ones((MLA_NOPE, tm), F32), cos, jnp.zeros((pad, tm), F32)], axis=0)
    s_t = jnp.concatenate([jnp.zeros((MLA_NOPE, tm), F32), -sin[:ROPE_HALF], sin[ROPE_HALF:],
                           jnp.zeros((pad, tm), F32)], axis=0)
    c_ref[...] = c_t.T
    s_ref[...] = s_t.T


def _rope_tables(positions, tm):
    n = positions.size
    inv = ROPE_THETA ** (-jnp.arange(ROPE_HALF, dtype=F32) / ROPE_HALF)
    inv_col = jnp.concatenate([inv, inv]).reshape(MLA_ROPE, 1)
    return pl.pallas_call(
        _rope_table_body,
        grid=(n // tm,),
        in_specs=[pl.BlockSpec((1, 1, tm), lambda i: (i, 0, 0)), _const_spec((MLA_ROPE, 1))],
        out_specs=[pl.BlockSpec((tm, LANES), lambda i: (i, 0))] * 2,
        out_shape=[jax.ShapeDtypeStruct((n, LANES), F32)] * 2,
        compiler_params=_params(("parallel",)),
        name="rope_tables",
    )(positions.reshape(n // tm, 1, tm), inv_col)


def _mla_proj_body(h_ref, ctab_ref, stab_ref, g_ref, wa_ref, qn_ref, kvn_ref,
                   wuq_ref, wuk_ref, wuv_ref, q_ref, k_ref, vt_ref):
    xn = _rms(h_ref[...], g_ref[...]).astype(BF16)
    a = _dot(xn, wa_ref[...])
    cq = _rms(a[:, :MLA_Q_RANK], qn_ref[...]).astype(BF16)
    ckv = _rms(a[:, MLA_Q_RANK:MLA_Q_RANK + MLA_KV_RANK], kvn_ref[...]).astype(BF16)
    kr = a[:, MLA_Q_RANK + MLA_KV_RANK:]
    q = _dot(cq, wuq_ref[...])
    kn = _dot(ckv, wuk_ref[...])
    vt_ref[0] = _dot_tt(wuv_ref[...], ckv).astype(BF16)

    lane = lax.broadcasted_iota(jnp.int32, (1, LANES), 1)
    first = (lane >= MLA_NOPE) & (lane < MLA_NOPE + ROPE_HALF)
    c_tab, s_tab = ctab_ref[...], stab_ref[...]

    def rope(t):
        other = jnp.where(first, pltpu.roll(t, LANES - ROPE_HALF, 1), pltpu.roll(t, ROPE_HALF, 1))
        return t * c_tab + other * s_tab

    kr = rope(kr)
    scale = (MLA_NOPE + MLA_ROPE) ** -0.5 * LOG2E
    for hh in range(HEADS):
        sl = slice(hh * LANES, (hh + 1) * LANES)
        q_ref[:, sl] = (rope(q[:, sl]) * scale).astype(BF16)
        k_ref[:, sl] = (kn[:, sl] + kr).astype(BF16)


def _vt_spec(tm, seq, width):
    per = seq // tm
    return pl.BlockSpec((1, width, tm), lambda i: (i // per, 0, i % per))


def _mla_proj(h, tabs, g, w, tm, batch, seq):
    n, d = h.shape
    row = lambda i: (i, 0)
    width = HEADS * HEAD_DIM
    return pl.pallas_call(
        _mla_proj_body,
        grid=(n // tm,),
        in_specs=[
            pl.BlockSpec((tm, d), row),
            pl.BlockSpec((tm, LANES), row),
            pl.BlockSpec((tm, LANES), row),
            _const_spec((1, d)),
            _const_spec(w["wa"].shape),
            _const_spec((1, MLA_Q_RANK)), _const_spec((1, MLA_KV_RANK)),
            _const_spec(w["wuq"].shape), _const_spec(w["wuk"].shape), _const_spec(w["wuv"].shape),
        ],
        out_specs=[
            pl.BlockSpec((tm, HEADS * LANES), row),
            pl.BlockSpec((tm, HEADS * LANES), row),
            _vt_spec(tm, seq, width),
        ],
        out_shape=[
            jax.ShapeDtypeStruct((n, HEADS * LANES), BF16),
            jax.ShapeDtypeStruct((n, HEADS * LANES), BF16),
            jax.ShapeDtypeStruct((batch, width, seq), BF16),
        ],
        compiler_params=_params(("parallel",)),
        name="mla_proj",
    )(h, tabs[0], tabs[1], g, w["wa"], w["qn"], w["kvn"],
      w["wuq"], w["wuk"], w["wuv"])


def _fox_proj_body(h_ref, g_ref, w_ref, scale_ref, wf_ref, bf_ref, a_ref, vt_ref, logf_ref):
    xn = _rms(h_ref[...], g_ref[...]).astype(BF16)
    nqk = a_ref.shape[1]
    a_ref[...] = (_dot(xn, w_ref[:, :nqk]) * scale_ref[...]).astype(BF16)
    vt_ref[0] = _dot_tt(w_ref[:, nqk:], xn).astype(BF16)
    f = _dot(xn, wf_ref[...]) + bf_ref[...]
    logf_ref[...] = jnp.minimum(f, 0.0) - jnp.log1p(jnp.exp(-jnp.abs(f)))


def _fox_proj(h, g, w, tm, batch, seq):
    n, d = h.shape
    width = HEADS * HEAD_DIM
    nout = 2 * width
    row = lambda i: (i, 0)
    return pl.pallas_call(
        _fox_proj_body,
        grid=(n // tm,),
        in_specs=[
            pl.BlockSpec((tm, d), row),
            _const_spec((1, d)),
            _const_spec((d, 3 * width)),
            _const_spec((1, nout)),
            _const_spec((d, LANES)),
            _const_spec((1, LANES)),
        ],
        out_specs=[pl.BlockSpec((tm, nout), row), _vt_spec(tm, seq, width),
                   pl.BlockSpec((tm, LANES), row)],
        out_shape=[jax.ShapeDtypeStruct((n, nout), BF16),
                   jax.ShapeDtypeStruct((batch, width, seq), BF16),
                   jax.ShapeDtypeStruct((n, LANES), F32)],
        compiler_params=_params(("parallel",)),
        name="fox_proj",
    )(h, g, w["wqkv"], w["scale"], w["wf"], w["bf"])


def _cumsum_body(x_ref, c_ref, ct_ref, *, seq, blk):
    r = lax.broadcasted_iota(jnp.int32, (blk, blk), 0)
    c = lax.broadcasted_iota(jnp.int32, (blk, blk), 1)
    tri = (c <= r).astype(F32)
    carry = jnp.zeros((1, LANES), F32)
    for b in range(seq // blk):
        xs = x_ref[0, b * blk:(b + 1) * blk, :]
        cs = lax.dot_general(tri, xs, (((1,), (0,)), ((), ())), precision=lax.Precision.HIGHEST,
                             preferred_element_type=F32) + carry
        c_ref[0, b * blk:(b + 1) * blk, :] = cs
        carry = cs[blk - 1:blk, :]
    ct_ref[0] = c_ref[0].T


def _cumsum(logf, batch, seq):
    x = logf.reshape(batch, seq, LANES)
    return pl.pallas_call(
        functools.partial(_cumsum_body, seq=seq, blk=256),
        grid=(batch,),
        in_specs=[pl.BlockSpec((1, seq, LANES), lambda b: (b, 0, 0))],
        out_specs=[pl.BlockSpec((1, seq, LANES), lambda b: (b, 0, 0)),
                   pl.BlockSpec((1, LANES, seq), lambda b: (b, 0, 0))],
        out_shape=[jax.ShapeDtypeStruct((batch, seq, LANES), F32),
                   jax.ShapeDtypeStruct((batch, LANES, seq), F32)],
        compiler_params=_params(("parallel",)),
        name="fox_cumsum",
    )(x)


def _flash_body(*refs, seq, tq, tk, forget, shared_lanes):
    if forget:
        q_ref, k_ref, vt_ref, c_ref, ct_ref, o_ref, va_ref, ck_ref = refs
    else:
        q_ref, k_ref, vt_ref, o_ref, va_ref = refs
    pair = pl.program_id(1)
    lane = lax.broadcasted_iota(jnp.int32, (1, LANES), 1)
    low = lane < HEAD_DIM
    causal = (lax.broadcasted_iota(jnp.int32, (tk, tq), 0)
              <= lax.broadcasted_iota(jnp.int32, (tk, tq), 1))
    for e in range(2):
        va_ref[e, :HEAD_DIM, :] = vt_ref[0, e * HEAD_DIM:(e + 1) * HEAD_DIM, :]
        va_ref[e, HEAD_DIM:, :] = jnp.ones((ONES_ROWS, seq), BF16)
    if forget:
        for e in range(2):
            col = jnp.sum(jnp.where(lane == 2 * pair + e, c_ref[0], 0.0), axis=1, keepdims=True)
            ck_ref[e] = jnp.broadcast_to(col * LOG2E, (seq, tq))

    class Chain:
        pass

    def start(qi, e):
        ch = Chain()
        qs = qi * tq
        ch.qi, ch.e = qi, e
        if shared_lanes:
            qp = q_ref[0, qs:qs + tq, :]
            ch.q = jnp.where(low, qp, 0) if e == 0 else jnp.where(low, 0, qp)
            ch.ksl = slice(0, LANES)
        else:
            ch.ksl = slice(e * LANES, (e + 1) * LANES)
            ch.q = q_ref[0, qs:qs + tq, ch.ksl]
        if forget:
            ch.cq = ct_ref[0, pl.ds(2 * pair + e, 1), qs:qs + tq] * LOG2E
        ch.s, ch.m, ch.acc = [], None, None
        return ch

    def lanes_from(x, off, new):
        return new if off == 0 else jnp.concatenate([x[:, :off], new], axis=1)

    def pass1(ch, c):
        ks = c * tk
        off = max(ks - ch.qi * tq, 0)
        s = _dot_nt(k_ref[0, ks:ks + tk, ch.ksl], ch.q[off:])
        if forget:
            s = s + (ch.cq[:, off:] - ck_ref[ch.e, ks:ks + tk, :tq - off])
        if ks + tk > ch.qi * tq:
            s = jnp.where(causal[:, :tq - off], s, NEG)
        mc = jnp.max(s, axis=0, keepdims=True)
        ch.m = mc if ch.m is None else lanes_from(ch.m, off, jnp.maximum(ch.m[:, off:], mc))
        ch.s.append(s)

    def pass2(ch, c):
        ks = c * tk
        off = tq - ch.s[c].shape[1]
        p = jnp.exp2(ch.s[c] - ch.m[:, off:])
        ac = _dot(va_ref[ch.e, :, ks:ks + tk], p.astype(BF16))
        ch.acc = ac if ch.acc is None else lanes_from(ch.acc, off, ch.acc[:, off:] + ac)

    outs = []

    def finish(ch):
        outs.append(ch.acc[:HEAD_DIM] / ch.acc[HEAD_DIM:HEAD_DIM + 1])
        if ch.e == 1:
            qs = ch.qi * tq
            o_t = jnp.concatenate(outs, axis=0)
            o_ref[0, qs:qs + tq, :] = o_t.T.astype(o_ref.dtype)
            outs.clear()

    per = tq // tk
    prev = ()
    for qi in range(seq // tq + 1):
        cur = tuple(start(qi, e) for e in range(2)) if qi < seq // tq else ()
        n_cur, n_prev = (qi + 1) * per if cur else 0, qi * per
        for c in range(max(n_cur, n_prev)):
            if c < n_cur:
                for ch in cur:
                    pass1(ch, c)
            if c < n_prev:
                for ch in prev:
                    pass2(ch, c)
        for ch in prev:
            finish(ch)
        prev = cur


def _flash(q, k, vt, batch, seq, *, shared_lanes, k_off, forget=None, tq=256, tk=256):
    qk_w = LANES if shared_lanes else 2 * LANES
    in_specs = [
        pl.BlockSpec((1, seq, qk_w), lambda b, p: (b, 0, p)),
        pl.BlockSpec((1, seq, qk_w), lambda b, p: (b, 0, k_off + p)),
        pl.BlockSpec((1, LANES, seq), lambda b, p: (b, p, 0)),
    ]
    args = [q, k, vt]
    scratch = [pltpu.VMEM((2, HEAD_DIM + ONES_ROWS, seq), BF16)]
    if forget is not None:
        c, ct = forget
        in_specs += [pl.BlockSpec((1, seq, LANES), lambda b, p: (b, 0, 0)),
                     pl.BlockSpec((1, HEADS, seq), lambda b, p: (b, 0, 0))]
        args += [c, ct]
        scratch += [pltpu.VMEM((2, seq, tq), F32)]
    return pl.pallas_call(
        functools.partial(_flash_body, seq=seq, tq=tq, tk=tk, forget=forget is not None,
                          shared_lanes=shared_lanes),
        grid=(batch, PAIRS),
        in_specs=in_specs,
        out_specs=pl.BlockSpec((1, seq, LANES), lambda b, p: (b, 0, p)),
        out_shape=jax.ShapeDtypeStruct((batch, seq, HEADS * HEAD_DIM), BF16),
        scratch_shapes=scratch,
        compiler_params=_params(("parallel", "parallel")),
        name="flash_fox" if forget is not None else "flash_mla",
    )(*args)


def _t5_bucket_np(dist):
    max_exact = REL_BUCKETS // 2
    n = np.maximum(dist.astype(np.float32), np.float32(1.0))
    large = max_exact + (np.log(n / np.float32(max_exact)) / np.float32(np.log(REL_MAX_DIST / max_exact))
                         * np.float32(REL_BUCKETS - max_exact)).astype(np.int32)
    large = np.minimum(large, REL_BUCKETS - 1)
    return np.where(dist < max_exact, dist, large).astype(np.int32)


def _dil_bucket_map(dilation, use_prev):
    qry = np.arange(DIL_BLOCK)
    key = np.arange(2 * DIL_BLOCK)
    rel = DIL_BLOCK + qry[None, :] - key[:, None]
    bk = _t5_bucket_np(np.clip(rel, 0, None) * dilation)
    if not use_prev:
        bk = bk[DIL_BLOCK:]
    return np.concatenate([bk, bk], axis=1)


def _dil_body(tab_ref, bucket_ref, q_ref, k_ref, vt_ref, o_ref, lse_ref, bias_ref, *, span, n_blk, seq):
    blk = DIL_BLOCK
    use_prev = n_blk > 1
    nkeys = 2 * blk if use_prev else blk

    @pl.when(pl.program_id(0) == 0)
    def _():
        bk = bucket_ref[...]
        second = lax.broadcasted_iota(jnp.int32, (1, 2 * blk), 1) >= blk
        key = lax.broadcasted_iota(jnp.int32, (nkeys, 2 * blk), 0) + (0 if use_prev else blk)
        qry = lax.broadcasted_iota(jnp.int32, (nkeys, 2 * blk), 1) & (blk - 1)
        rel = blk + qry - key
        band = (rel >= 0) & (rel <= span)

        def fill(p, _):
            acc = jnp.zeros((nkeys, 2 * blk), F32)
            for bb in range(REL_BUCKETS):
                val = jnp.where(second, tab_ref[bb, 2 * p + 1], tab_ref[bb, 2 * p])
                acc = jnp.where(bk == bb, val * LOG2E, acc)
            bias_ref[0, p] = jnp.where(band & (key >= blk), acc, NEG)
            bias_ref[1, p] = jnp.where(band, acc, NEG)
            return 0

        lax.fori_loop(0, PAIRS, fill, 0)

    low = lax.broadcasted_iota(jnp.int32, (1, LANES), 1) < HEAD_DIM
    head_row = lax.broadcasted_iota(jnp.int32, (LANES, blk), 0)

    def block(s, _):
        cur = pl.ds(pl.multiple_of(s * blk, blk), blk)
        cls, rows = s // n_blk, pl.ds(pl.multiple_of((s % n_blk) * blk, blk), blk)
        if use_prev:
            n = s % n_blk
            prv = pl.ds(pl.multiple_of(jnp.where(n > 0, s - 1, s) * blk, blk), blk)
            table = jnp.where(n > 0, 1, 0)
        else:
            table = 1
        def logits(p):
            sl = slice(p * LANES, (p + 1) * LANES)
            qp = q_ref[0, cur, sl]
            qq = jnp.concatenate([jnp.where(low, qp, 0), jnp.where(low, 0, qp)], axis=0)
            if use_prev:
                kk = jnp.concatenate([k_ref[0, prv, sl], k_ref[0, cur, sl]], axis=0)
            else:
                kk = k_ref[0, cur, sl]
            return _dot_nt(kk, qq)

        def masked(p, raw):
            st = raw + bias_ref[table, p]
            return st, jnp.max(st, axis=0, keepdims=True)

        def weighted(p, st, m):
            sl = slice(p * LANES, (p + 1) * LANES)
            if use_prev:
                vv = jnp.concatenate([vt_ref[0, sl, prv], vt_ref[0, sl, cur]], axis=1)
            else:
                vv = vt_ref[0, sl, cur]
            vv = jnp.concatenate([vv, jnp.ones((ONES_ROWS, nkeys), BF16)], axis=0)
            return _dot(vv, jnp.exp2(st - m).astype(BF16))

        def finish(p, ot, m, lse_t):
            sl = slice(p * LANES, (p + 1) * LANES)
            l = ot[LANES:LANES + 1]
            ot = ot[:LANES] / l
            o_t = jnp.concatenate([ot[:HEAD_DIM, :blk], ot[HEAD_DIM:, blk:]], axis=0)
            o_ref[0, cls, rows, sl] = o_t.T.astype(o_ref.dtype)
            lse = m * (1.0 / LOG2E) + jnp.log(l)
            return jnp.where(head_row == 2 * p, lse[:, :blk],
                             jnp.where(head_row == 2 * p + 1, lse[:, blk:], lse_t))

        raws = [logits(p) for p in range(PAIRS)]
        sms = [masked(p, raws[p]) for p in range(PAIRS)]
        ots = [weighted(p, *sms[p]) for p in range(PAIRS)]
        lse_t = jnp.zeros((LANES, blk), F32)
        for p in range(PAIRS):
            lse_t = finish(p, ots[p], sms[p][1], lse_t)
        lse_ref[0, cls, rows, :] = lse_t.T
        return 0

    lax.fori_loop(0, seq // blk, block, 0, unroll=2)


def _dil_attention(qk, vt, table, group, batch, seq):
    window, dilation = DIL_PATTERNS[group]
    span = window // dilation
    sub = seq // dilation
    assert sub % DIL_BLOCK == 0
    n_blk = sub // DIL_BLOCK
    width = HEADS * HEAD_DIM
    bucket = jnp.asarray(_dil_bucket_map(dilation, n_blk > 1))
    return pl.pallas_call(
        functools.partial(_dil_body, span=span, n_blk=n_blk, seq=seq),
        grid=(batch,),
        in_specs=[
            pl.BlockSpec(memory_space=pltpu.SMEM),
            _const_spec(bucket.shape),
            pl.BlockSpec((1, seq, width), lambda b: (b, 0, 0)),
            pl.BlockSpec((1, seq, width), lambda b: (b, 0, 1)),
            pl.BlockSpec((1, width, seq), lambda b: (b, 0, 0)),
        ],
        out_specs=[pl.BlockSpec((1, dilation, sub, width), lambda b: (b, 0, 0, 0)),
                   pl.BlockSpec((1, dilation, sub, LANES), lambda b: (b, 0, 0, 0))],
        out_shape=[jax.ShapeDtypeStruct((batch, dilation, sub, width), BF16),
                   jax.ShapeDtypeStruct((batch, dilation, sub, LANES), F32)],
        scratch_shapes=[pltpu.VMEM((2, PAIRS) + bucket.shape, F32)],
        compiler_params=_params(("arbitrary",)),
        name=f"dil_attn_g{group}",
    )(table, bucket, qk, qk, vt)


def _oproj_merge_body(o0_ref, o1_ref, o2_ref, l0_ref, l1_ref, l2_ref, spread_ref, w_ref, h_ref, g_ref,
                      out_ref, lse_ref, slab_ref, merged_ref):
    o_refs = (o0_ref, o1_ref, o2_ref)
    l_refs = (l0_ref, l1_ref, l2_ref)
    n_groups = len(DIL_PATTERNS)
    tm = h_ref.shape[0]
    lses = []
    for gi in range(n_groups):
        dil = DIL_PATTERNS[gi][1]
        if dil == 1:
            lses.append(l_refs[gi][0, 0])
            continue
        for r in range(dil):
            rows = pl.ds(r, tm // dil, stride=dil)
            lse_ref[gi, rows, :] = l_refs[gi][0, r]
            for s in range(PAIRS):
                slab_ref[gi, s, rows, :] = o_refs[gi][0, r, :, s * LANES:(s + 1) * LANES].astype(F32)
        lses.append(lse_ref[gi])
    mx = functools.reduce(jnp.maximum, lses)
    ex = [jnp.exp(t - mx) for t in lses]
    den = functools.reduce(jnp.add, ex)
    wide = []
    for t in ex:
        a = t / den
        a_hi = a.astype(BF16)
        a_lo = (a - a_hi.astype(F32)).astype(BF16)
        wide.append(_dot(a_hi, spread_ref[...]) + _dot(a_lo, spread_ref[...]))
    for p in range(PAIRS):
        sl = slice(p * LANES, (p + 1) * LANES)
        acc = jnp.zeros((tm, LANES), F32)
        for gi in range(n_groups):
            if DIL_PATTERNS[gi][1] == 1:
                acc = acc + wide[gi][:, sl] * o_refs[gi][0, 0, :, sl].astype(F32)
            else:
                acc = acc + wide[gi][:, sl] * slab_ref[gi, p]
        merged_ref[:, sl] = acc.astype(BF16)
    out_ref[...] = h_ref[...] + _rms(_dot(merged_ref[...], w_ref[...]), g_ref[...])


def _oproj_merge(os_, lses, w, h, g, tm, batch, seq):
    n, d = h.shape
    row = lambda i: (i, 0)
    width = os_[0].shape[-1]
    per = seq // tm
    n_groups = len(DIL_PATTERNS)

    def grouped_spec(dil, cols):
        return pl.BlockSpec((1, dil, tm // dil, cols), lambda i: (i // per, 0, i % per, 0))

    dils = [dil for _, dil in DIL_PATTERNS]
    spread = np.zeros((LANES, width), np.float32)
    for hh in range(HEADS):
        spread[hh, hh * HEAD_DIM:(hh + 1) * HEAD_DIM] = 1.0
    spread = jnp.asarray(spread, BF16)
    return pl.pallas_call(
        _oproj_merge_body,
        grid=(n // tm,),
        in_specs=[grouped_spec(dil, width) for dil in dils] + [grouped_spec(dil, LANES) for dil in dils]
        + [_const_spec(spread.shape), _const_spec(w.shape), pl.BlockSpec((tm, d), row), _const_spec((1, d))],
        out_specs=pl.BlockSpec((tm, d), row),
        out_shape=jax.ShapeDtypeStruct((n, d), F32),
        scratch_shapes=[pltpu.VMEM((n_groups, tm, LANES), F32),
                        pltpu.VMEM((n_groups, PAIRS, tm, LANES), F32),
                        pltpu.VMEM((tm, width), BF16)],
        compiler_params=_params(("parallel",)),
        name="oproj_merge",
    )(*os_, *lses, spread, w, h, g)


MXU_TILE = 256
FFN_SPLITS = (0, 6 * MXU_TILE, D_FF)


def _ffn_body(*refs, with_oproj):
    if with_oproj:
        o_ref, wattn_ref, g1_ref, *refs = refs
    h_ref, p_ref, g2_ref, g3_ref, win_ref, wo_ref, wproj_ref, wgate_ref, out_ref = refs
    h = h_ref[...]
    if with_oproj:
        h = h + _rms(_dot(o_ref[...], wattn_ref[0]), g1_ref[...])
    xn = _rms(h, g2_ref[...]).astype(BF16)
    y = None
    for a, b in zip(FFN_SPLITS[:-1], FFN_SPLITS[1:]):
        gate = _dot(xn, win_ref[0, :, a:b])
        up = _dot(xn, win_ref[0, :, D_FF + a:D_FF + b])
        act = (gate * jax.nn.sigmoid(gate) * up).astype(BF16)
        part = _dot(act, wo_ref[0, a:b, :])
        y = part if y is None else y + part
    h2 = h + _rms(y, g3_ref[...])
    emb = _dot(p_ref[0].astype(BF16), wproj_ref[0])
    out_ref[...] = h2 + emb * jax.nn.sigmoid(_dot(h2.astype(BF16), wgate_ref[0]))


def _layer_spec(stacked, layer):
    nd = stacked.ndim - 1
    return pl.BlockSpec((1,) + stacked.shape[1:], lambda *_: (layer,) + (0,) * nd,
                        pipeline_mode=pl.Buffered(1))


def _ffn(h, p, layer, g2, g3, w_in, w_out, w_proj, w_gate, tm, oproj=None):
    n, d = h.shape
    row = lambda i: (i, 0)
    head_specs, head_args = [], []
    if oproj is not None:
        o, w_attn, attn_layer, g1 = oproj
        head_specs = [pl.BlockSpec((tm, o.shape[1]), row), _layer_spec(w_attn, attn_layer),
                      _const_spec((1, d))]
        head_args = [o, w_attn, g1]
    return pl.pallas_call(
        functools.partial(_ffn_body, with_oproj=oproj is not None),
        grid=(n // tm,),
        in_specs=head_specs + [
            pl.BlockSpec((tm, d), row),
            pl.BlockSpec((1, tm, D_PLE), lambda i: (layer, i, 0)),
            _const_spec((1, d)),
            _const_spec((1, d)),
            _layer_spec(w_in, layer),
            _layer_spec(w_out, layer),
            _layer_spec(w_proj, layer),
            _layer_spec(w_gate, layer),
        ],
        out_specs=pl.BlockSpec((tm, d), row),
        out_shape=jax.ShapeDtypeStruct((n, d), F32),
        compiler_params=_params(("parallel",)),
        name="ffn",
    )(*head_args, h, p, g2, g3, w_in, w_out, w_proj, w_gate)


def _mla_weights(w_a, q_norm, kv_norm, w_uq, w_ukv):
    rank = MLA_Q_RANK + MLA_KV_RANK
    wa = jnp.pad(w_a, ((0, 0), (0, LANES - MLA_ROPE)))
    wa = jnp.concatenate([wa[:, :rank], jnp.roll(wa[:, rank:], MLA_NOPE, axis=1)], axis=1).astype(BF16)
    uq = w_uq.reshape(MLA_Q_RANK, HEADS, MLA_NOPE + MLA_ROPE)
    wuq = jnp.pad(uq, ((0, 0), (0, 0), (0, LANES - MLA_NOPE - MLA_ROPE)))
    wuq = wuq.reshape(MLA_Q_RANK, HEADS * LANES).astype(BF16)
    ukv = w_ukv.reshape(MLA_KV_RANK, HEADS, MLA_NOPE + HEAD_DIM)
    wuk = jnp.pad(ukv[:, :, :MLA_NOPE], ((0, 0), (0, 0), (0, LANES - MLA_NOPE)))
    wuk = wuk.reshape(MLA_KV_RANK, HEADS * LANES).astype(BF16)
    wuv = ukv[:, :, MLA_NOPE:].reshape(MLA_KV_RANK, HEADS * HEAD_DIM).astype(BF16)
    return dict(wa=wa, qn=q_norm.reshape(1, -1), kvn=kv_norm.reshape(1, -1),
                wuq=wuq, wuk=wuk, wuv=wuv)


def _q_scale_row(n_cols, q_starts, width, scale):
    row = np.ones((1, n_cols), np.float32)
    for s in q_starts:
        row[0, s:s + width] = scale
    return jnp.asarray(row)


def kernel(x, p, positions, norm_g, ffn_w_in, ffn_w_out, ple_w_proj, ple_w_gate, rel_bias, mla_w_a, mla_q_norm, mla_kv_norm, mla_w_uq, mla_w_ukv, mla_w_o, dil_w_qkv, dil_w_o, fox_w_qkvf, fox_b_f, fox_w_o):
    batch, seq, d = x.shape
    n = batch * seq
    inner = HEADS * HEAD_DIM
    h = x.reshape(n, d)
    rope_tabs = _rope_tables(positions, ROW_TILE)
    w_in, w_out = ffn_w_in.astype(BF16), ffn_w_out.astype(BF16)
    w_proj, w_gate = ple_w_proj.astype(BF16), ple_w_gate.astype(BF16)
    mla_wo, fox_wo = mla_w_o.astype(BF16), fox_w_o.astype(BF16)
    p_rows = p.reshape(N_LAYERS, n, D_PLE)
    for i in range(N_LAYERS):
        mixer, j = i % N_MIXERS, i // N_MIXERS
        g = norm_g[i].reshape(4, 1, d)
        if mixer == 0:
            w = _mla_weights(mla_w_a[j], mla_q_norm[j], mla_kv_norm[j], mla_w_uq[j], mla_w_ukv[j])
            q, k, vt = _mla_proj(h, rope_tabs, g[0], w, ROW_TILE, batch, seq)
            o = _flash(q.reshape(batch, seq, -1), k.reshape(batch, seq, -1), vt, batch, seq,
                       shared_lanes=False, k_off=0, tq=512)
            oproj = (o.reshape(n, inner), mla_wo, j, g[1])
        elif mixer == 1:
            table = rel_bias.reshape(REL_BUCKETS, len(DIL_PATTERNS), HEADS)
            h3, wq = h.reshape(batch, seq, d), dil_w_qkv[j].astype(BF16)
            outs, lses = [], []
            for gi in range(len(DIL_PATTERNS)):
                qk, vt = _dil_proj(h3, g[0], wq, gi)
                o, lse = _dil_attention(qk, vt, table[:, gi], gi, batch, seq)
                outs.append(o)
                lses.append(lse)
            h = _oproj_merge(outs, lses, dil_w_o[j].astype(BF16), h, g[1], ROW_TILE, batch, seq)
            oproj = None
        else:
            wq = fox_w_qkvf[j]
            w = dict(
                wqkv=wq[:, :3 * inner].astype(BF16),
                scale=_q_scale_row(2 * inner, [0], inner, HEAD_DIM ** -0.5 * LOG2E),
                wf=jnp.pad(wq[:, 3 * inner:], ((0, 0), (0, LANES - HEADS))).astype(BF16),
                bf=jnp.pad(fox_b_f[j], (0, LANES - HEADS)).reshape(1, LANES),
            )
            a, vt, logf = _fox_proj(h, g[0], w, ROW_TILE, batch, seq)
            c, ct = _cumsum(logf, batch, seq)
            a3 = a.reshape(batch, seq, 2 * inner)
            o = _flash(a3, a3, vt, batch, seq, shared_lanes=True, k_off=PAIRS, forget=(c, ct))
            oproj = (o.reshape(n, inner), fox_wo, j, g[1])
        h = _ffn(h, p_rows, i, g[2], g[3], w_in, w_out, w_proj, w_gate, tm=ROW_TILE, oproj=oproj)
    return h.reshape(batch, seq, d)
```

```python
import functools

import numpy as np
import jax
import jax.numpy as jnp
from jax import lax
from jax.experimental import pallas as pl
from jax.experimental.pallas import tpu as pltpu

F32 = jnp.float32
BF16 = jnp.bfloat16

D_MODEL = 1024
N_LAYERS = 4
N_MIXERS = 3
D_PLE = 256
EPS = 1e-6
NEG = -1e30
D_FF = 2816

HEADS = 16
HEAD_DIM = 64
LANES = 128
PAIRS = HEADS // 2
ONES_ROWS = 16
LOG2E = 1.4426950408889634

MLA_Q_RANK = 384
MLA_KV_RANK = 256
MLA_NOPE = 64
MLA_ROPE = 32
ROPE_HALF = MLA_ROPE // 2
ROPE_THETA = 10000.0

DIL_PATTERNS = ((128, 1), (512, 4), (2048, 16))
DIL_BLOCK = 128
REL_BUCKETS = 32
REL_MAX_DIST = 2048

VMEM_LIMIT = 56 * 1024 * 1024
ROW_TILE = 512


def _params(sem):
    return pltpu.CompilerParams(dimension_semantics=sem, vmem_limit_bytes=VMEM_LIMIT)


def _rms(x, g):
    y = x * lax.rsqrt(jnp.mean(x * x, axis=-1, keepdims=True) + EPS)
    return y * g


def _dot(a, b):
    return jnp.dot(a, b, preferred_element_type=F32)


def _dot_nt(a, b):
    return lax.dot_general(a, b, (((1,), (1,)), ((), ())), preferred_element_type=F32)


def _dot_tt(a, b):
    return lax.dot_general(a, b, (((0,), (1,)), ((), ())), preferred_element_type=F32)


def _const_spec(shape):
    nd = len(shape)
    return pl.BlockSpec(shape, lambda *_: (0,) * nd)


ROW_CHUNK = 256


def _dil_proj_body(h_ref, g_ref, w_ref, qk_ref, vt_ref, slab_ref, perm_ref, *, dilation, seq):
    c = pl.program_id(1)
    n_slab = D_MODEL // LANES

    @pl.when(c == 0)
    def _():
        part = slab_ref.shape[1]
        for h0 in range(0, seq, part):
            def norm_rows(i, _):
                rows = pl.multiple_of(i * ROW_CHUNK, ROW_CHUNK)
                xn = _rms(h_ref[0, pl.ds(h0 + rows, ROW_CHUNK), :], g_ref[...])
                if dilation == 1:
                    perm_ref[pl.ds(h0 + rows, ROW_CHUNK), :] = xn.astype(BF16)
                else:
                    for s in range(n_slab):
                        slab_ref[s, pl.ds(rows, ROW_CHUNK), :] = xn[:, s * LANES:(s + 1) * LANES]
                return 0

            lax.fori_loop(0, part // ROW_CHUNK, norm_rows, 0)
            if dilation > 1:
                sub, cnt = seq // dilation, part // dilation
                for r in range(dilation):
                    dst = r * sub + h0 // dilation
                    for s in range(n_slab):
                        perm_ref[dst:dst + cnt, s * LANES:(s + 1) * LANES] = (
                            slab_ref[s, pl.ds(r, cnt, stride=dilation), :].astype(BF16))

    @pl.when(c == 0)
    def _():
        qk_ref[0] = (_dot(perm_ref[...], w_ref[...]) * (HEAD_DIM ** -0.5 * LOG2E)).astype(BF16)

    @pl.when(c == 1)
    def _():
        qk_ref[0] = _dot(perm_ref[...], w_ref[...]).astype(BF16)

    @pl.when(c == 2)
    def _():
        vt_ref[0] = _dot_tt(w_ref[...], perm_ref[...]).astype(BF16)


def _dil_proj(h3, g, w, group):
    batch, seq, d = h3.shape
    dilation = DIL_PATTERNS[group][1]
    width = HEADS * HEAD_DIM
    return pl.pallas_call(
        functools.partial(_dil_proj_body, dilation=dilation, seq=seq),
        grid=(batch, 3),
        in_specs=[
            pl.BlockSpec((1, seq, d), lambda b, c: (b, 0, 0)),
            pl.BlockSpec((1, d), lambda b, c: (0, 0)),
            pl.BlockSpec((d, width), lambda b, c: (0, group * 3 + c)),
        ],
        out_specs=[
            pl.BlockSpec((1, seq, width), lambda b, c: (b, 0, jnp.minimum(c, 1))),
            pl.BlockSpec((1, width, seq), lambda b, c: (b, 0, 0)),
        ],
        out_shape=[jax.ShapeDtypeStruct((batch, seq, 2 * width), BF16),
                   jax.ShapeDtypeStruct((batch, width, seq), BF16)],
        scratch_shapes=[pltpu.VMEM((d // LANES, seq // 2, LANES), F32), pltpu.VMEM((seq, d), BF16)],
        compiler_params=_params(("arbitrary", "arbitrary")),
        name=f"dil_proj_g{group}",
    )(h3, g, w)


def _rope_table_body(pos_ref, inv_ref, c_ref, s_ref):
    tm = pos_ref.shape[-1]
    ang = inv_ref[...] * pos_ref[0].astype(F32)
    cos, sin = jnp.cos(ang), jnp.sin(ang)
    pad = LANES - MLA_NOPE - MLA_ROPE
    c_t = jnp.concatenate([jnp.ones((MLA_NOPE, tm), F32), cos, jnp.zeros((pad, tm), F32)], axis=0)
    s_t = jnp.concatenate([jnp.zeros((MLA_NOPE, tm), F32), -sin[:ROPE_HALF], sin[ROPE_HALF:],
                           jnp.zeros((pad, tm), F32)], axis=0)
    c_ref[...] = c_t.T
    s_ref[...] = s_t.T


def _rope_tables(positions, tm):
    n = positions.size
    inv = ROPE_THETA ** (-jnp.arange(ROPE_HALF, dtype=F32) / ROPE_HALF)
    inv_col = jnp.concatenate([inv, inv]).reshape(MLA_ROPE, 1)
    return pl.pallas_call(
        _rope_table_body,
        grid=(n // tm,),
        in_specs=[pl.BlockSpec((1, 1, tm), lambda i: (i, 0, 0)), _const_spec((MLA_ROPE, 1))],
        out_specs=[pl.BlockSpec((tm, LANES), lambda i: (i, 0))] * 2,
        out_shape=[jax.ShapeDtypeStruct((n, LANES), F32)] * 2,
        compiler_params=_params(("parallel",)),
        name="rope_tables",
    )(positions.reshape(n // tm, 1, tm), inv_col)


def _mla_proj_body(h_ref, ctab_ref, stab_ref, g_ref, wa_ref, qn_ref, kvn_ref,
                   wuq_ref, wuk_ref, wuv_ref, q_ref, k_ref, vt_ref):
    xn = _rms(h_ref[...], g_ref[...]).astype(BF16)
    a = _dot(xn, wa_ref[...])
    cq = _rms(a[:, :MLA_Q_RANK], qn_ref[...]).astype(BF16)
    ckv = _rms(a[:, MLA_Q_RANK:MLA_Q_RANK + MLA_KV_RANK], kvn_ref[...]).astype(BF16)
    kr = a[:, MLA_Q_RANK + MLA_KV_RANK:]
    q = _dot(cq, wuq_ref[...])
    kn = _dot(ckv, wuk_ref[...])
    vt_ref[0] = _dot_tt(wuv_ref[...], ckv).astype(BF16)

    lane = lax.broadcasted_iota(jnp.int32, (1, LANES), 1)
    first = (lane >= MLA_NOPE) & (lane < MLA_NOPE + ROPE_HALF)
    c_tab, s_tab = ctab_ref[...], stab_ref[...]

    def rope(t):
        other = jnp.where(first, pltpu.roll(t, LANES - ROPE_HALF, 1), pltpu.roll(t, ROPE_HALF, 1))
        return t * c_tab + other * s_tab

    kr = rope(kr)
    scale = (MLA_NOPE + MLA_ROPE) ** -0.5 * LOG2E
    for hh in range(HEADS):
        sl = slice(hh * LANES, (hh + 1) * LANES)
        q_ref[:, sl] = (rope(q[:, sl]) * scale).astype(BF16)
        k_ref[:, sl] = (kn[:, sl] + kr).astype(BF16)


def _vt_spec(tm, seq, width):
    per = seq // tm
    return pl.BlockSpec((1, width, tm), lambda i: (i // per, 0, i % per))


def _mla_proj(h, tabs, g, w, tm, batch, seq):
    n, d = h.shape
    row = lambda i: (i, 0)
    width = HEADS * HEAD_DIM
    return pl.pallas_call(
        _mla_proj_body,
        grid=(n // tm,),
        in_specs=[
            pl.BlockSpec((tm, d), row),
            pl.BlockSpec((tm, LANES), row),
            pl.BlockSpec((tm, LANES), row),
            _const_spec((1, d)),
            _const_spec(w["wa"].shape),
            _const_spec((1, MLA_Q_RANK)), _const_spec((1, MLA_KV_RANK)),
            _const_spec(w["wuq"].shape), _const_spec(w["wuk"].shape), _const_spec(w["wuv"].shape),
        ],
        out_specs=[
            pl.BlockSpec((tm, HEADS * LANES), row),
            pl.BlockSpec((tm, HEADS * LANES), row),
            _vt_spec(tm, seq, width),
        ],
        out_shape=[
            jax.ShapeDtypeStruct((n, HEADS * LANES), BF16),
            jax.ShapeDtypeStruct((n, HEADS * LANES), BF16),
            jax.ShapeDtypeStruct((batch, width, seq), BF16),
        ],
        compiler_params=_params(("parallel",)),
        name="mla_proj",
    )(h, tabs[0], tabs[1], g, w["wa"], w["qn"], w["kvn"],
      w["wuq"], w["wuk"], w["wuv"])


def _fox_proj_body(h_ref, g_ref, w_ref, scale_ref, wf_ref, bf_ref, a_ref, vt_ref, logf_ref):
    xn = _rms(h_ref[...], g_ref[...]).astype(BF16)
    nqk = a_ref.shape[1]
    a_ref[...] = (_dot(xn, w_ref[:, :nqk]) * scale_ref[...]).astype(BF16)
    vt_ref[0] = _dot_tt(w_ref[:, nqk:], xn).astype(BF16)
    f = _dot(xn, wf_ref[...]) + bf_ref[...]
    logf_ref[...] = jnp.minimum(f, 0.0) - jnp.log1p(jnp.exp(-jnp.abs(f)))


def _fox_proj(h, g, w, tm, batch, seq):
    n, d = h.shape
    width = HEADS * HEAD_DIM
    nout = 2 * width
    row = lambda i: (i, 0)
    return pl.pallas_call(
        _fox_proj_body,
        grid=(n // tm,),
        in_specs=[
            pl.BlockSpec((tm, d), row),
            _const_spec((1, d)),
            _const_spec((d, 3 * width)),
            _const_spec((1, nout)),
            _const_spec((d, LANES)),
            _const_spec((1, LANES)),
        ],
        out_specs=[pl.BlockSpec((tm, nout), row), _vt_spec(tm, seq, width),
                   pl.BlockSpec((tm, LANES), row)],
        out_shape=[jax.ShapeDtypeStruct((n, nout), BF16),
                   jax.ShapeDtypeStruct((batch, width, seq), BF16),
                   jax.ShapeDtypeStruct((n, LANES), F32)],
        compiler_params=_params(("parallel",)),
        name="fox_proj",
    )(h, g, w["wqkv"], w["scale"], w["wf"], w["bf"])


def _cumsum_body(x_ref, c_ref, ct_ref, *, seq, blk):
    r = lax.broadcasted_iota(jnp.int32, (blk, blk), 0)
    c = lax.broadcasted_iota(jnp.int32, (blk, blk), 1)
    tri = (c <= r).astype(F32)
    carry = jnp.zeros((1, LANES), F32)
    for b in range(seq // blk):
        xs = x_ref[0, b * blk:(b + 1) * blk, :]
        cs = lax.dot_general(tri, xs, (((1,), (0,)), ((), ())), precision=lax.Precision.HIGHEST,
                             preferred_element_type=F32) + carry
        c_ref[0, b * blk:(b + 1) * blk, :] = cs
        carry = cs[blk - 1:blk, :]
    ct_ref[0] = c_ref[0].T


def _cumsum(logf, batch, seq):
    x = logf.reshape(batch, seq, LANES)
    return pl.pallas_call(
        functools.partial(_cumsum_body, seq=seq, blk=256),
        grid=(batch,),
        in_specs=[pl.BlockSpec((1, seq, LANES), lambda b: (b, 0, 0))],
        out_specs=[pl.BlockSpec((1, seq, LANES), lambda b: (b, 0, 0)),
                   pl.BlockSpec((1, LANES, seq), lambda b: (b, 0, 0))],
        out_shape=[jax.ShapeDtypeStruct((batch, seq, LANES), F32),
                   jax.ShapeDtypeStruct((batch, LANES, seq), F32)],
        compiler_params=_params(("parallel",)),
        name="fox_cumsum",
    )(x)


def _flash_body(*refs, seq, tq, tk, forget, shared_lanes):
    if forget:
        q_ref, k_ref, vt_ref, c_ref, ct_ref, o_ref, va_ref, ck_ref = refs
    else:
        q_ref, k_ref, vt_ref, o_ref, va_ref = refs
    pair = pl.program_id(1)
    lane = lax.broadcasted_iota(jnp.int32, (1, LANES), 1)
    low = lane < HEAD_DIM
    causal = (lax.broadcasted_iota(jnp.int32, (tk, tq), 0)
              <= lax.broadcasted_iota(jnp.int32, (tk, tq), 1))
    for e in range(2):
        va_ref[e, :HEAD_DIM, :] = vt_ref[0, e * HEAD_DIM:(e + 1) * HEAD_DIM, :]
        va_ref[e, HEAD_DIM:, :] = jnp.ones((ONES_ROWS, seq), BF16)
    if forget:
        for e in range(2):
            col = jnp.sum(jnp.where(lane == 2 * pair + e, c_ref[0], 0.0), axis=1, keepdims=True)
            ck_ref[e] = jnp.broadcast_to(col * LOG2E, (seq, tq))

    class Chain:
        pass

    def start(qi, e):
        ch = Chain()
        qs = qi * tq
        ch.qi, ch.e = qi, e
        if shared_lanes:
            qp = q_ref[0, qs:qs + tq, :]
            ch.q = jnp.where(low, qp, 0) if e == 0 else jnp.where(low, 0, qp)
            ch.ksl = slice(0, LANES)
        else:
            ch.ksl = slice(e * LANES, (e + 1) * LANES)
            ch.q = q_ref[0, qs:qs + tq, ch.ksl]
        if forget:
            ch.cq = ct_ref[0, pl.ds(2 * pair + e, 1), qs:qs + tq] * LOG2E
        ch.s, ch.m, ch.acc = [], None, None
        return ch

    def lanes_from(x, off, new):
        return new if off == 0 else jnp.concatenate([x[:, :off], new], axis=1)

    def pass1(ch, c):
        ks = c * tk
        off = max(ks - ch.qi * tq, 0)
        s = _dot_nt(k_ref[0, ks:ks + tk, ch.ksl], ch.q[off:])
        if forget:
            s = s + (ch.cq[:, off:] - ck_ref[ch.e, ks:ks + tk, :tq - off])
        if ks + tk > ch.qi * tq:
            s = jnp.where(causal[:, :tq - off], s, NEG)
        mc = jnp.max(s, axis=0, keepdims=True)
        ch.m = mc if ch.m is None else lanes_from(ch.m, off, jnp.maximum(ch.m[:, off:], mc))
        ch.s.append(s)

    def pass2(ch, c):
        ks = c * tk
        off = tq - ch.s[c].shape[1]
        p = jnp.exp2(ch.s[c] - ch.m[:, off:])
        ac = _dot(va_ref[ch.e, :, ks:ks + tk], p.astype(BF16))
        ch.acc = ac if ch.acc is None else lanes_from(ch.acc, off, ch.acc[:, off:] + ac)

    outs = []

    def finish(ch):
        outs.append(ch.acc[:HEAD_DIM] / ch.acc[HEAD_DIM:HEAD_DIM + 1])
        if ch.e == 1:
            qs = ch.qi * tq
            o_t = jnp.concatenate(outs, axis=0)
            o_ref[0, qs:qs + tq, :] = o_t.T.astype(o_ref.dtype)
            outs.clear()

    per = tq // tk
    prev = ()
    for qi in range(seq // tq + 1):
        cur = tuple(start(qi, e) for e in range(2)) if qi < seq // tq else ()
        n_cur, n_prev = (qi + 1) * per if cur else 0, qi * per
        for c in range(max(n_cur, n_prev)):
            if c < n_cur:
                for ch in cur:
                    pass1(ch, c)
            if c < n_prev:
                for ch in prev:
                    pass2(ch, c)
        for ch in prev:
            finish(ch)
        prev = cur


def _flash(q, k, vt, batch, seq, *, shared_lanes, k_off, forget=None, tq=256, tk=256):
    qk_w = LANES if shared_lanes else 2 * LANES
    in_specs = [
        pl.BlockSpec((1, seq, qk_w), lambda b, p: (b, 0, p)),
        pl.BlockSpec((1, seq, qk_w), lambda b, p: (b, 0, k_off + p)),
        pl.BlockSpec((1, LANES, seq), lambda b, p: (b, p, 0)),
    ]
    args = [q, k, vt]
    scratch = [pltpu.VMEM((2, HEAD_DIM + ONES_ROWS, seq), BF16)]
    if forget is not None:
        c, ct = forget
        in_specs += [pl.BlockSpec((1, seq, LANES), lambda b, p: (b, 0, 0)),
                     pl.BlockSpec((1, HEADS, seq), lambda b, p: (b, 0, 0))]
        args += [c, ct]
        scratch += [pltpu.VMEM((2, seq, tq), F32)]
    return pl.pallas_call(
        functools.partial(_flash_body, seq=seq, tq=tq, tk=tk, forget=forget is not None,
                          shared_lanes=shared_lanes),
        grid=(batch, PAIRS),
        in_specs=in_specs,
        out_specs=pl.BlockSpec((1, seq, LANES), lambda b, p: (b, 0, p)),
        out_shape=jax.ShapeDtypeStruct((batch, seq, HEADS * HEAD_DIM), BF16),
        scratch_shapes=scratch,
        compiler_params=_params(("parallel", "parallel")),
        name="flash_fox" if forget is not None else "flash_mla",
    )(*args)


def _t5_bucket_np(dist):
    max_exact = REL_BUCKETS // 2
    n = np.maximum(dist.astype(np.float32), np.float32(1.0))
    large = max_exact + (np.log(n / np.float32(max_exact)) / np.float32(np.log(REL_MAX_DIST / max_exact))
                         * np.float32(REL_BUCKETS - max_exact)).astype(np.int32)
    large = np.minimum(large, REL_BUCKETS - 1)
    return np.where(dist < max_exact, dist, large).astype(np.int32)


def _dil_bucket_map(dilation, use_prev):
    qry = np.arange(DIL_BLOCK)
    key = np.arange(2 * DIL_BLOCK)
    rel = DIL_BLOCK + qry[None, :] - key[:, None]
    bk = _t5_bucket_np(np.clip(rel, 0, None) * dilation)
    if not use_prev:
        bk = bk[DIL_BLOCK:]
    return np.concatenate([bk, bk], axis=1)


def _dil_body(tab_ref, bucket_ref, q_ref, k_ref, vt_ref, o_ref, lse_ref, bias_ref, *, span, n_blk, seq):
    blk = DIL_BLOCK
    use_prev = n_blk > 1
    nkeys = 2 * blk if use_prev else blk

    @pl.when(pl.program_id(0) == 0)
    def _():
        bk = bucket_ref[...]
        second = lax.broadcasted_iota(jnp.int32, (1, 2 * blk), 1) >= blk
        key = lax.broadcasted_iota(jnp.int32, (nkeys, 2 * blk), 0) + (0 if use_prev else blk)
        qry = lax.broadcasted_iota(jnp.int32, (nkeys, 2 * blk), 1) & (blk - 1)
        rel = blk + qry - key
        band = (rel >= 0) & (rel <= span)

        def fill(p, _):
            acc = jnp.zeros((nkeys, 2 * blk), F32)
            for bb in range(REL_BUCKETS):
                val = jnp.where(second, tab_ref[bb, 2 * p + 1], tab_ref[bb, 2 * p])
                acc = jnp.where(bk == bb, val * LOG2E, acc)
            bias_ref[0, p] = jnp.where(band & (key >= blk), acc, NEG)
            bias_ref[1, p] = jnp.where(band, acc, NEG)
            return 0

        lax.fori_loop(0, PAIRS, fill, 0)

    low = lax.broadcasted_iota(jnp.int32, (1, LANES), 1) < HEAD_DIM
    head_row = lax.broadcasted_iota(jnp.int32, (LANES, blk), 0)

    def block(s, _):
        cur = pl.ds(pl.multiple_of(s * blk, blk), blk)
        cls, rows = s // n_blk, pl.ds(pl.multiple_of((s % n_blk) * blk, blk), blk)
        if use_prev:
            n = s % n_blk
            prv = pl.ds(pl.multiple_of(jnp.where(n > 0, s - 1, s) * blk, blk), blk)
            table = jnp.where(n > 0, 1, 0)
        else:
            table = 1
        def logits(p):
            sl = slice(p * LANES, (p + 1) * LANES)
            qp = q_ref[0, cur, sl]
            qq = jnp.concatenate([jnp.where(low, qp, 0), jnp.where(low, 0, qp)], axis=0)
            if use_prev:
                kk = jnp.concatenate([k_ref[0, prv, sl], k_ref[0, cur, sl]], axis=0)
            else:
                kk = k_ref[0, cur, sl]
            return _dot_nt(kk, qq)

        def masked(p, raw):
            st = raw + bias_ref[table, p]
            return st, jnp.max(st, axis=0, keepdims=True)

        def weighted(p, st, m):
            sl = slice(p * LANES, (p + 1) * LANES)
            if use_prev:
                vv = jnp.concatenate([vt_ref[0, sl, prv], vt_ref[0, sl, cur]], axis=1)
            else:
                vv = vt_ref[0, sl, cur]
            vv = jnp.concatenate([vv, jnp.ones((ONES_ROWS, nkeys), BF16)], axis=0)
            return _dot(vv, jnp.exp2(st - m).astype(BF16))

        def finish(p, ot, m, lse_t):
            sl = slice(p * LANES, (p + 1) * LANES)
            l = ot[LANES:LANES + 1]
            ot = ot[:LANES] / l
            o_t = jnp.concatenate([ot[:HEAD_DIM, :blk], ot[HEAD_DIM:, blk:]], axis=0)
            o_ref[0, cls, rows, sl] = o_t.T.astype(o_ref.dtype)
            lse = m * (1.0 / LOG2E) + jnp.log(l)
            return jnp.where(head_row == 2 * p, lse[:, :blk],
                             jnp.where(head_row == 2 * p + 1, lse[:, blk:], lse_t))

        raws = [logits(p) for p in range(PAIRS)]
        sms = [masked(p, raws[p]) for p in range(PAIRS)]
        ots = [weighted(p, *sms[p]) for p in range(PAIRS)]
        lse_t = jnp.zeros((LANES, blk), F32)
        for p in range(PAIRS):
            lse_t = finish(p, ots[p], sms[p][1], lse_t)
        lse_ref[0, cls, rows, :] = lse_t.T
        return 0

    lax.fori_loop(0, seq // blk, block, 0, unroll=4)


def _dil_attention(qk, vt, table, group, batch, seq):
    window, dilation = DIL_PATTERNS[group]
    span = window // dilation
    sub = seq // dilation
    assert sub % DIL_BLOCK == 0
    n_blk = sub // DIL_BLOCK
    width = HEADS * HEAD_DIM
    bucket = jnp.asarray(_dil_bucket_map(dilation, n_blk > 1))
    return pl.pallas_call(
        functools.partial(_dil_body, span=span, n_blk=n_blk, seq=seq),
        grid=(batch,),
        in_specs=[
            pl.BlockSpec(memory_space=pltpu.SMEM),
            _const_spec(bucket.shape),
            pl.BlockSpec((1, seq, width), lambda b: (b, 0, 0)),
            pl.BlockSpec((1, seq, width), lambda b: (b, 0, 1)),
            pl.BlockSpec((1, width, seq), lambda b: (b, 0, 0)),
        ],
        out_specs=[pl.BlockSpec((1, dilation, sub, width), lambda b: (b, 0, 0, 0)),
                   pl.BlockSpec((1, dilation, sub, LANES), lambda b: (b, 0, 0, 0))],
        out_shape=[jax.ShapeDtypeStruct((batch, dilation, sub, width), BF16),
                   jax.ShapeDtypeStruct((batch, dilation, sub, LANES), F32)],
        scratch_shapes=[pltpu.VMEM((2, PAIRS) + bucket.shape, F32)],
        compiler_params=_params(("arbitrary",)),
        name=f"dil_attn_g{group}",
    )(table, bucket, qk, qk, vt)


def _oproj_merge_body(o0_ref, o1_ref, o2_ref, l0_ref, l1_ref, l2_ref, spread_ref, w_ref, h_ref, g_ref,
                      out_ref, lse_ref, slab_ref, merged_ref):
    o_refs = (o0_ref, o1_ref, o2_ref)
    l_refs = (l0_ref, l1_ref, l2_ref)
    n_groups = len(DIL_PATTERNS)
    tm = h_ref.shape[0]
    lses = []
    for gi in range(n_groups):
        dil = DIL_PATTERNS[gi][1]
        if dil == 1:
            lses.append(l_refs[gi][0, 0])
            continue
        for r in range(dil):
            rows = pl.ds(r, tm // dil, stride=dil)
            lse_ref[gi, rows, :] = l_refs[gi][0, r]
            for s in range(PAIRS):
                slab_ref[gi, s, rows, :] = o_refs[gi][0, r, :, s * LANES:(s + 1) * LANES].astype(F32)
        lses.append(lse_ref[gi])
    mx = functools.reduce(jnp.maximum, lses)
    ex = [jnp.exp(t - mx) for t in lses]
    den = functools.reduce(jnp.add, ex)
    wide = []
    for t in ex:
        a = t / den
        a_hi = a.astype(BF16)
        a_lo = (a - a_hi.astype(F32)).astype(BF16)
        wide.append(_dot(a_hi, spread_ref[...]) + _dot(a_lo, spread_ref[...]))
    for p in range(PAIRS):
        sl = slice(p * LANES, (p + 1) * LANES)
        acc = jnp.zeros((tm, LANES), F32)
        for gi in range(n_groups):
            if DIL_PATTERNS[gi][1] == 1:
                acc = acc + wide[gi][:, sl] * o_refs[gi][0, 0, :, sl].astype(F32)
            else:
                acc = acc + wide[gi][:, sl] * slab_ref[gi, p]
        merged_ref[:, sl] = acc.astype(BF16)
    out_ref[...] = h_ref[...] + _rms(_dot(merged_ref[...], w_ref[...]), g_ref[...])


def _oproj_merge(os_, lses, w, h, g, tm, batch, seq):
    n, d = h.shape
    row = lambda i: (i, 0)
    width = os_[0].shape[-1]
    per = seq // tm
    n_groups = len(DIL_PATTERNS)

    def grouped_spec(dil, cols):
        return pl.BlockSpec((1, dil, tm // dil, cols), lambda i: (i // per, 0, i % per, 0))

    dils = [dil for _, dil in DIL_PATTERNS]
    spread = np.zeros((LANES, width), np.float32)
    for hh in range(HEADS):
        spread[hh, hh * HEAD_DIM:(hh + 1) * HEAD_DIM] = 1.0
    spread = jnp.asarray(spread, BF16)
    return pl.pallas_call(
        _oproj_merge_body,
        grid=(n // tm,),
        in_specs=[grouped_spec(dil, width) for dil in dils] + [grouped_spec(dil, LANES) for dil in dils]
        + [_const_spec(spread.shape), _const_spec(w.shape), pl.BlockSpec((tm, d), row), _const_spec((1, d))],
        out_specs=pl.BlockSpec((tm, d), row),
        out_shape=jax.ShapeDtypeStruct((n, d), F32),
        scratch_shapes=[pltpu.VMEM((n_groups, tm, LANES), F32),
                        pltpu.VMEM((n_groups, PAIRS, tm, LANES), F32),
                        pltpu.VMEM((tm, width), BF16)],
        compiler_params=_params(("parallel",)),
        name="oproj_merge",
    )(*os_, *lses, spread, w, h, g)


MXU_TILE = 256
FFN_SPLITS = (0, 6 * MXU_TILE, D_FF)


def _ffn_body(*refs, with_oproj):
    if with_oproj:
        o_ref, wattn_ref, g1_ref, *refs = refs
    h_ref, p_ref, g2_ref, g3_ref, win_ref, wo_ref, wproj_ref, wgate_ref, out_ref = refs
    h = h_ref[...]
    if with_oproj:
        h = h + _rms(_dot(o_ref[...], wattn_ref[0]), g1_ref[...])
    xn = _rms(h, g2_ref[...]).astype(BF16)
    y = None
    for a, b in zip(FFN_SPLITS[:-1], FFN_SPLITS[1:]):
        gate = _dot(xn, win_ref[0, :, a:b])
        up = _dot(xn, win_ref[0, :, D_FF + a:D_FF + b])
        act = (gate * jax.nn.sigmoid(gate) * up).astype(BF16)
        part = _dot(act, wo_ref[0, a:b, :])
        y = part if y is None else y + part
    h2 = h + _rms(y, g3_ref[...])
    emb = _dot(p_ref[0].astype(BF16), wproj_ref[0])
    out_ref[...] = h2 + emb * jax.nn.sigmoid(_dot(h2.astype(BF16), wgate_ref[0]))


def _layer_spec(stacked, layer):
    nd = stacked.ndim - 1
    return pl.BlockSpec((1,) + stacked.shape[1:], lambda *_: (layer,) + (0,) * nd,
                        pipeline_mode=pl.Buffered(1))


def _ffn(h, p, layer, g2, g3, w_in, w_out, w_proj, w_gate, tm, oproj=None):
    n, d = h.shape
    row = lambda i: (i, 0)
    head_specs, head_args = [], []
    if oproj is not None:
        o, w_attn, attn_layer, g1 = oproj
        head_specs = [pl.BlockSpec((tm, o.shape[1]), row), _layer_spec(w_attn, attn_layer),
                      _const_spec((1, d))]
        head_args = [o, w_attn, g1]
    return pl.pallas_call(
        functools.partial(_ffn_body, with_oproj=oproj is not None),
        grid=(n // tm,),
        in_specs=head_specs + [
            pl.BlockSpec((tm, d), row),
            pl.BlockSpec((1, tm, D_PLE), lambda i: (layer, i, 0)),
            _const_spec((1, d)),
            _const_spec((1, d)),
            _layer_spec(w_in, layer),
            _layer_spec(w_out, layer),
            _layer_spec(w_proj, layer),
            _layer_spec(w_gate, layer),
        ],
        out_specs=pl.BlockSpec((tm, d), row),
        out_shape=jax.ShapeDtypeStruct((n, d), F32),
        compiler_params=_params(("parallel",)),
        name="ffn",
    )(*head_args, h, p, g2, g3, w_in, w_out, w_proj, w_gate)


def _mla_weights(w_a, q_norm, kv_norm, w_uq, w_ukv):
    rank = MLA_Q_RANK + MLA_KV_RANK
    wa = jnp.pad(w_a, ((0, 0), (0, LANES - MLA_ROPE)))
    wa = jnp.concatenate([wa[:, :rank], jnp.roll(wa[:, rank:], MLA_NOPE, axis=1)], axis=1).astype(BF16)
    uq = w_uq.reshape(MLA_Q_RANK, HEADS, MLA_NOPE + MLA_ROPE)
    wuq = jnp.pad(uq, ((0, 0), (0, 0), (0, LANES - MLA_NOPE - MLA_ROPE)))
    wuq = wuq.reshape(MLA_Q_RANK, HEADS * LANES).astype(BF16)
    ukv = w_ukv.reshape(MLA_KV_RANK, HEADS, MLA_NOPE + HEAD_DIM)
    wuk = jnp.pad(ukv[:, :, :MLA_NOPE], ((0, 0), (0, 0), (0, LANES - MLA_NOPE)))
    wuk = wuk.reshape(MLA_KV_RANK, HEADS * LANES).astype(BF16)
    wuv = ukv[:, :, MLA_NOPE:].reshape(MLA_KV_RANK, HEADS * HEAD_DIM).astype(BF16)
    return dict(wa=wa, qn=q_norm.reshape(1, -1), kvn=kv_norm.reshape(1, -1),
                wuq=wuq, wuk=wuk, wuv=wuv)


def _q_scale_row(n_cols, q_starts, width, scale):
    row = np.ones((1, n_cols), np.float32)
    for s in q_starts:
        row[0, s:s + width] = scale
    return jnp.asarray(row)


def kernel(x, p, positions, norm_g, ffn_w_in, ffn_w_out, ple_w_proj, ple_w_gate, rel_bias, mla_w_a, mla_q_norm, mla_kv_norm, mla_w_uq, mla_w_ukv, mla_w_o, dil_w_qkv, dil_w_o, fox_w_qkvf, fox_b_f, fox_w_o):
    batch, seq, d = x.shape
    n = batch * seq
    inner = HEADS * HEAD_DIM
    h = x.reshape(n, d)
    rope_tabs = _rope_tables(positions, ROW_TILE)
    w_in, w_out = ffn_w_in.astype(BF16), ffn_w_out.astype(BF16)
    w_proj, w_gate = ple_w_proj.astype(BF16), ple_w_gate.astype(BF16)
    mla_wo, fox_wo = mla_w_o.astype(BF16), fox_w_o.astype(BF16)
    p_rows = p.reshape(N_LAYERS, n, D_PLE)
    for i in range(N_LAYERS):
        mixer, j = i % N_MIXERS, i // N_MIXERS
        g = norm_g[i].reshape(4, 1, d)
        if mixer == 0:
            w = _mla_weights(mla_w_a[j], mla_q_norm[j], mla_kv_norm[j], mla_w_uq[j], mla_w_ukv[j])
            q, k, vt = _mla_proj(h, rope_tabs, g[0], w, ROW_TILE, batch, seq)
            o = _flash(q.reshape(batch, seq, -1), k.reshape(batch, seq, -1), vt, batch, seq,
                       shared_lanes=False, k_off=0, tq=512)
            oproj = (o.reshape(n, inner), mla_wo, j, g[1])
        elif mixer == 1:
            table = rel_bias.reshape(REL_BUCKETS, len(DIL_PATTERNS), HEADS)
            h3, wq = h.reshape(batch, seq, d), dil_w_qkv[j].astype(BF16)
            outs, lses = [], []
            for gi in range(len(DIL_PATTERNS)):
                qk, vt = _dil_proj(h3, g[0], wq, gi)
                o, lse = _dil_attention(qk, vt, table[:, gi], gi, batch, seq)
                outs.append(o)
                lses.append(lse)
            h = _oproj_merge(outs, lses, dil_w_o[j].astype(BF16), h, g[1], ROW_TILE, batch, seq)
            oproj = None
        else:
            wq = fox_w_qkvf[j]
            w = dict(
                wqkv=wq[:, :3 * inner].astype(BF16),
                scale=_q_scale_row(2 * inner, [0], inner, HEAD_DIM ** -0.5 * LOG2E),
                wf=jnp.pad(wq[:, 3 * inner:], ((0, 0), (0, LANES - HEADS))).astype(BF16),
                bf=jnp.pad(fox_b_f[j], (0, LANES - HEADS)).reshape(1, LANES),
            )
            a, vt, logf = _fox_proj(h, g[0], w, ROW_TILE, batch, seq)
            c, ct = _cumsum(logf, batch, seq)
            a3 = a.reshape(batch, seq, 2 * inner)
            o = _flash(a3, a3, vt, batch, seq, shared_lanes=True, k_off=PAIRS, forget=(c, ct))
            oproj = (o.reshape(n, inner), fox_wo, j, g[1])
        h = _ffn(h, p_rows, i, g[2], g[3], w_in, w_out, w_proj, w_gate, tm=ROW_TILE, oproj=oproj)
    return h.reshape(batch, seq, d)
```

```python
import functools

import numpy as np
import jax
import jax.numpy as jnp
from jax import lax
from jax.experimental import pallas as pl
from jax.experimental.pallas import tpu as pltpu

F32 = jnp.float32
BF16 = jnp.bfloat16

D_MODEL = 1024
N_LAYERS = 4
N_MIXERS = 3
D_PLE = 256
EPS = 1e-6
NEG = -1e30
D_FF = 2816

HEADS = 16
HEAD_DIM = 64
LANES = 128
PAIRS = HEADS // 2
ONES_ROWS = 16
LOG2E = 1.4426950408889634

MLA_Q_RANK = 384
MLA_KV_RANK = 256
MLA_NOPE = 64
MLA_ROPE = 32
ROPE_HALF = MLA_ROPE // 2
ROPE_THETA = 10000.0

DIL_PATTERNS = ((128, 1), (512, 4), (2048, 16))
DIL_BLOCK = 128
REL_BUCKETS = 32
REL_MAX_DIST = 2048

VMEM_LIMIT = 56 * 1024 * 1024
ROW_TILE = 512


def _params(sem):
    return pltpu.CompilerParams(dimension_semantics=sem, vmem_limit_bytes=VMEM_LIMIT)


def _rms(x, g):
    y = x * lax.rsqrt(jnp.mean(x * x, axis=-1, keepdims=True) + EPS)
    return y * g


def _dot(a, b):
    return jnp.dot(a, b, preferred_element_type=F32)


def _dot_nt(a, b):
    return lax.dot_general(a, b, (((1,), (1,)), ((), ())), preferred_element_type=F32)


def _dot_tt(a, b):
    return lax.dot_general(a, b, (((0,), (1,)), ((), ())), preferred_element_type=F32)


def _const_spec(shape):
    nd = len(shape)
    return pl.BlockSpec(shape, lambda *_: (0,) * nd)


ROW_CHUNK = 256


def _dil_proj_body(h_ref, g_ref, w_ref, qk_ref, vt_ref, slab_ref, perm_ref, *, dilation, seq):
    c = pl.program_id(1)
    n_slab = D_MODEL // LANES

    @pl.when(c == 0)
    def _():
        part = slab_ref.shape[1]
        for h0 in range(0, seq, part):
            def norm_rows(i, _):
                rows = pl.multiple_of(i * ROW_CHUNK, ROW_CHUNK)
                xn = _rms(h_ref[0, pl.ds(h0 + rows, ROW_CHUNK), :], g_ref[...])
                if dilation == 1:
                    perm_ref[pl.ds(h0 + rows, ROW_CHUNK), :] = xn.astype(BF16)
                else:
                    for s in range(n_slab):
                        slab_ref[s, pl.ds(rows, ROW_CHUNK), :] = xn[:, s * LANES:(s + 1) * LANES]
                return 0

            lax.fori_loop(0, part // ROW_CHUNK, norm_rows, 0)
            if dilation > 1:
                sub, cnt = seq // dilation, part // dilation
                for r in range(dilation):
                    dst = r * sub + h0 // dilation
                    for s in range(n_slab):
                        perm_ref[dst:dst + cnt, s * LANES:(s + 1) * LANES] = (
                            slab_ref[s, pl.ds(r, cnt, stride=dilation), :].astype(BF16))

    @pl.when(c == 0)
    def _():
        qk_ref[0] = (_dot(perm_ref[...], w_ref[...]) * (HEAD_DIM ** -0.5 * LOG2E)).astype(BF16)

    @pl.when(c == 1)
    def _():
        qk_ref[0] = _dot(perm_ref[...], w_ref[...]).astype(BF16)

    @pl.when(c == 2)
    def _():
        vt_ref[0] = _dot_tt(w_ref[...], perm_ref[...]).astype(BF16)


def _dil_proj(h3, g, w, group):
    batch, seq, d = h3.shape
    dilation = DIL_PATTERNS[group][1]
    width = HEADS * HEAD_DIM
    return pl.pallas_call(
        functools.partial(_dil_proj_body, dilation=dilation, seq=seq),
        grid=(batch, 3),
        in_specs=[
            pl.BlockSpec((1, seq, d), lambda b, c: (b, 0, 0)),
            pl.BlockSpec((1, d), lambda b, c: (0, 0)),
            pl.BlockSpec((d, width), lambda b, c: (0, group * 3 + c)),
        ],
        out_specs=[
            pl.BlockSpec((1, seq, width), lambda b, c: (b, 0, jnp.minimum(c, 1))),
            pl.BlockSpec((1, width, seq), lambda b, c: (b, 0, 0)),
        ],
        out_shape=[jax.ShapeDtypeStruct((batch, seq, 2 * width), BF16),
                   jax.ShapeDtypeStruct((batch, width, seq), BF16)],
        scratch_shapes=[pltpu.VMEM((d // LANES, seq // 2, LANES), F32), pltpu.VMEM((seq, d), BF16)],
        compiler_params=_params(("arbitrary", "arbitrary")),
        name=f"dil_proj_g{group}",
    )(h3, g, w)


def _rope_table_body(pos_ref, inv_ref, c_ref, s_ref):
    tm = pos_ref.shape[-1]
    ang = inv_ref[...] * pos_ref[0].astype(F32)
    cos, sin = jnp.cos(ang), jnp.sin(ang)
    pad = LANES - MLA_NOPE - MLA_ROPE
    c_t = jnp.concatenate([jnp.ones((MLA_NOPE, tm), F32), cos, jnp.zeros((pad, tm), F32)], axis=0)
    s_t = jnp.concatenate([jnp.zeros((MLA_NOPE, tm), F32), -sin[:ROPE_HALF], sin[ROPE_HALF:],
                           jnp.zeros((pad, tm), F32)], axis=0)
    c_ref[...] = c_t.T
    s_ref[...] = s_t.T


def _rope_tables(positions, tm):
    n = positions.size
    inv = ROPE_THETA ** (-jnp.arange(ROPE_HALF, dtype=F32) / ROPE_HALF)
    inv_col = jnp.concatenate([inv, inv]).reshape(MLA_ROPE, 1)
    return pl.pallas_call(
        _rope_table_body,
        grid=(n // tm,),
        in_specs=[pl.BlockSpec((1, 1, tm), lambda i: (i, 0, 0)), _const_spec((MLA_ROPE, 1))],
        out_specs=[pl.BlockSpec((tm, LANES), lambda i: (i, 0))] * 2,
        out_shape=[jax.ShapeDtypeStruct((n, LANES), F32)] * 2,
        compiler_params=_params(("parallel",)),
        name="rope_tables",
    )(positions.reshape(n // tm, 1, tm), inv_col)


def _mla_proj_body(h_ref, ctab_ref, stab_ref, g_ref, wa_ref, qn_ref, kvn_ref,
                   wuq_ref, wuk_ref, wuv_ref, q_ref, k_ref, vt_ref):
    xn = _rms(h_ref[...], g_ref[...]).astype(BF16)
    a = _dot(xn, wa_ref[...])
    cq = _rms(a[:, :MLA_Q_RANK], qn_ref[...]).astype(BF16)
    ckv = _rms(a[:, MLA_Q_RANK:MLA_Q_RANK + MLA_KV_RANK], kvn_ref[...]).astype(BF16)
    kr = a[:, MLA_Q_RANK + MLA_KV_RANK:]
    q = _dot(cq, wuq_ref[...])
    kn = _dot(ckv, wuk_ref[...])
    vt_ref[0] = _dot_tt(wuv_ref[...], ckv).astype(BF16)

    lane = lax.broadcasted_iota(jnp.int32, (1, LANES), 1)
    first = (lane >= MLA_NOPE) & (lane < MLA_NOPE + ROPE_HALF)
    c_tab, s_tab = ctab_ref[...], stab_ref[...]

    def rope(t):
        other = jnp.where(first, pltpu.roll(t, LANES - ROPE_HALF, 1), pltpu.roll(t, ROPE_HALF, 1))
        return t * c_tab + other * s_tab

    kr = rope(kr)
    scale = (MLA_NOPE + MLA_ROPE) ** -0.5 * LOG2E
    for hh in range(HEADS):
        sl = slice(hh * LANES, (hh + 1) * LANES)
        q_ref[:, sl] = (rope(q[:, sl]) * scale).astype(BF16)
        k_ref[:, sl] = (kn[:, sl] + kr).astype(BF16)


def _vt_spec(tm, seq, width):
    per = seq // tm
    return pl.BlockSpec((1, width, tm), lambda i: (i // per, 0, i % per))


def _mla_proj(h, tabs, g, w, tm, batch, seq):
    n, d = h.shape
    row = lambda i: (i, 0)
    width = HEADS * HEAD_DIM
    return pl.pallas_call(
        _mla_proj_body,
        grid=(n // tm,),
        in_specs=[
            pl.BlockSpec((tm, d), row),
            pl.BlockSpec((tm, LANES), row),
            pl.BlockSpec((tm, LANES), row),
            _const_spec((1, d)),
            _const_spec(w["wa"].shape),
            _const_spec((1, MLA_Q_RANK)), _const_spec((1, MLA_KV_RANK)),
            _const_spec(w["wuq"].shape), _const_spec(w["wuk"].shape), _const_spec(w["wuv"].shape),
        ],
        out_specs=[
            pl.BlockSpec((tm, HEADS * LANES), row),
            pl.BlockSpec((tm, HEADS * LANES), row),
            _vt_spec(tm, seq, width),
        ],
        out_shape=[
            jax.ShapeDtypeStruct((n, HEADS * LANES), BF16),
            jax.ShapeDtypeStruct((n, HEADS * LANES), BF16),
            jax.ShapeDtypeStruct((batch, width, seq), BF16),
        ],
        compiler_params=_params(("parallel",)),
        name="mla_proj",
    )(h, tabs[0], tabs[1], g, w["wa"], w["qn"], w["kvn"],
      w["wuq"], w["wuk"], w["wuv"])


def _fox_proj_body(h_ref, g_ref, w_ref, scale_ref, wf_ref, bf_ref, a_ref, vt_ref, logf_ref):
    xn = _rms(h_ref[...], g_ref[...]).astype(BF16)
    nqk = a_ref.shape[1]
    a_ref[...] = (_dot(xn, w_ref[:, :nqk]) * scale_ref[...]).astype(BF16)
    vt_ref[0] = _dot_tt(w_ref[:, nqk:], xn).astype(BF16)
    f = _dot(xn, wf_ref[...]) + bf_ref[...]
    logf_ref[...] = jnp.minimum(f, 0.0) - jnp.log1p(jnp.exp(-jnp.abs(f)))


def _fox_proj(h, g, w, tm, batch, seq):
    n, d = h.shape
    width = HEADS * HEAD_DIM
    nout = 2 * width
    row = lambda i: (i, 0)
    return pl.pallas_call(
        _fox_proj_body,
        grid=(n // tm,),
        in_specs=[
            pl.BlockSpec((tm, d), row),
            _const_spec((1, d)),
            _const_spec((d, 3 * width)),
            _const_spec((1, nout)),
            _const_spec((d, LANES)),
            _const_spec((1, LANES)),
        ],
        out_specs=[pl.BlockSpec((tm, nout), row), _vt_spec(tm, seq, width),
                   pl.BlockSpec((tm, LANES), row)],
        out_shape=[jax.ShapeDtypeStruct((n, nout), BF16),
                   jax.ShapeDtypeStruct((batch, width, seq), BF16),
                   jax.ShapeDtypeStruct((n, LANES), F32)],
        compiler_params=_params(("parallel",)),
        name="fox_proj",
    )(h, g, w["wqkv"], w["scale"], w["wf"], w["bf"])


def _cumsum_body(x_ref, c_ref, ct_ref, *, seq, blk):
    r = lax.broadcasted_iota(jnp.int32, (blk, blk), 0)
    c = lax.broadcasted_iota(jnp.int32, (blk, blk), 1)
    tri = (c <= r).astype(F32)
    carry = jnp.zeros((1, LANES), F32)
    for b in range(seq // blk):
        xs = x_ref[0, b * blk:(b + 1) * blk, :]
        cs = lax.dot_general(tri, xs, (((1,), (0,)), ((), ())), precision=lax.Precision.HIGHEST,
                             preferred_element_type=F32) + carry
        c_ref[0, b * blk:(b + 1) * blk, :] = cs
        carry = cs[blk - 1:blk, :]
    ct_ref[0] = c_ref[0].T


def _cumsum(logf, batch, seq):
    x = logf.reshape(batch, seq, LANES)
    return pl.pallas_call(
        functools.partial(_cumsum_body, seq=seq, blk=256),
        grid=(batch,),
        in_specs=[pl.BlockSpec((1, seq, LANES), lambda b: (b, 0, 0))],
        out_specs=[pl.BlockSpec((1, seq, LANES), lambda b: (b, 0, 0)),
                   pl.BlockSpec((1, LANES, seq), lambda b: (b, 0, 0))],
        out_shape=[jax.ShapeDtypeStruct((batch, seq, LANES), F32),
                   jax.ShapeDtypeStruct((batch, LANES, seq), F32)],
        compiler_params=_params(("parallel",)),
        name="fox_cumsum",
    )(x)


def _flash_body(*refs, seq, tq, tk, forget, shared_lanes):
    if forget:
        q_ref, k_ref, vt_ref, c_ref, ct_ref, o_ref, va_ref, ck_ref = refs
    else:
        q_ref, k_ref, vt_ref, o_ref, va_ref = refs
    pair = pl.program_id(1)
    lane = lax.broadcasted_iota(jnp.int32, (1, LANES), 1)
    low = lane < HEAD_DIM
    causal = (lax.broadcasted_iota(jnp.int32, (tk, tq), 0)
              <= lax.broadcasted_iota(jnp.int32, (tk, tq), 1))
    for e in range(2):
        va_ref[e, :HEAD_DIM, :] = vt_ref[0, e * HEAD_DIM:(e + 1) * HEAD_DIM, :]
        va_ref[e, HEAD_DIM:, :] = jnp.ones((ONES_ROWS, seq), BF16)
    if forget:
        for e in range(2):
            col = jnp.sum(jnp.where(lane == 2 * pair + e, c_ref[0], 0.0), axis=1, keepdims=True)
            ck_ref[e] = jnp.broadcast_to(col * LOG2E, (seq, tq))

    class Chain:
        pass

    def start(qi, e):
        ch = Chain()
        qs = qi * tq
        ch.qi, ch.e = qi, e
        if shared_lanes:
            qp = q_ref[0, qs:qs + tq, :]
            ch.q = jnp.where(low, qp, 0) if e == 0 else jnp.where(low, 0, qp)
            ch.ksl = slice(0, LANES)
        else:
            ch.ksl = slice(e * LANES, (e + 1) * LANES)
            ch.q = q_ref[0, qs:qs + tq, ch.ksl]
        if forget:
            ch.cq = ct_ref[0, pl.ds(2 * pair + e, 1), qs:qs + tq] * LOG2E
        ch.s, ch.m, ch.acc = [], None, None
        return ch

    def lanes_from(x, off, new):
        return new if off == 0 else jnp.concatenate([x[:, :off], new], axis=1)

    def pass1(ch, c):
        ks = c * tk
        off = max(ks - ch.qi * tq, 0)
        s = _dot_nt(k_ref[0, ks:ks + tk, ch.ksl], ch.q[off:])
        if forget:
            s = s + (ch.cq[:, off:] - ck_ref[ch.e, ks:ks + tk, :tq - off])
        if ks + tk > ch.qi * tq:
            s = jnp.where(causal[:, :tq - off], s, NEG)
        mc = jnp.max(s, axis=0, keepdims=True)
        ch.m = mc if ch.m is None else lanes_from(ch.m, off, jnp.maximum(ch.m[:, off:], mc))
        ch.s.append(s)

    def pass2(ch, c):
        ks = c * tk
        off = tq - ch.s[c].shape[1]
        p = jnp.exp2(ch.s[c] - ch.m[:, off:])
        ac = _dot(va_ref[ch.e, :, ks:ks + tk], p.astype(BF16))
        ch.acc = ac if ch.acc is None else lanes_from(ch.acc, off, ch.acc[:, off:] + ac)

    outs = []

    def finish(ch):
        outs.append(ch.acc[:HEAD_DIM] / ch.acc[HEAD_DIM:HEAD_DIM + 1])
        if ch.e == 1:
            qs = ch.qi * tq
            o_t = jnp.concatenate(outs, axis=0)
            o_ref[0, qs:qs + tq, :] = o_t.T.astype(o_ref.dtype)
            outs.clear()

    per = tq // tk
    prev = ()
    for qi in range(seq // tq + 1):
        cur = tuple(start(qi, e) for e in range(2)) if qi < seq // tq else ()
        n_cur, n_prev = (qi + 1) * per if cur else 0, qi * per
        for c in range(max(n_cur, n_prev)):
            if c < n_cur:
                for ch in cur:
                    pass1(ch, c)
            if c < n_prev:
                for ch in prev:
                    pass2(ch, c)
        for ch in prev:
            finish(ch)
        prev = cur


def _flash(q, k, vt, batch, seq, *, shared_lanes, k_off, forget=None, tq=256, tk=256):
    qk_w = LANES if shared_lanes else 2 * LANES
    in_specs = [
        pl.BlockSpec((1, seq, qk_w), lambda b, p: (b, 0, p)),
        pl.BlockSpec((1, seq, qk_w), lambda b, p: (b, 0, k_off + p)),
        pl.BlockSpec((1, LANES, seq), lambda b, p: (b, p, 0)),
    ]
    args = [q, k, vt]
    scratch = [pltpu.VMEM((2, HEAD_DIM + ONES_ROWS, seq), BF16)]
    if forget is not None:
        c, ct = forget
        in_specs += [pl.BlockSpec((1, seq, LANES), lambda b, p: (b, 0, 0)),
                     pl.BlockSpec((1, HEADS, seq), lambda b, p: (b, 0, 0))]
        args += [c, ct]
        scratch += [pltpu.VMEM((2, seq, tq), F32)]
    return pl.pallas_call(
        functools.partial(_flash_body, seq=seq, tq=tq, tk=tk, forget=forget is not None,
                          shared_lanes=shared_lanes),
        grid=(batch, PAIRS),
        in_specs=in_specs,
        out_specs=pl.BlockSpec((1, seq, LANES), lambda b, p: (b, 0, p)),
        out_shape=jax.ShapeDtypeStruct((batch, seq, HEADS * HEAD_DIM), BF16),
        scratch_shapes=scratch,
        compiler_params=_params(("parallel", "parallel")),
        name="flash_fox" if forget is not None else "flash_mla",
    )(*args)


def _t5_bucket_np(dist):
    max_exact = REL_BUCKETS // 2
    n = np.maximum(dist.astype(np.float32), np.float32(1.0))
    large = max_exact + (np.log(n / np.float32(max_exact)) / np.float32(np.log(REL_MAX_DIST / max_exact))
                         * np.float32(REL_BUCKETS - max_exact)).astype(np.int32)
    large = np.minimum(large, REL_BUCKETS - 1)
    return np.where(dist < max_exact, dist, large).astype(np.int32)


def _dil_bucket_map(dilation, use_prev):
    qry = np.arange(DIL_BLOCK)
    key = np.arange(2 * DIL_BLOCK)
    rel = DIL_BLOCK + qry[None, :] - key[:, None]
    bk = _t5_bucket_np(np.clip(rel, 0, None) * dilation)
    if not use_prev:
        bk = bk[DIL_BLOCK:]
    return np.concatenate([bk, bk], axis=1)


def _dil_body(tab_ref, bucket_ref, q_ref, k_ref, vt_ref, o_ref, lse_ref, bias_ref, *, span, n_blk, seq):
    blk = DIL_BLOCK
    use_prev = n_blk > 1
    nkeys = 2 * blk if use_prev else blk

    @pl.when(pl.program_id(0) == 0)
    def _():
        bk = bucket_ref[...]
        second = lax.broadcasted_iota(jnp.int32, (1, 2 * blk), 1) >= blk
        key = lax.broadcasted_iota(jnp.int32, (nkeys, 2 * blk), 0) + (0 if use_prev else blk)
        qry = lax.broadcasted_iota(jnp.int32, (nkeys, 2 * blk), 1) & (blk - 1)
        rel = blk + qry - key
        band = (rel >= 0) & (rel <= span)

        def fill(p, _):
            acc = jnp.zeros((nkeys, 2 * blk), F32)
            for bb in range(REL_BUCKETS):
                val = jnp.where(second, tab_ref[bb, 2 * p + 1], tab_ref[bb, 2 * p])
                acc = jnp.where(bk == bb, val * LOG2E, acc)
            bias_ref[0, p] = jnp.where(band & (key >= blk), acc, NEG)
            bias_ref[1, p] = jnp.where(band, acc, NEG)
            return 0

        lax.fori_loop(0, PAIRS, fill, 0)

    low = lax.broadcasted_iota(jnp.int32, (1, LANES), 1) < HEAD_DIM
    head_row = lax.broadcasted_iota(jnp.int32, (LANES, blk), 0)

    def block(s, _):
        cur = pl.ds(pl.multiple_of(s * blk, blk), blk)
        cls, rows = s // n_blk, pl.ds(pl.multiple_of((s % n_blk) * blk, blk), blk)
        if use_prev:
            n = s % n_blk
            prv = pl.ds(pl.multiple_of(jnp.where(n > 0, s - 1, s) * blk, blk), blk)
            table = jnp.where(n > 0, 1, 0)
        else:
            table = 1
        def logits(p):
            sl = slice(p * LANES, (p + 1) * LANES)
            qp = q_ref[0, cur, sl]
            qq = jnp.concatenate([jnp.where(low, qp, 0), jnp.where(low, 0, qp)], axis=0)
            if use_prev:
                kk = jnp.concatenate([k_ref[0, prv, sl], k_ref[0, cur, sl]], axis=0)
            else:
                kk = k_ref[0, cur, sl]
            return _dot_nt(kk, qq)

        def masked(p, raw):
            st = raw + bias_ref[table, p]
            return st, jnp.max(st, axis=0, keepdims=True)

        def weighted(p, st, m):
            sl = slice(p * LANES, (p + 1) * LANES)
            if use_prev:
                vv = jnp.concatenate([vt_ref[0, sl, prv], vt_ref[0, sl, cur]], axis=1)
            else:
                vv = vt_ref[0, sl, cur]
            vv = jnp.concatenate([vv, jnp.ones((ONES_ROWS, nkeys), BF16)], axis=0)
            return _dot(vv, jnp.exp2(st - m).astype(BF16))

        def finish(p, ot, m, lse_t):
            sl = slice(p * LANES, (p + 1) * LANES)
            l = ot[LANES:LANES + 1]
            ot = ot[:LANES] / l
            o_t = jnp.concatenate([ot[:HEAD_DIM, :blk], ot[HEAD_DIM:, blk:]], axis=0)
            o_ref[0, cls, rows, sl] = o_t.T.astype(o_ref.dtype)
            lse = m * (1.0 / LOG2E) + jnp.log(l)
            return jnp.where(head_row == 2 * p, lse[:, :blk],
                             jnp.where(head_row == 2 * p + 1, lse[:, blk:], lse_t))

        raws = [logits(p) for p in range(PAIRS)]
        sms = [masked(p, raws[p]) for p in range(PAIRS)]
        ots = [weighted(p, *sms[p]) for p in range(PAIRS)]
        lse_t = jnp.zeros((LANES, blk), F32)
        for p in range(PAIRS):
            lse_t = finish(p, ots[p], sms[p][1], lse_t)
        lse_ref[0, cls, rows, :] = lse_t.T
        return 0

    lax.fori_loop(0, seq // blk, block, 0, unroll=8)


def _dil_attention(qk, vt, table, group, batch, seq):
    window, dilation = DIL_PATTERNS[group]
    span = window // dilation
    sub = seq // dilation
    assert sub % DIL_BLOCK == 0
    n_blk = sub // DIL_BLOCK
    width = HEADS * HEAD_DIM
    bucket = jnp.asarray(_dil_bucket_map(dilation, n_blk > 1))
    return pl.pallas_call(
        functools.partial(_dil_body, span=span, n_blk=n_blk, seq=seq),
        grid=(batch,),
        in_specs=[
            pl.BlockSpec(memory_space=pltpu.SMEM),
            _const_spec(bucket.shape),
            pl.BlockSpec((1, seq, width), lambda b: (b, 0, 0)),
            pl.BlockSpec((1, seq, width), lambda b: (b, 0, 1)),
            pl.BlockSpec((1, width, seq), lambda b: (b, 0, 0)),
        ],
        out_specs=[pl.BlockSpec((1, dilation, sub, width), lambda b: (b, 0, 0, 0)),
                   pl.BlockSpec((1, dilation, sub, LANES), lambda b: (b, 0, 0, 0))],
        out_shape=[jax.ShapeDtypeStruct((batch, dilation, sub, width), BF16),
                   jax.ShapeDtypeStruct((batch, dilation, sub, LANES), F32)],
        scratch_shapes=[pltpu.VMEM((2, PAIRS) + bucket.shape, F32)],
        compiler_params=_params(("arbitrary",)),
        name=f"dil_attn_g{group}",
    )(table, bucket, qk, qk, vt)


def _oproj_merge_body(o0_ref, o1_ref, o2_ref, l0_ref, l1_ref, l2_ref, spread_ref, w_ref, h_ref, g_ref,
                      out_ref, lse_ref, slab_ref, merged_ref):
    o_refs = (o0_ref, o1_ref, o2_ref)
    l_refs = (l0_ref, l1_ref, l2_ref)
    n_groups = len(DIL_PATTERNS)
    tm = h_ref.shape[0]
    lses = []
    for gi in range(n_groups):
        dil = DIL_PATTERNS[gi][1]
        if dil == 1:
            lses.append(l_refs[gi][0, 0])
            continue
        for r in range(dil):
            rows = pl.ds(r, tm // dil, stride=dil)
            lse_ref[gi, rows, :] = l_refs[gi][0, r]
            for s in range(PAIRS):
                slab_ref[gi, s, rows, :] = o_refs[gi][0, r, :, s * LANES:(s + 1) * LANES].astype(F32)
        lses.append(lse_ref[gi])
    mx = functools.reduce(jnp.maximum, lses)
    ex = [jnp.exp(t - mx) for t in lses]
    den = functools.reduce(jnp.add, ex)
    wide = []
    for t in ex:
        a = t / den
        a_hi = a.astype(BF16)
        a_lo = (a - a_hi.astype(F32)).astype(BF16)
        wide.append(_dot(jnp.concatenate([a_hi, a_lo], axis=1), spread_ref[...]))
    for p in range(PAIRS):
        sl = slice(p * LANES, (p + 1) * LANES)
        acc = jnp.zeros((tm, LANES), F32)
        for gi in range(n_groups):
            if DIL_PATTERNS[gi][1] == 1:
                acc = acc + wide[gi][:, sl] * o_refs[gi][0, 0, :, sl].astype(F32)
            else:
                acc = acc + wide[gi][:, sl] * slab_ref[gi, p]
        merged_ref[:, sl] = acc.astype(BF16)
    out_ref[...] = h_ref[...] + _rms(_dot(merged_ref[...], w_ref[...]), g_ref[...])


def _oproj_merge(os_, lses, w, h, g, tm, batch, seq):
    n, d = h.shape
    row = lambda i: (i, 0)
    width = os_[0].shape[-1]
    per = seq // tm
    n_groups = len(DIL_PATTERNS)

    def grouped_spec(dil, cols):
        return pl.BlockSpec((1, dil, tm // dil, cols), lambda i: (i // per, 0, i % per, 0))

    dils = [dil for _, dil in DIL_PATTERNS]
    spread = np.zeros((2, LANES, width), np.float32)
    for hh in range(HEADS):
        spread[:, hh, hh * HEAD_DIM:(hh + 1) * HEAD_DIM] = 1.0
    spread = jnp.asarray(spread.reshape(2 * LANES, width), BF16)
    return pl.pallas_call(
        _oproj_merge_body,
        grid=(n // tm,),
        in_specs=[grouped_spec(dil, width) for dil in dils] + [grouped_spec(dil, LANES) for dil in dils]
        + [_const_spec(spread.shape), _const_spec(w.shape), pl.BlockSpec((tm, d), row), _const_spec((1, d))],
        out_specs=pl.BlockSpec((tm, d), row),
        out_shape=jax.ShapeDtypeStruct((n, d), F32),
        scratch_shapes=[pltpu.VMEM((n_groups, tm, LANES), F32),
                        pltpu.VMEM((n_groups, PAIRS, tm, LANES), F32),
                        pltpu.VMEM((tm, width), BF16)],
        compiler_params=_params(("parallel",)),
        name="oproj_merge",
    )(*os_, *lses, spread, w, h, g)


MXU_TILE = 256
FFN_SPLITS = (0, 6 * MXU_TILE, D_FF)


def _ffn_body(*refs, with_oproj):
    if with_oproj:
        o_ref, wattn_ref, g1_ref, *refs = refs
    h_ref, p_ref, g2_ref, g3_ref, win_ref, wo_ref, wproj_ref, wgate_ref, out_ref = refs
    h = h_ref[...]
    if with_oproj:
        h = h + _rms(_dot(o_ref[...], wattn_ref[0]), g1_ref[...])
    xn = _rms(h, g2_ref[...]).astype(BF16)
    y = None
    for a, b in zip(FFN_SPLITS[:-1], FFN_SPLITS[1:]):
        gate = _dot(xn, win_ref[0, :, a:b])
        up = _dot(xn, win_ref[0, :, D_FF + a:D_FF + b])
        act = (gate * jax.nn.sigmoid(gate) * up).astype(BF16)
        part = _dot(act, wo_ref[0, a:b, :])
        y = part if y is None else y + part
    h2 = h + _rms(y, g3_ref[...])
    emb = _dot(p_ref[0].astype(BF16), wproj_ref[0])
    out_ref[...] = h2 + emb * jax.nn.sigmoid(_dot(h2.astype(BF16), wgate_ref[0]))


def _layer_spec(stacked, layer):
    nd = stacked.ndim - 1
    return pl.BlockSpec((1,) + stacked.shape[1:], lambda *_: (layer,) + (0,) * nd,
                        pipeline_mode=pl.Buffered(1))


def _ffn(h, p, layer, g2, g3, w_in, w_out, w_proj, w_gate, tm, oproj=None):
    n, d = h.shape
    row = lambda i: (i, 0)
    head_specs, head_args = [], []
    if oproj is not None:
        o, w_attn, attn_layer, g1 = oproj
        head_specs = [pl.BlockSpec((tm, o.shape[1]), row), _layer_spec(w_attn, attn_layer),
                      _const_spec((1, d))]
        head_args = [o, w_attn, g1]
    return pl.pallas_call(
        functools.partial(_ffn_body, with_oproj=oproj is not None),
        grid=(n // tm,),
        in_specs=head_specs + [
            pl.BlockSpec((tm, d), row),
            pl.BlockSpec((1, tm, D_PLE), lambda i: (layer, i, 0)),
            _const_spec((1, d)),
            _const_spec((1, d)),
            _layer_spec(w_in, layer),
            _layer_spec(w_out, layer),
            _layer_spec(w_proj, layer),
            _layer_spec(w_gate, layer),
        ],
        out_specs=pl.BlockSpec((tm, d), row),
        out_shape=jax.ShapeDtypeStruct((n, d), F32),
        compiler_params=_params(("parallel",)),
        name="ffn",
    )(*head_args, h, p, g2, g3, w_in, w_out, w_proj, w_gate)


def _mla_weights(w_a, q_norm, kv_norm, w_uq, w_ukv):
    rank = MLA_Q_RANK + MLA_KV_RANK
    wa = jnp.pad(w_a, ((0, 0), (0, LANES - MLA_ROPE)))
    wa = jnp.concatenate([wa[:, :rank], jnp.roll(wa[:, rank:], MLA_NOPE, axis=1)], axis=1).astype(BF16)
    uq = w_uq.reshape(MLA_Q_RANK, HEADS, MLA_NOPE + MLA_ROPE)
    wuq = jnp.pad(uq, ((0, 0), (0, 0), (0, LANES - MLA_NOPE - MLA_ROPE)))
    wuq = wuq.reshape(MLA_Q_RANK, HEADS * LANES).astype(BF16)
    ukv = w_ukv.reshape(MLA_KV_RANK, HEADS, MLA_NOPE + HEAD_DIM)
    wuk = jnp.pad(ukv[:, :, :MLA_NOPE], ((0, 0), (0, 0), (0, LANES - MLA_NOPE)))
    wuk = wuk.reshape(MLA_KV_RANK, HEADS * LANES).astype(BF16)
    wuv = ukv[:, :, MLA_NOPE:].reshape(MLA_KV_RANK, HEADS * HEAD_DIM).astype(BF16)
    return dict(wa=wa, qn=q_norm.reshape(1, -1), kvn=kv_norm.reshape(1, -1),
                wuq=wuq, wuk=wuk, wuv=wuv)


def _q_scale_row(n_cols, q_starts, width, scale):
    row = np.ones((1, n_cols), np.float32)
    for s in q_starts:
        row[0, s:s + width] = scale
    return jnp.asarray(row)


def kernel(x, p, positions, norm_g, ffn_w_in, ffn_w_out, ple_w_proj, ple_w_gate, rel_bias, mla_w_a, mla_q_norm, mla_kv_norm, mla_w_uq, mla_w_ukv, mla_w_o, dil_w_qkv, dil_w_o, fox_w_qkvf, fox_b_f, fox_w_o):
    batch, seq, d = x.shape
    n = batch * seq
    inner = HEADS * HEAD_DIM
    h = x.reshape(n, d)
    rope_tabs = _rope_tables(positions, ROW_TILE)
    w_in, w_out = ffn_w_in.astype(BF16), ffn_w_out.astype(BF16)
    w_proj, w_gate = ple_w_proj.astype(BF16), ple_w_gate.astype(BF16)
    mla_wo, fox_wo = mla_w_o.astype(BF16), fox_w_o.astype(BF16)
    p_rows = p.reshape(N_LAYERS, n, D_PLE)
    for i in range(N_LAYERS):
        mixer, j = i % N_MIXERS, i // N_MIXERS
        g = norm_g[i].reshape(4, 1, d)
        if mixer == 0:
            w = _mla_weights(mla_w_a[j], mla_q_norm[j], mla_kv_norm[j], mla_w_uq[j], mla_w_ukv[j])
            q, k, vt = _mla_proj(h, rope_tabs, g[0], w, ROW_TILE, batch, seq)
            o = _flash(q.reshape(batch, seq, -1), k.reshape(batch, seq, -1), vt, batch, seq,
                       shared_lanes=False, k_off=0, tq=512)
            oproj = (o.reshape(n, inner), mla_wo, j, g[1])
        elif mixer == 1:
            table = rel_bias.reshape(REL_BUCKETS, len(DIL_PATTERNS), HEADS)
            h3, wq = h.reshape(batch, seq, d), dil_w_qkv[j].astype(BF16)
            outs, lses = [], []
            for gi in range(len(DIL_PATTERNS)):
                qk, vt = _dil_proj(h3, g[0], wq, gi)
                o, lse = _dil_attention(qk, vt, table[:, gi], gi, batch, seq)
                outs.append(o)
                lses.append(lse)
            h = _oproj_merge(outs, lses, dil_w_o[j].astype(BF16), h, g[1], ROW_TILE, batch, seq)
            oproj = None
        else:
            wq = fox_w_qkvf[j]
            w = dict(
                wqkv=wq[:, :3 * inner].astype(BF16),
                scale=_q_scale_row(2 * inner, [0], inner, HEAD_DIM ** -0.5 * LOG2E),
                wf=jnp.pad(wq[:, 3 * inner:], ((0, 0), (0, LANES - HEADS))).astype(BF16),
                bf=jnp.pad(fox_b_f[j], (0, LANES - HEADS)).reshape(1, LANES),
            )
            a, vt, logf = _fox_proj(h, g[0], w, ROW_TILE, batch, seq)
            c, ct = _cumsum(logf, batch, seq)
            a3 = a.reshape(batch, seq, 2 * inner)
            o = _flash(a3, a3, vt, batch, seq, shared_lanes=True, k_off=PAIRS, forget=(c, ct))
            oproj = (o.reshape(n, inner), fox_wo, j, g[1])
        h = _ffn(h, p_rows, i, g[2], g[3], w_in, w_out, w_proj, w_gate, tm=ROW_TILE, oproj=oproj)
    return h.reshape(batch, seq, d)
```

```python
import functools

import numpy as np
import jax
import jax.numpy as jnp
from jax import lax
from jax.experimental import pallas as pl
from jax.experimental.pallas import tpu as pltpu

F32 = jnp.float32
BF16 = jnp.bfloat16

D_MODEL = 1024
N_LAYERS = 4
N_MIXERS = 3
D_PLE = 256
EPS = 1e-6
NEG = -1e30
D_FF = 2816

HEADS = 16
HEAD_DIM = 64
LANES = 128
PAIRS = HEADS // 2
ONES_ROWS = 16
LOG2E = 1.4426950408889634

MLA_Q_RANK = 384
MLA_KV_RANK = 256
MLA_NOPE = 64
MLA_ROPE = 32
ROPE_HALF = MLA_ROPE // 2
ROPE_THETA = 10000.0

DIL_PATTERNS = ((128, 1), (512, 4), (2048, 16))
DIL_BLOCK = 128
REL_BUCKETS = 32
REL_MAX_DIST = 2048

VMEM_LIMIT = 56 * 1024 * 1024
ROW_TILE = 512


def _params(sem):
    return pltpu.CompilerParams(dimension_semantics=sem, vmem_limit_bytes=VMEM_LIMIT)


def _rms(x, g):
    y = x * lax.rsqrt(jnp.mean(x * x, axis=-1, keepdims=True) + EPS)
    return y * g


def _dot(a, b):
    return jnp.dot(a, b, preferred_element_type=F32)


def _dot_nt(a, b):
    return lax.dot_general(a, b, (((1,), (1,)), ((), ())), preferred_element_type=F32)


def _dot_tt(a, b):
    return lax.dot_general(a, b, (((0,), (1,)), ((), ())), preferred_element_type=F32)


def _const_spec(shape):
    nd = len(shape)
    return pl.BlockSpec(shape, lambda *_: (0,) * nd)


ROW_CHUNK = 256


def _dil_proj_body(h_ref, g_ref, w_ref, qk_ref, vt_ref, slab_ref, slab2_ref, perm_ref, *, dilation, seq):
    c = pl.program_id(1)
    n_slab = D_MODEL // LANES

    @pl.when(c == 0)
    def _():
        part = slab_ref.shape[1]
        for h0 in range(0, seq, part):
            def norm_rows(i, _):
                rows = pl.multiple_of(i * ROW_CHUNK, ROW_CHUNK)
                xn = _rms(h_ref[0, pl.ds(h0 + rows, ROW_CHUNK), :], g_ref[...])
                if dilation == 1:
                    perm_ref[pl.ds(h0 + rows, ROW_CHUNK), :] = xn.astype(BF16)
                else:
                    for s in range(n_slab):
                        slab_ref[s, pl.ds(rows, ROW_CHUNK), :] = xn[:, s * LANES:(s + 1) * LANES]
                return 0

            lax.fori_loop(0, part // ROW_CHUNK, norm_rows, 0)
            if dilation > 1:
                sub, cnt = seq // dilation, part // dilation
                first = 4 if dilation > 4 else dilation
                rest = dilation // first
                if rest > 1:
                    for r0 in range(first):
                        for s in range(n_slab):
                            slab2_ref[s, r0 * (part // first):(r0 + 1) * (part // first), :] = (
                                slab_ref[s, pl.ds(r0, part // first, stride=first), :])
                for r in range(dilation):
                    dst = r * sub + h0 // dilation
                    r0, r1 = r % first, r // first
                    for s in range(n_slab):
                        if rest > 1:
                            src = slab2_ref[s, pl.ds(r0 * (part // first) + r1, cnt, stride=rest), :]
                        else:
                            src = slab_ref[s, pl.ds(r, cnt, stride=dilation), :]
                        perm_ref[dst:dst + cnt, s * LANES:(s + 1) * LANES] = src.astype(BF16)

    @pl.when(c == 0)
    def _():
        qk_ref[0] = (_dot(perm_ref[...], w_ref[...]) * (HEAD_DIM ** -0.5 * LOG2E)).astype(BF16)

    @pl.when(c == 1)
    def _():
        qk_ref[0] = _dot(perm_ref[...], w_ref[...]).astype(BF16)

    @pl.when(c == 2)
    def _():
        vt_ref[0] = _dot_tt(w_ref[...], perm_ref[...]).astype(BF16)


def _dil_proj(h3, g, w, group):
    batch, seq, d = h3.shape
    dilation = DIL_PATTERNS[group][1]
    width = HEADS * HEAD_DIM
    return pl.pallas_call(
        functools.partial(_dil_proj_body, dilation=dilation, seq=seq),
        grid=(batch, 3),
        in_specs=[
            pl.BlockSpec((1, seq, d), lambda b, c: (b, 0, 0)),
            pl.BlockSpec((1, d), lambda b, c: (0, 0)),
            pl.BlockSpec((d, width), lambda b, c: (0, group * 3 + c)),
        ],
        out_specs=[
            pl.BlockSpec((1, seq, width), lambda b, c: (b, 0, jnp.minimum(c, 1))),
            pl.BlockSpec((1, width, seq), lambda b, c: (b, 0, 0)),
        ],
        out_shape=[jax.ShapeDtypeStruct((batch, seq, 2 * width), BF16),
                   jax.ShapeDtypeStruct((batch, width, seq), BF16)],
        scratch_shapes=[pltpu.VMEM((d // LANES, seq // 4, LANES), F32),
                        pltpu.VMEM((d // LANES, seq // 4, LANES), F32), pltpu.VMEM((seq, d), BF16)],
        compiler_params=_params(("arbitrary", "arbitrary")),
        name=f"dil_proj_g{group}",
    )(h3, g, w)


def _rope_table_body(pos_ref, inv_ref, c_ref, s_ref):
    tm = pos_ref.shape[-1]
    ang = inv_ref[...] * pos_ref[0].astype(F32)
    cos, sin = jnp.cos(ang), jnp.sin(ang)
    pad = LANES - MLA_NOPE - MLA_ROPE
    c_t = jnp.concatenate([jnp.ones((MLA_NOPE, tm), F32), cos, jnp.zeros((pad, tm), F32)], axis=0)
    s_t = jnp.concatenate([jnp.zeros((MLA_NOPE, tm), F32), -sin[:ROPE_HALF], sin[ROPE_HALF:],
                           jnp.zeros((pad, tm), F32)], axis=0)
    c_ref[...] = c_t.T
    s_ref[...] = s_t.T


def _rope_tables(positions, tm):
    n = positions.size
    inv = ROPE_THETA ** (-jnp.arange(ROPE_HALF, dtype=F32) / ROPE_HALF)
    inv_col = jnp.concatenate([inv, inv]).reshape(MLA_ROPE, 1)
    return pl.pallas_call(
        _rope_table_body,
        grid=(n // tm,),
        in_specs=[pl.BlockSpec((1, 1, tm), lambda i: (i, 0, 0)), _const_spec((MLA_ROPE, 1))],
        out_specs=[pl.BlockSpec((tm, LANES), lambda i: (i, 0))] * 2,
        out_shape=[jax.ShapeDtypeStruct((n, LANES), F32)] * 2,
        compiler_params=_params(("parallel",)),
        name="rope_tables",
    )(positions.reshape(n // tm, 1, tm), inv_col)


def _mla_proj_body(h_ref, ctab_ref, stab_ref, g_ref, wa_ref, qn_ref, kvn_ref,
                   wuq_ref, wuk_ref, wuv_ref, q_ref, k_ref, vt_ref):
    xn = _rms(h_ref[...], g_ref[...]).astype(BF16)
    a = _dot(xn, wa_ref[...])
    cq = _rms(a[:, :MLA_Q_RANK], qn_ref[...]).astype(BF16)
    ckv = _rms(a[:, MLA_Q_RANK:MLA_Q_RANK + MLA_KV_RANK], kvn_ref[...]).astype(BF16)
    kr = a[:, MLA_Q_RANK + MLA_KV_RANK:]
    q = _dot(cq, wuq_ref[...])
    kn = _dot(ckv, wuk_ref[...])
    vt_ref[0] = _dot_tt(wuv_ref[...], ckv).astype(BF16)

    lane = lax.broadcasted_iota(jnp.int32, (1, LANES), 1)
    first = (lane >= MLA_NOPE) & (lane < MLA_NOPE + ROPE_HALF)
    c_tab, s_tab = ctab_ref[...], stab_ref[...]

    def rope(t):
        other = jnp.where(first, pltpu.roll(t, LANES - ROPE_HALF, 1), pltpu.roll(t, ROPE_HALF, 1))
        return t * c_tab + other * s_tab

    kr = rope(kr)
    scale = (MLA_NOPE + MLA_ROPE) ** -0.5 * LOG2E
    for hh in range(HEADS):
        sl = slice(hh * LANES, (hh + 1) * LANES)
        q_ref[:, sl] = (rope(q[:, sl]) * scale).astype(BF16)
        k_ref[:, sl] = (kn[:, sl] + kr).astype(BF16)


def _vt_spec(tm, seq, width):
    per = seq // tm
    return pl.BlockSpec((1, width, tm), lambda i: (i // per, 0, i % per))


def _mla_proj(h, tabs, g, w, tm, batch, seq):
    n, d = h.shape
    row = lambda i: (i, 0)
    width = HEADS * HEAD_DIM
    return pl.pallas_call(
        _mla_proj_body,
        grid=(n // tm,),
        in_specs=[
            pl.BlockSpec((tm, d), row),
            pl.BlockSpec((tm, LANES), row),
            pl.BlockSpec((tm, LANES), row),
            _const_spec((1, d)),
            _const_spec(w["wa"].shape),
            _const_spec((1, MLA_Q_RANK)), _const_spec((1, MLA_KV_RANK)),
            _const_spec(w["wuq"].shape), _const_spec(w["wuk"].shape), _const_spec(w["wuv"].shape),
        ],
        out_specs=[
            pl.BlockSpec((tm, HEADS * LANES), row),
            pl.BlockSpec((tm, HEADS * LANES), row),
            _vt_spec(tm, seq, width),
        ],
        out_shape=[
            jax.ShapeDtypeStruct((n, HEADS * LANES), BF16),
            jax.ShapeDtypeStruct((n, HEADS * LANES), BF16),
            jax.ShapeDtypeStruct((batch, width, seq), BF16),
        ],
        compiler_params=_params(("parallel",)),
        name="mla_proj",
    )(h, tabs[0], tabs[1], g, w["wa"], w["qn"], w["kvn"],
      w["wuq"], w["wuk"], w["wuv"])


def _fox_proj_body(h_ref, g_ref, w_ref, scale_ref, wf_ref, bf_ref, a_ref, vt_ref, logf_ref):
    xn = _rms(h_ref[...], g_ref[...]).astype(BF16)
    nqk = a_ref.shape[1]
    a_ref[...] = (_dot(xn, w_ref[:, :nqk]) * scale_ref[...]).astype(BF16)
    vt_ref[0] = _dot_tt(w_ref[:, nqk:], xn).astype(BF16)
    f = _dot(xn, wf_ref[...]) + bf_ref[...]
    logf_ref[...] = jnp.minimum(f, 0.0) - jnp.log1p(jnp.exp(-jnp.abs(f)))


def _fox_proj(h, g, w, tm, batch, seq):
    n, d = h.shape
    width = HEADS * HEAD_DIM
    nout = 2 * width
    row = lambda i: (i, 0)
    return pl.pallas_call(
        _fox_proj_body,
        grid=(n // tm,),
        in_specs=[
            pl.BlockSpec((tm, d), row),
            _const_spec((1, d)),
            _const_spec((d, 3 * width)),
            _const_spec((1, nout)),
            _const_spec((d, LANES)),
            _const_spec((1, LANES)),
        ],
        out_specs=[pl.BlockSpec((tm, nout), row), _vt_spec(tm, seq, width),
                   pl.BlockSpec((tm, LANES), row)],
        out_shape=[jax.ShapeDtypeStruct((n, nout), BF16),
                   jax.ShapeDtypeStruct((batch, width, seq), BF16),
                   jax.ShapeDtypeStruct((n, LANES), F32)],
        compiler_params=_params(("parallel",)),
        name="fox_proj",
    )(h, g, w["wqkv"], w["scale"], w["wf"], w["bf"])


def _cumsum_body(x_ref, c_ref, ct_ref, *, seq, blk):
    r = lax.broadcasted_iota(jnp.int32, (blk, blk), 0)
    c = lax.broadcasted_iota(jnp.int32, (blk, blk), 1)
    tri = (c <= r).astype(F32)
    carry = jnp.zeros((1, LANES), F32)
    for b in range(seq // blk):
        xs = x_ref[0, b * blk:(b + 1) * blk, :]
        cs = lax.dot_general(tri, xs, (((1,), (0,)), ((), ())), precision=lax.Precision.HIGHEST,
                             preferred_element_type=F32) + carry
        c_ref[0, b * blk:(b + 1) * blk, :] = cs
        carry = cs[blk - 1:blk, :]
    ct_ref[0] = c_ref[0].T


def _cumsum(logf, batch, seq):
    x = logf.reshape(batch, seq, LANES)
    return pl.pallas_call(
        functools.partial(_cumsum_body, seq=seq, blk=256),
        grid=(batch,),
        in_specs=[pl.BlockSpec((1, seq, LANES), lambda b: (b, 0, 0))],
        out_specs=[pl.BlockSpec((1, seq, LANES), lambda b: (b, 0, 0)),
                   pl.BlockSpec((1, LANES, seq), lambda b: (b, 0, 0))],
        out_shape=[jax.ShapeDtypeStruct((batch, seq, LANES), F32),
                   jax.ShapeDtypeStruct((batch, LANES, seq), F32)],
        compiler_params=_params(("parallel",)),
        name="fox_cumsum",
    )(x)


def _flash_body(*refs, seq, tq, tk, forget, shared_lanes):
    if forget:
        q_ref, k_ref, vt_ref, c_ref, ct_ref, o_ref, va_ref, ck_ref = refs
    else:
        q_ref, k_ref, vt_ref, o_ref, va_ref = refs
    pair = pl.program_id(1)
    lane = lax.broadcasted_iota(jnp.int32, (1, LANES), 1)
    low = lane < HEAD_DIM
    causal = (lax.broadcasted_iota(jnp.int32, (tk, tq), 0)
              <= lax.broadcasted_iota(jnp.int32, (tk, tq), 1))
    for e in range(2):
        va_ref[e, :HEAD_DIM, :] = vt_ref[0, e * HEAD_DIM:(e + 1) * HEAD_DIM, :]
        va_ref[e, HEAD_DIM:, :] = jnp.ones((ONES_ROWS, seq), BF16)
    if forget:
        for e in range(2):
            col = jnp.sum(jnp.where(lane == 2 * pair + e, c_ref[0], 0.0), axis=1, keepdims=True)
            ck_ref[e] = jnp.broadcast_to(col * LOG2E, (seq, tq))

    class Chain:
        pass

    def start(qi, e):
        ch = Chain()
        qs = qi * tq
        ch.qi, ch.e = qi, e
        if shared_lanes:
            qp = q_ref[0, qs:qs + tq, :]
            ch.q = jnp.where(low, qp, 0) if e == 0 else jnp.where(low, 0, qp)
            ch.ksl = slice(0, LANES)
        else:
            ch.ksl = slice(e * LANES, (e + 1) * LANES)
            ch.q = q_ref[0, qs:qs + tq, ch.ksl]
        if forget:
            ch.cq = ct_ref[0, pl.ds(2 * pair + e, 1), qs:qs + tq] * LOG2E
        ch.s, ch.m, ch.acc = [], None, None
        return ch

    def lanes_from(x, off, new):
        return new if off == 0 else jnp.concatenate([x[:, :off], new], axis=1)

    def pass1(ch, c):
        ks = c * tk
        off = max(ks - ch.qi * tq, 0)
        s = _dot_nt(k_ref[0, ks:ks + tk, ch.ksl], ch.q[off:])
        if forget:
            s = s + (ch.cq[:, off:] - ck_ref[ch.e, ks:ks + tk, :tq - off])
        if ks + tk > ch.qi * tq:
            s = jnp.where(causal[:, :tq - off], s, NEG)
        mc = jnp.max(s, axis=0, keepdims=True)
        ch.m = mc if ch.m is None else lanes_from(ch.m, off, jnp.maximum(ch.m[:, off:], mc))
        ch.s.append(s)

    def pass2(ch, c):
        ks = c * tk
        off = tq - ch.s[c].shape[1]
        p = jnp.exp2(ch.s[c] - ch.m[:, off:])
        ac = _dot(va_ref[ch.e, :, ks:ks + tk], p.astype(BF16))
        ch.acc = ac if ch.acc is None else lanes_from(ch.acc, off, ch.acc[:, off:] + ac)

    outs = []

    def finish(ch):
        outs.append(ch.acc[:HEAD_DIM] / ch.acc[HEAD_DIM:HEAD_DIM + 1])
        if ch.e == 1:
            qs = ch.qi * tq
            o_t = jnp.concatenate(outs, axis=0)
            o_ref[0, qs:qs + tq, :] = o_t.T.astype(o_ref.dtype)
            outs.clear()

    per = tq // tk
    prev = ()
    for qi in range(seq // tq + 1):
        cur = tuple(start(qi, e) for e in range(2)) if qi < seq // tq else ()
        n_cur, n_prev = (qi + 1) * per if cur else 0, qi * per
        for c in range(max(n_cur, n_prev)):
            if c < n_cur:
                for ch in cur:
                    pass1(ch, c)
            if c < n_prev:
                for ch in prev:
                    pass2(ch, c)
        for ch in prev:
            finish(ch)
        prev = cur


def _flash(q, k, vt, batch, seq, *, shared_lanes, k_off, forget=None, tq=256, tk=256):
    qk_w = LANES if shared_lanes else 2 * LANES
    in_specs = [
        pl.BlockSpec((1, seq, qk_w), lambda b, p: (b, 0, p)),
        pl.BlockSpec((1, seq, qk_w), lambda b, p: (b, 0, k_off + p)),
        pl.BlockSpec((1, LANES, seq), lambda b, p: (b, p, 0)),
    ]
    args = [q, k, vt]
    scratch = [pltpu.VMEM((2, HEAD_DIM + ONES_ROWS, seq), BF16)]
    if forget is not None:
        c, ct = forget
        in_specs += [pl.BlockSpec((1, seq, LANES), lambda b, p: (b, 0, 0)),
                     pl.BlockSpec((1, HEADS, seq), lambda b, p: (b, 0, 0))]
        args += [c, ct]
        scratch += [pltpu.VMEM((2, seq, tq), F32)]
    return pl.pallas_call(
        functools.partial(_flash_body, seq=seq, tq=tq, tk=tk, forget=forget is not None,
                          shared_lanes=shared_lanes),
        grid=(batch, PAIRS),
        in_specs=in_specs,
        out_specs=pl.BlockSpec((1, seq, LANES), lambda b, p: (b, 0, p)),
        out_shape=jax.ShapeDtypeStruct((batch, seq, HEADS * HEAD_DIM), BF16),
        scratch_shapes=scratch,
        compiler_params=_params(("parallel", "parallel")),
        name="flash_fox" if forget is not None else "flash_mla",
    )(*args)


def _t5_bucket_np(dist):
    max_exact = REL_BUCKETS // 2
    n = np.maximum(dist.astype(np.float32), np.float32(1.0))
    large = max_exact + (np.log(n / np.float32(max_exact)) / np.float32(np.log(REL_MAX_DIST / max_exact))
                         * np.float32(REL_BUCKETS - max_exact)).astype(np.int32)
    large = np.minimum(large, REL_BUCKETS - 1)
    return np.where(dist < max_exact, dist, large).astype(np.int32)


def _dil_bucket_map(dilation, use_prev):
    qry = np.arange(DIL_BLOCK)
    key = np.arange(2 * DIL_BLOCK)
    rel = DIL_BLOCK + qry[None, :] - key[:, None]
    bk = _t5_bucket_np(np.clip(rel, 0, None) * dilation)
    if not use_prev:
        bk = bk[DIL_BLOCK:]
    return np.concatenate([bk, bk], axis=1)


def _dil_body(tab_ref, bucket_ref, q_ref, k_ref, vt_ref, o_ref, lse_ref, bias_ref, *, span, n_blk, seq):
    blk = DIL_BLOCK
    use_prev = n_blk > 1
    nkeys = 2 * blk if use_prev else blk

    @pl.when(pl.program_id(0) == 0)
    def _():
        bk = bucket_ref[...]
        second = lax.broadcasted_iota(jnp.int32, (1, 2 * blk), 1) >= blk
        key = lax.broadcasted_iota(jnp.int32, (nkeys, 2 * blk), 0) + (0 if use_prev else blk)
        qry = lax.broadcasted_iota(jnp.int32, (nkeys, 2 * blk), 1) & (blk - 1)
        rel = blk + qry - key
        band = (rel >= 0) & (rel <= span)

        def fill(p, _):
            acc = jnp.zeros((nkeys, 2 * blk), F32)
            for bb in range(REL_BUCKETS):
                val = jnp.where(second, tab_ref[bb, 2 * p + 1], tab_ref[bb, 2 * p])
                acc = jnp.where(bk == bb, val * LOG2E, acc)
            bias_ref[0, p] = jnp.where(band & (key >= blk), acc, NEG)
            bias_ref[1, p] = jnp.where(band, acc, NEG)
            return 0

        lax.fori_loop(0, PAIRS, fill, 0)

    low = lax.broadcasted_iota(jnp.int32, (1, LANES), 1) < HEAD_DIM
    head_row = lax.broadcasted_iota(jnp.int32, (LANES, blk), 0)

    def block(s, _):
        cur = pl.ds(pl.multiple_of(s * blk, blk), blk)
        cls, rows = s // n_blk, pl.ds(pl.multiple_of((s % n_blk) * blk, blk), blk)
        if use_prev:
            n = s % n_blk
            prv = pl.ds(pl.multiple_of(jnp.where(n > 0, s - 1, s) * blk, blk), blk)
            table = jnp.where(n > 0, 1, 0)
        else:
            table = 1
        def logits(p):
            sl = slice(p * LANES, (p + 1) * LANES)
            qp = q_ref[0, cur, sl]
            qq = jnp.concatenate([jnp.where(low, qp, 0), jnp.where(low, 0, qp)], axis=0)
            if use_prev:
                kk = jnp.concatenate([k_ref[0, prv, sl], k_ref[0, cur, sl]], axis=0)
            else:
                kk = k_ref[0, cur, sl]
            return _dot_nt(kk, qq)

        def masked(p, raw):
            st = raw + bias_ref[table, p]
            return st, jnp.max(st, axis=0, keepdims=True)

        def weighted(p, st, m):
            sl = slice(p * LANES, (p + 1) * LANES)
            if use_prev:
                vv = jnp.concatenate([vt_ref[0, sl, prv], vt_ref[0, sl, cur]], axis=1)
            else:
                vv = vt_ref[0, sl, cur]
            vv = jnp.concatenate([vv, jnp.ones((ONES_ROWS, nkeys), BF16)], axis=0)
            return _dot(vv, jnp.exp2(st - m).astype(BF16))

        def finish(p, ot, m, lse_t):
            sl = slice(p * LANES, (p + 1) * LANES)
            l = ot[LANES:LANES + 1]
            ot = ot[:LANES] / l
            o_t = jnp.concatenate([ot[:HEAD_DIM, :blk], ot[HEAD_DIM:, blk:]], axis=0)
            o_ref[0, cls, rows, sl] = o_t.T.astype(o_ref.dtype)
            lse = m * (1.0 / LOG2E) + jnp.log(l)
            return jnp.where(head_row == 2 * p, lse[:, :blk],
                             jnp.where(head_row == 2 * p + 1, lse[:, blk:], lse_t))

        raws = [logits(p) for p in range(PAIRS)]
        sms = [masked(p, raws[p]) for p in range(PAIRS)]
        ots = [weighted(p, *sms[p]) for p in range(PAIRS)]
        lse_t = jnp.zeros((LANES, blk), F32)
        for p in range(PAIRS):
            lse_t = finish(p, ots[p], sms[p][1], lse_t)
        lse_ref[0, cls, rows, :] = lse_t.T
        return 0

    lax.fori_loop(0, seq // blk, block, 0, unroll=8)


def _dil_attention(qk, vt, table, group, batch, seq):
    window, dilation = DIL_PATTERNS[group]
    span = window // dilation
    sub = seq // dilation
    assert sub % DIL_BLOCK == 0
    n_blk = sub // DIL_BLOCK
    width = HEADS * HEAD_DIM
    bucket = jnp.asarray(_dil_bucket_map(dilation, n_blk > 1))
    return pl.pallas_call(
        functools.partial(_dil_body, span=span, n_blk=n_blk, seq=seq),
        grid=(batch,),
        in_specs=[
            pl.BlockSpec(memory_space=pltpu.SMEM),
            _const_spec(bucket.shape),
            pl.BlockSpec((1, seq, width), lambda b: (b, 0, 0)),
            pl.BlockSpec((1, seq, width), lambda b: (b, 0, 1)),
            pl.BlockSpec((1, width, seq), lambda b: (b, 0, 0)),
        ],
        out_specs=[pl.BlockSpec((1, dilation, sub, width), lambda b: (b, 0, 0, 0)),
                   pl.BlockSpec((1, dilation, sub, LANES), lambda b: (b, 0, 0, 0))],
        out_shape=[jax.ShapeDtypeStruct((batch, dilation, sub, width), BF16),
                   jax.ShapeDtypeStruct((batch, dilation, sub, LANES), F32)],
        scratch_shapes=[pltpu.VMEM((2, PAIRS) + bucket.shape, F32)],
        compiler_params=_params(("arbitrary",)),
        name=f"dil_attn_g{group}",
    )(table, bucket, qk, qk, vt)


def _oproj_merge_body(o0_ref, o1_ref, o2_ref, l0_ref, l1_ref, l2_ref, spread_ref, w_ref, h_ref, g_ref,
                      out_ref, lse_ref, slab_ref, merged_ref):
    o_refs = (o0_ref, o1_ref, o2_ref)
    l_refs = (l0_ref, l1_ref, l2_ref)
    n_groups = len(DIL_PATTERNS)
    tm = h_ref.shape[0]
    lses = []
    for gi in range(n_groups):
        dil = DIL_PATTERNS[gi][1]
        if dil == 1:
            lses.append(l_refs[gi][0, 0])
            continue
        for r in range(dil):
            rows = pl.ds(r, tm // dil, stride=dil)
            lse_ref[gi, rows, :] = l_refs[gi][0, r]
            for s in range(PAIRS):
                slab_ref[gi, s, rows, :] = o_refs[gi][0, r, :, s * LANES:(s + 1) * LANES].astype(F32)
        lses.append(lse_ref[gi])
    mx = functools.reduce(jnp.maximum, lses)
    ex = [jnp.exp(t - mx) for t in lses]
    den = functools.reduce(jnp.add, ex)
    wide = []
    for t in ex:
        a = t / den
        a_hi = a.astype(BF16)
        a_lo = (a - a_hi.astype(F32)).astype(BF16)
        wide.append(_dot(jnp.concatenate([a_hi, a_lo], axis=1), spread_ref[...]))
    for p in range(PAIRS):
        sl = slice(p * LANES, (p + 1) * LANES)
        acc = jnp.zeros((tm, LANES), F32)
        for gi in range(n_groups):
            if DIL_PATTERNS[gi][1] == 1:
                acc = acc + wide[gi][:, sl] * o_refs[gi][0, 0, :, sl].astype(F32)
            else:
                acc = acc + wide[gi][:, sl] * slab_ref[gi, p]
        merged_ref[:, sl] = acc.astype(BF16)
    out_ref[...] = h_ref[...] + _rms(_dot(merged_ref[...], w_ref[...]), g_ref[...])


def _oproj_merge(os_, lses, w, h, g, tm, batch, seq):
    n, d = h.shape
    row = lambda i: (i, 0)
    width = os_[0].shape[-1]
    per = seq // tm
    n_groups = len(DIL_PATTERNS)

    def grouped_spec(dil, cols):
        return pl.BlockSpec((1, dil, tm // dil, cols), lambda i: (i // per, 0, i % per, 0))

    dils = [dil for _, dil in DIL_PATTERNS]
    spread = np.zeros((2, LANES, width), np.float32)
    for hh in range(HEADS):
        spread[:, hh, hh * HEAD_DIM:(hh + 1) * HEAD_DIM] = 1.0
    spread = jnp.asarray(spread.reshape(2 * LANES, width), BF16)
    return pl.pallas_call(
        _oproj_merge_body,
        grid=(n // tm,),
        in_specs=[grouped_spec(dil, width) for dil in dils] + [grouped_spec(dil, LANES) for dil in dils]
        + [_const_spec(spread.shape), _const_spec(w.shape), pl.BlockSpec((tm, d), row), _const_spec((1, d))],
        out_specs=pl.BlockSpec((tm, d), row),
        out_shape=jax.ShapeDtypeStruct((n, d), F32),
        scratch_shapes=[pltpu.VMEM((n_groups, tm, LANES), F32),
                        pltpu.VMEM((n_groups, PAIRS, tm, LANES), F32),
                        pltpu.VMEM((tm, width), BF16)],
        compiler_params=_params(("parallel",)),
        name="oproj_merge",
    )(*os_, *lses, spread, w, h, g)


MXU_TILE = 256
FFN_SPLITS = (0, 6 * MXU_TILE, D_FF)


def _ffn_body(*refs, with_oproj):
    if with_oproj:
        o_ref, wattn_ref, g1_ref, *refs = refs
    h_ref, p_ref, g2_ref, g3_ref, win_ref, wo_ref, wproj_ref, wgate_ref, out_ref = refs
    h = h_ref[...]
    if with_oproj:
        h = h + _rms(_dot(o_ref[...], wattn_ref[0]), g1_ref[...])
    xn = _rms(h, g2_ref[...]).astype(BF16)
    y = None
    for a, b in zip(FFN_SPLITS[:-1], FFN_SPLITS[1:]):
        gate = _dot(xn, win_ref[0, :, a:b])
        up = _dot(xn, win_ref[0, :, D_FF + a:D_FF + b])
        act = (gate * jax.nn.sigmoid(gate) * up).astype(BF16)
        part = _dot(act, wo_ref[0, a:b, :])
        y = part if y is None else y + part
    h2 = h + _rms(y, g3_ref[...])
    emb = _dot(p_ref[0].astype(BF16), wproj_ref[0])
    out_ref[...] = h2 + emb * jax.nn.sigmoid(_dot(h2.astype(BF16), wgate_ref[0]))


def _layer_spec(stacked, layer):
    nd = stacked.ndim - 1
    return pl.BlockSpec((1,) + stacked.shape[1:], lambda *_: (layer,) + (0,) * nd,
                        pipeline_mode=pl.Buffered(1))


def _ffn(h, p, layer, g2, g3, w_in, w_out, w_proj, w_gate, tm, oproj=None):
    n, d = h.shape
    row = lambda i: (i, 0)
    head_specs, head_args = [], []
    if oproj is not None:
        o, w_attn, attn_layer, g1 = oproj
        head_specs = [pl.BlockSpec((tm, o.shape[1]), row), _layer_spec(w_attn, attn_layer),
                      _const_spec((1, d))]
        head_args = [o, w_attn, g1]
    return pl.pallas_call(
        functools.partial(_ffn_body, with_oproj=oproj is not None),
        grid=(n // tm,),
        in_specs=head_specs + [
            pl.BlockSpec((tm, d), row),
            pl.BlockSpec((1, tm, D_PLE), lambda i: (layer, i, 0)),
            _const_spec((1, d)),
            _const_spec((1, d)),
            _layer_spec(w_in, layer),
            _layer_spec(w_out, layer),
            _layer_spec(w_proj, layer),
            _layer_spec(w_gate, layer),
        ],
        out_specs=pl.BlockSpec((tm, d), row),
        out_shape=jax.ShapeDtypeStruct((n, d), F32),
        compiler_params=_params(("parallel",)),
        name="ffn",
    )(*head_args, h, p, g2, g3, w_in, w_out, w_proj, w_gate)


def _mla_weights(w_a, q_norm, kv_norm, w_uq, w_ukv):
    rank = MLA_Q_RANK + MLA_KV_RANK
    wa = jnp.pad(w_a, ((0, 0), (0, LANES - MLA_ROPE)))
    wa = jnp.concatenate([wa[:, :rank], jnp.roll(wa[:, rank:], MLA_NOPE, axis=1)], axis=1).astype(BF16)
    uq = w_uq.reshape(MLA_Q_RANK, HEADS, MLA_NOPE + MLA_ROPE)
    wuq = jnp.pad(uq, ((0, 0), (0, 0), (0, LANES - MLA_NOPE - MLA_ROPE)))
    wuq = wuq.reshape(MLA_Q_RANK, HEADS * LANES).astype(BF16)
    ukv = w_ukv.reshape(MLA_KV_RANK, HEADS, MLA_NOPE + HEAD_DIM)
    wuk = jnp.pad(ukv[:, :, :MLA_NOPE], ((0, 0), (0, 0), (0, LANES - MLA_NOPE)))
    wuk = wuk.reshape(MLA_KV_RANK, HEADS * LANES).astype(BF16)
    wuv = ukv[:, :, MLA_NOPE:].reshape(MLA_KV_RANK, HEADS * HEAD_DIM).astype(BF16)
    return dict(wa=wa, qn=q_norm.reshape(1, -1), kvn=kv_norm.reshape(1, -1),
                wuq=wuq, wuk=wuk, wuv=wuv)


def _q_scale_row(n_cols, q_starts, width, scale):
    row = np.ones((1, n_cols), np.float32)
    for s in q_starts:
        row[0, s:s + width] = scale
    return jnp.asarray(row)


def kernel(x, p, positions, norm_g, ffn_w_in, ffn_w_out, ple_w_proj, ple_w_gate, rel_bias, mla_w_a, mla_q_norm, mla_kv_norm, mla_w_uq, mla_w_ukv, mla_w_o, dil_w_qkv, dil_w_o, fox_w_qkvf, fox_b_f, fox_w_o):
    batch, seq, d = x.shape
    n = batch * seq
    inner = HEADS * HEAD_DIM
    h = x.reshape(n, d)
    rope_tabs = _rope_tables(positions, ROW_TILE)
    w_in, w_out = ffn_w_in.astype(BF16), ffn_w_out.astype(BF16)
    w_proj, w_gate = ple_w_proj.astype(BF16), ple_w_gate.astype(BF16)
    mla_wo, fox_wo = mla_w_o.astype(BF16), fox_w_o.astype(BF16)
    p_rows = p.reshape(N_LAYERS, n, D_PLE)
    for i in range(N_LAYERS):
        mixer, j = i % N_MIXERS, i // N_MIXERS
        g = norm_g[i].reshape(4, 1, d)
        if mixer == 0:
            w = _mla_weights(mla_w_a[j], mla_q_norm[j], mla_kv_norm[j], mla_w_uq[j], mla_w_ukv[j])
            q, k, vt = _mla_proj(h, rope_tabs, g[0], w, ROW_TILE, batch, seq)
            o = _flash(q.reshape(batch, seq, -1), k.reshape(batch, seq, -1), vt, batch, seq,
                       shared_lanes=False, k_off=0, tq=512)
            oproj = (o.reshape(n, inner), mla_wo, j, g[1])
        elif mixer == 1:
            table = rel_bias.reshape(REL_BUCKETS, len(DIL_PATTERNS), HEADS)
            h3, wq = h.reshape(batch, seq, d), dil_w_qkv[j].astype(BF16)
            outs, lses = [], []
            for gi in range(len(DIL_PATTERNS)):
                qk, vt = _dil_proj(h3, g[0], wq, gi)
                o, lse = _dil_attention(qk, vt, table[:, gi], gi, batch, seq)
                outs.append(o)
                lses.append(lse)
            h = _oproj_merge(outs, lses, dil_w_o[j].astype(BF16), h, g[1], ROW_TILE, batch, seq)
            oproj = None
        else:
            wq = fox_w_qkvf[j]
            w = dict(
                wqkv=wq[:, :3 * inner].astype(BF16),
                scale=_q_scale_row(2 * inner, [0], inner, HEAD_DIM ** -0.5 * LOG2E),
                wf=jnp.pad(wq[:, 3 * inner:], ((0, 0), (0, LANES - HEADS))).astype(BF16),
                bf=jnp.pad(fox_b_f[j], (0, LANES - HEADS)).reshape(1, LANES),
            )
            a, vt, logf = _fox_proj(h, g[0], w, ROW_TILE, batch, seq)
            c, ct = _cumsum(logf, batch, seq)
            a3 = a.reshape(batch, seq, 2 * inner)
            o = _flash(a3, a3, vt, batch, seq, shared_lanes=True, k_off=PAIRS, forget=(c, ct))
            oproj = (o.reshape(n, inner), fox_wo, j, g[1])
        h = _ffn(h, p_rows, i, g[2], g[3], w_in, w_out, w_proj, w_gate, tm=ROW_TILE, oproj=oproj)
    return h.reshape(batch, seq, d)
```
